```python
import math
import jax, jax.numpy as jnp
from jax import lax
import numpy as np

D_MODEL = 1024
BATCH = 4
SEQ = 4096
DEPTH = 2
DEC_BATCH = 128
DEC_SEQ = 4
PAST_LEN = 2048
PAGE_SIZE = 128

HEAD_DIM = 64
MIX_WIDTH = D_MODEL
BRANCH_WIDTH = MIX_WIDTH // 2
H_A = BRANCH_WIDTH // HEAD_DIM
H_B = BRANCH_WIDTH // HEAD_DIM
H_C = BRANCH_WIDTH // HEAD_DIM
CONV_CH = BRANCH_WIDTH
CONV_WIDTH = 31
MOBA_BLOCK = 256
MOBA_TOPK = 3
MOBA_Q_BLOCK = 64
ATTN_Q_BLOCK = 128
RET_CHUNK = 128
T5_BUCKETS = 32
T5_MAX_DIST = 128
ROPE_BASE = 10000.0
EPS = 1e-6
N_EVEN = (DEPTH + 1) // 2
N_ODD = DEPTH // 2
EVEN_IN = 8 * BRANCH_WIDTH
ODD_IN = 7 * BRANCH_WIDTH + H_C

kernel_name = "hybrid_moba_retention_fox_conformer_step"

F32 = jnp.float32


def rmsnorm(x, g):
    xf = x.astype(F32)
    y = xf * lax.rsqrt(jnp.mean(xf * xf, axis=-1, keepdims=True) + EPS)
    return (y * g.astype(F32)).astype(x.dtype)


def layernorm(x, g):
    xf = x.astype(F32)
    mu = jnp.mean(xf, axis=-1, keepdims=True)
    xc = xf - mu
    y = xc * lax.rsqrt(jnp.mean(xc * xc, axis=-1, keepdims=True) + EPS)
    return (y * g.astype(F32)).astype(x.dtype)


def rope(x, pos):
    half = x.shape[-1] // 2
    inv = ROPE_BASE ** (-jnp.arange(half, dtype=F32) / half)
    ang = pos.astype(F32)[:, None] * inv[None, :]
    cos, sin = jnp.cos(ang)[None, :, None, :], jnp.sin(ang)[None, :, None, :]
    xf = x.astype(F32)
    x1, x2 = xf[..., :half], xf[..., half:]
    return jnp.concatenate([x1 * cos - x2 * sin, x1 * sin + x2 * cos], axis=-1).astype(x.dtype)


def t5_bucket(rel):
    max_exact = T5_BUCKETS // 2
    relf = jnp.maximum(rel, 1).astype(F32)
    large = max_exact + (jnp.log(relf / max_exact) / math.log(T5_MAX_DIST / max_exact)
                         * (T5_BUCKETS - max_exact)).astype(jnp.int32)
    large = jnp.minimum(large, T5_BUCKETS - 1)
    return jnp.where(rel < max_exact, rel, large)


def moba_attend(q, k_all, v_all, t5_table):
    B, Lq, H, Dh = q.shape
    Lk = k_all.shape[1]
    q_start = Lk - Lq
    nb = -(-Lk // MOBA_BLOCK)
    pad = nb * MOBA_BLOCK - Lk
    kp = jnp.pad(k_all, ((0, 0), (0, pad), (0, 0), (0, 0))).reshape(B, nb, MOBA_BLOCK, H, Dh)
    vp = jnp.pad(v_all, ((0, 0), (0, pad), (0, 0), (0, 0))).reshape(B, nb, MOBA_BLOCK, H, Dh)
    k_mean = jnp.mean(kp.astype(F32), axis=2)
    kb = kp.transpose(0, 3, 1, 2, 4)
    vb = vp.transpose(0, 3, 1, 2, 4)
    n_sel = min(MOBA_TOPK, nb)
    qb = math.gcd(Lq, MOBA_Q_BLOCK)
    nq = Lq // qb
    q_blocks = q.reshape(B, nq, qb, H, Dh).transpose(1, 0, 2, 3, 4)
    pos_blocks = (q_start + jnp.arange(Lq, dtype=jnp.int32)).reshape(nq, qb)
    gather = jax.vmap(jax.vmap(lambda blocks, idx: blocks[idx]))
    head_ids = jnp.arange(H)[None, :, None, None, None]
    table = t5_table.astype(F32)
    scale = Dh ** -0.5

    def one_block(args):
        qc, pos = args
        qf = qc.astype(F32)
        own = pos // MOBA_BLOCK
        gate = jnp.einsum('bqhd,bnhd->bhqn', qf, k_mean)
        is_past = jnp.arange(nb)[None, :] < own[:, None]
        gate = jnp.where(is_past[None, None], gate, -jnp.inf)
        _, top_idx = lax.top_k(gate, n_sel)
        idx = jnp.concatenate([top_idx, jnp.broadcast_to(own[None, None, :, None], (B, H, qb, 1))], axis=-1)
        blk_ok = jnp.concatenate([jnp.arange(n_sel)[None, :] < own[:, None],
                                  jnp.ones((qb, 1), dtype=bool)], axis=-1)
        kg = gather(kb, idx)
        vg = gather(vb, idx)
        key_pos = idx[..., None] * MOBA_BLOCK + jnp.arange(MOBA_BLOCK)
        rel = pos[:, None, None] - key_pos
        ok = blk_ok[:, :, None] & (rel >= 0)
        bias = table[t5_bucket(jnp.maximum(rel, 0)), head_ids]
        logits = jnp.einsum('bqhd,bhqkjd->bhqkj', qf, kg.astype(F32)) * scale + bias
        logits = jnp.where(ok, logits, -jnp.inf)
        p = jax.nn.softmax(logits.reshape(B, H, qb, -1), axis=-1).astype(v_all.dtype)
        return jnp.einsum('bhqn,bhqnd->bqhd', p, vg.reshape(B, H, qb, -1, Dh))

    out = lax.map(one_block, (q_blocks, pos_blocks))
    return out.transpose(1, 0, 2, 3, 4).reshape(B, Lq, H, Dh)


def retention(q, k, v, s0):
    B, L, H, Dh = q.shape
    c = math.gcd(L, RET_CHUNK)
    n = L // c
    lg = jnp.log1p(-jnp.exp2(-5.0 - jnp.arange(H, dtype=F32)))
    i = jnp.arange(c, dtype=F32)
    diff = i[:, None] - i[None, :]
    intra = jnp.where(diff[None] >= 0, jnp.exp(lg[:, None, None] * diff[None]), 0.0)
    q_dec = jnp.exp(lg[None, :] * (i[:, None] + 1.0))
    k_dec = jnp.exp(lg[None, :] * (c - 1.0 - i[:, None]))
    c_dec = jnp.exp(lg * c)
    to_chunks = lambda t: t.astype(F32).reshape(B, n, c, H, Dh).transpose(1, 0, 2, 3, 4)

    def step(s, qkv):
        qc, kc, vc = qkv
        a = jnp.einsum('bihd,bjhd->bhij', qc, kc) * intra[None]
        o = (jnp.einsum('bhij,bjhe->bihe', a, vc)
             + jnp.einsum('bihd,bhde->bihe', qc * q_dec[None, :, :, None], s))
        s = s * c_dec[None, :, None, None] + jnp.einsum('bjhd,bjhe->bhde', kc * k_dec[None, :, :, None], vc)
        return s, o

    s, o = lax.scan(step, s0.astype(F32), (to_chunks(q), to_chunks(k), to_chunks(v)))
    return o.transpose(1, 0, 2, 3, 4).reshape(B, L, H, Dh), s


def fox_attend(q, k_all, v_all, fcum):
    B, Lq, H, Dh = q.shape
    Lk = k_all.shape[1]
    q_start = Lk - Lq
    qb = math.gcd(Lq, ATTN_Q_BLOCK)
    nq = Lq // qb
    kf = k_all.astype(F32)
    fk = fcum.transpose(0, 2, 1)
    key_pos = jnp.arange(Lk, dtype=jnp.int32)
    q_blocks = q.astype(F32).reshape(B, nq, qb, H, Dh).transpose(1, 0, 2, 3, 4)
    fq_blocks = fcum[:, q_start:].reshape(B, nq, qb, H).transpose(1, 0, 3, 2)
    pos_blocks = (q_start + jnp.arange(Lq, dtype=jnp.int32)).reshape(nq, qb)
    scale = Dh ** -0.5

    def one_block(args):
        qc, fqc, pos = args
        logits = jnp.einsum('bqhd,bkhd->bhqk', qc, kf) * scale + (fqc[..., None] - fk[:, :, None, :])
        logits = jnp.where((key_pos[None, :] <= pos[:, None])[None, None], logits, -jnp.inf)
        p = jax.nn.softmax(logits, axis=-1).astype(v_all.dtype)
        return jnp.einsum('bhqk,bkhd->bqhd', p, v_all)

    out = lax.map(one_block, (q_blocks, fq_blocks, pos_blocks))
    return out.transpose(1, 0, 2, 3, 4).reshape(B, Lq, H, Dh)


def causal_depthwise_conv(u, buf0, w, b):
    buf = jnp.concatenate([buf0.astype(u.dtype), u], axis=1)
    y = lax.conv_general_dilated(buf, w[:, None, :].astype(u.dtype), (1,), 'VALID',
                                 dimension_numbers=('NWC', 'WIO', 'NWC'),
                                 feature_group_count=u.shape[-1])
    return y + b.astype(u.dtype), buf[:, buf.shape[1] - (CONV_WIDTH - 1):]


def even_layer(x, past_k, past_v, ret_state, norm_g, w_in, q_gain, k_gain, t5_table, ret_gain, w_out):
    B, L, _ = x.shape
    P = past_k.shape[1]
    z = rmsnorm(x, norm_g) @ w_in
    qa, ka, va, ga, qr, kr, vr, gr = jnp.split(z, 8, axis=-1)
    heads = lambda t: t.reshape(B, L, -1, HEAD_DIM)
    qa = rmsnorm(heads(qa), q_gain)
    ka = rmsnorm(heads(ka), k_gain)
    va = heads(va)
    k_all = jnp.concatenate([past_k.astype(ka.dtype), ka], axis=1)
    v_all = jnp.concatenate([past_v.astype(va.dtype), va], axis=1)
    oa = moba_attend(qa, k_all, v_all, t5_table).reshape(B, L, BRANCH_WIDTH) * jax.nn.silu(ga)
    pos = P + jnp.arange(L, dtype=jnp.int32)
    qr = rope(heads(qr), pos)
    kr = rope(heads(kr), pos) * (HEAD_DIM ** -0.5)
    o_r, s_new = retention(qr, kr, heads(vr), ret_state)
    o_r = o_r - jnp.mean(o_r, axis=-1, keepdims=True)
    o_r = o_r * lax.rsqrt(jnp.mean(o_r * o_r, axis=-1, keepdims=True) + EPS)
    o_r = (o_r.reshape(B, L, BRANCH_WIDTH) * ret_gain.astype(F32)).astype(x.dtype) * jax.nn.silu(gr)
    y = jnp.concatenate([oa, o_r], axis=-1) @ w_out
    return x + y.astype(x.dtype), ka, va, s_new.astype(x.dtype)


def odd_layer(x, past_k, past_v, past_logf, conv_state, norm_g, w_in, b_forget, q_gain, k_gain,
              conv_w, conv_b, conv_gain, w_out):
    B, L, _ = x.shape
    W = BRANCH_WIDTH
    z = rmsnorm(x, norm_g) @ w_in
    qc, kc, vc, gc, fz, ua, ub, gd = jnp.split(
        z, [W, 2 * W, 3 * W, 4 * W, 4 * W + H_C, 5 * W + H_C, 6 * W + H_C], axis=-1)
    heads = lambda t: t.reshape(B, L, -1, HEAD_DIM)
    qc = rmsnorm(heads(qc), q_gain)
    kc = rmsnorm(heads(kc), k_gain)
    vc = heads(vc)
    logf = jax.nn.log_sigmoid(fz.astype(F32) + b_forget.astype(F32))
    fcum = jnp.cumsum(jnp.concatenate([past_logf.astype(F32), logf], axis=1), axis=1)
    k_all = jnp.concatenate([past_k.astype(kc.dtype), kc], axis=1)
    v_all = jnp.concatenate([past_v.astype(vc.dtype), vc], axis=1)
    oc = fox_attend(qc, k_all, v_all, fcum).reshape(B, L, W) * jax.nn.silu(gc)
    u = ua * jax.nn.sigmoid(ub)
    d, conv_new = causal_depthwise_conv(u, conv_state, conv_w, conv_b)
    d = jax.nn.silu(layernorm(d, conv_gain)) * jax.nn.silu(gd)
    y = jnp.concatenate([oc, d], axis=-1) @ w_out
    return x + y.astype(x.dtype), kc, vc, logf.astype(x.dtype), conv_new


def setup_inputs(seed: int = 0) -> dict:
    key = jax.random.key(seed)
    ks = jax.random.split(key, 32)
    n_pages = PAST_LEN // PAGE_SIZE
    n_used = DEC_BATCH * n_pages
    n_pool = n_used + max(1, n_used // 4)
    nrm = lambda k, shape: jax.random.normal(k, shape, dtype=F32)
    page_table = jax.random.permutation(ks[0], n_pool)[:n_used].reshape(DEC_BATCH, n_pages).astype(jnp.int32)
    return {
        "x_prompt": nrm(ks[1], (BATCH, SEQ, D_MODEL)),
        "x_sample": nrm(ks[2], (DEC_BATCH, DEC_SEQ, D_MODEL)),
        "cache_moba_k": nrm(ks[3], (N_EVEN, n_pool, PAGE_SIZE, H_A, HEAD_DIM)),
        "cache_moba_v": nrm(ks[4], (N_EVEN, n_pool, PAGE_SIZE, H_A, HEAD_DIM)),
        "cache_fox_k": nrm(ks[5], (N_ODD, n_pool, PAGE_SIZE, H_C, HEAD_DIM)),
        "cache_fox_v": nrm(ks[6], (N_ODD, n_pool, PAGE_SIZE, H_C, HEAD_DIM)),
        "cache_fox_logf": jax.nn.log_sigmoid(2.0 + nrm(ks[7], (N_ODD, n_pool, PAGE_SIZE, H_C))),
        "state_ret": 0.1 * nrm(ks[8], (N_EVEN, DEC_BATCH, H_B, HEAD_DIM, HEAD_DIM)),
        "state_conv": 0.5 * nrm(ks[9], (N_ODD, DEC_BATCH, CONV_WIDTH - 1, CONV_CH)),
        "page_table": page_table,
        "norm_g_even": 1.0 + 0.02 * nrm(ks[10], (N_EVEN, D_MODEL)),
        "w_in_even": nrm(ks[11], (N_EVEN, D_MODEL, EVEN_IN)) * D_MODEL ** -0.5,
        "moba_q_gain": 1.0 + 0.02 * nrm(ks[12], (N_EVEN, HEAD_DIM)),
        "moba_k_gain": 1.0 + 0.02 * nrm(ks[13], (N_EVEN, HEAD_DIM)),
        "t5_table": 0.5 * nrm(ks[14], (T5_BUCKETS, H_A)),
        "ret_gain": 1.0 + 0.02 * nrm(ks[15], (N_EVEN, BRANCH_WIDTH)),
        "w_out_even": nrm(ks[16], (N_EVEN, MIX_WIDTH, D_MODEL)) * MIX_WIDTH ** -0.5,
        "norm_g_odd": 1.0 + 0.02 * nrm(ks[17], (N_ODD, D_MODEL)),
        "w_in_odd": nrm(ks[18], (N_ODD, D_MODEL, ODD_IN)) * D_MODEL ** -0.5,
        "b_forget": 2.0 + 0.1 * nrm(ks[19], (N_ODD, H_C)),
        "fox_q_gain": 1.0 + 0.02 * nrm(ks[20], (N_ODD, HEAD_DIM)),
        "fox_k_gain": 1.0 + 0.02 * nrm(ks[21], (N_ODD, HEAD_DIM)),
        "conv_w": nrm(ks[22], (N_ODD, CONV_WIDTH, CONV_CH)) * CONV_WIDTH ** -0.5,
        "conv_b": 0.01 * nrm(ks[23], (N_ODD, CONV_CH)),
        "conv_gain": 1.0 + 0.02 * nrm(ks[24], (N_ODD, CONV_CH)),
        "w_out_odd": nrm(ks[25], (N_ODD, MIX_WIDTH, D_MODEL)) * MIX_WIDTH ** -0.5,
    }


def reference(x_prompt, x_sample, cache_moba_k, cache_moba_v, cache_fox_k, cache_fox_v, cache_fox_logf,
              state_ret, state_conv, page_table, norm_g_even, w_in_even, moba_q_gain, moba_k_gain, t5_table,
              ret_gain, w_out_even, norm_g_odd, w_in_odd, b_forget, fox_q_gain, fox_k_gain, conv_w, conv_b,
              conv_gain, w_out_odd):
    B = x_prompt.shape[0]
    Bd = x_sample.shape[0]
    past_len = page_table.shape[1] * PAGE_SIZE

    def paged(pool):
        g = pool[page_table]
        return g.reshape((Bd, past_len) + pool.shape[2:])

    xp, xs = x_prompt, x_sample
    mk_p, mv_p, mk_s, mv_s, rs_p, rs_s = [], [], [], [], [], []
    fk_p, fv_p, fl_p, fk_s, fv_s, fl_s, cs_p, cs_s = [], [], [], [], [], [], [], []
    for l in range(DEPTH):
        i = l // 2
        if l % 2 == 0:
            w = (norm_g_even[i], w_in_even[i], moba_q_gain[i], moba_k_gain[i], t5_table, ret_gain[i], w_out_even[i])
            xp, k1, v1, s1 = even_layer(xp, jnp.zeros((B, 0, H_A, HEAD_DIM), xp.dtype),
                                        jnp.zeros((B, 0, H_A, HEAD_DIM), xp.dtype),
                                        jnp.zeros((B, H_B, HEAD_DIM, HEAD_DIM), xp.dtype), *w)
            xs, k2, v2, s2 = even_layer(xs, paged(cache_moba_k[i]), paged(cache_moba_v[i]), state_ret[i], *w)
            mk_p.append(k1); mv_p.append(v1); rs_p.append(s1)
            mk_s.append(k2); mv_s.append(v2); rs_s.append(s2)
        else:
            w = (norm_g_odd[i], w_in_odd[i], b_forget[i], fox_q_gain[i], fox_k_gain[i],
                 conv_w[i], conv_b[i], conv_gain[i], w_out_odd[i])
            xp, k1, v1, f1, c1 = odd_layer(xp, jnp.zeros((B, 0, H_C, HEAD_DIM), xp.dtype),
                                           jnp.zeros((B, 0, H_C, HEAD_DIM), xp.dtype),
                                           jnp.zeros((B, 0, H_C), xp.dtype),
                                           jnp.zeros((B, CONV_WIDTH - 1, CONV_CH), xp.dtype), *w)
            xs, k2, v2, f2, c2 = odd_layer(xs, paged(cache_fox_k[i]), paged(cache_fox_v[i]),
                                           paged(cache_fox_logf[i]), state_conv[i], *w)
            fk_p.append(k1); fv_p.append(v1); fl_p.append(f1); cs_p.append(c1)
            fk_s.append(k2); fv_s.append(v2); fl_s.append(f2); cs_s.append(c2)
    return (xp, xs,
            jnp.stack(mk_p), jnp.stack(mv_p), jnp.stack(mk_s), jnp.stack(mv_s),
            jnp.stack(rs_p), jnp.stack(rs_s),
            jnp.stack(fk_p), jnp.stack(fv_p), jnp.stack(fl_p),
            jnp.stack(fk_s), jnp.stack(fv_s), jnp.stack(fl_s),
            jnp.stack(cs_p), jnp.stack(cs_s))
```

```python
import functools
import math

import numpy as np
import jax
import jax.numpy as jnp
from jax import lax
from jax.experimental import pallas as pl
from jax.experimental.pallas import tpu as pltpu

F32 = jnp.float32
BF16 = jnp.bfloat16
HIGHEST = lax.Precision.HIGHEST

HEAD_DIM = 64
PAIR = 2 * HEAD_DIM
SEG = 512
N_HEADS = SEG // HEAD_DIM
EPS = 1e-6
MOBA_BLOCK = 256
MOBA_TOPK = 3
T5_BUCKETS = 32
T5_MAX_DIST = 128
ROPE_BASE = 10000.0
CONV_WIDTH = 31
PAGE_SIZE = 128
NEG = -1e30
ATT_TILE = 256
V7X_VMEM_LIMIT = 48 * 1024 * 1024


def _t5_bucket_upper_bounds():
    max_exact = T5_BUCKETS // 2
    rel = np.arange(0, 4 * T5_MAX_DIST)
    relf = np.maximum(rel, 1).astype(np.float64)
    large = max_exact + np.trunc(np.log(relf / max_exact) / math.log(T5_MAX_DIST / max_exact)
                                 * (T5_BUCKETS - max_exact)).astype(np.int64)
    bucket = np.where(rel < max_exact, rel, np.minimum(large, T5_BUCKETS - 1))
    return tuple(int(rel[bucket <= b].max()) for b in range(T5_BUCKETS - 1))


T5_UPPER = _t5_bucket_upper_bounds()


def _cparams(n_axes):
    return pltpu.CompilerParams(dimension_semantics=("arbitrary",) * n_axes,
                                vmem_limit_bytes=V7X_VMEM_LIMIT)


def _silu(x):
    return x * jax.nn.sigmoid(x)


def _lo_mask():
    return lax.broadcasted_iota(jnp.int32, (1, PAIR), 1) < HEAD_DIM


def _pair_sum(x, lo):
    s0 = jnp.sum(jnp.where(lo, x, 0.0), axis=-1, keepdims=True)
    s1 = jnp.sum(jnp.where(lo, 0.0, x), axis=-1, keepdims=True)
    return jnp.where(lo, s0, s1)


def _pair_rmsnorm(x, gain, lo):
    ms = _pair_sum(x * x, lo) * (1.0 / HEAD_DIM)
    return x * lax.rsqrt(ms + EPS) * gain


def _t5_bias(rel, tab_ref, h):
    bias = jnp.full(rel.shape, tab_ref[T5_BUCKETS - 1, h], F32)
    for b in range(T5_BUCKETS - 2, -1, -1):
        bias = jnp.where(rel <= T5_UPPER[b], tab_ref[b, h], bias)
    return bias


def _top_blocks(gate, n_valid, own, width):
    lane = lax.broadcasted_iota(jnp.int32, gate.shape, 1)
    g = jnp.where(lane < n_valid, gate, -jnp.inf)
    sel = lane == own
    for _ in range(MOBA_TOPK):
        m = jnp.max(g, axis=-1, keepdims=True)
        idx = jnp.min(jnp.where(g == m, lane, width), axis=-1, keepdims=True)
        pick = (lane == idx) & (m > -jnp.inf)
        sel = sel | pick
        g = jnp.where(pick, -jnp.inf, g)
    return sel


def _nt_dot(a, b, precision=None):
    return lax.dot_general(a, b, (((1,), (1,)), ((), ())), precision=precision,
                           preferred_element_type=F32)


def _proj_in_kernel(x_ref, g_ref, w_ref, o_ref, xn_ref):
    j = pl.program_id(1)

    @pl.when(j == 0)
    def _():
        x = x_ref[...]
        ms = jnp.mean(x * x, axis=-1, keepdims=True)
        xn_ref[...] = (x * lax.rsqrt(ms + EPS) * g_ref[...]).astype(BF16)

    o_ref[0] = jnp.dot(xn_ref[...], w_ref[j], preferred_element_type=F32)


def _proj_in(x2d, gain, w3):
    n, d = x2d.shape
    nseg = w3.shape[0]
    tm = min(n, 1024)
    return pl.pallas_call(
        _proj_in_kernel,
        grid=(n // tm, nseg),
        in_specs=[pl.BlockSpec((tm, d), lambda i, j: (i, 0)),
                  pl.BlockSpec((1, d), lambda i, j: (0, 0)),
                  pl.BlockSpec((nseg, d, SEG), lambda i, j: (0, 0, 0))],
        out_specs=pl.BlockSpec((1, tm, SEG), lambda i, j: (j, i, 0)),
        out_shape=jax.ShapeDtypeStruct((nseg, n, SEG), F32),
        scratch_shapes=[pltpu.VMEM((tm, d), BF16)],
        compiler_params=_cparams(2),
        name="proj_in",
    )(x2d, gain.reshape(1, d), w3)


def _proj_out_kernel(a_ref, b_ref, x_ref, w_ref, o_ref):
    acc = jnp.dot(a_ref[...], w_ref[0], preferred_element_type=F32)
    acc = acc + jnp.dot(b_ref[...], w_ref[1], preferred_element_type=F32)
    o_ref[...] = x_ref[...] + acc


def _proj_out(a, b, x2d, w2):
    n, d = x2d.shape
    tm = min(n, 512)
    return pl.pallas_call(
        _proj_out_kernel,
        grid=(n // tm,),
        in_specs=[pl.BlockSpec((tm, SEG), lambda i: (i, 0)),
                  pl.BlockSpec((tm, SEG), lambda i: (i, 0)),
                  pl.BlockSpec((tm, d), lambda i: (i, 0)),
                  pl.BlockSpec((2, SEG, d), lambda i: (0, 0, 0))],
        out_specs=pl.BlockSpec((tm, d), lambda i: (i, 0)),
        out_shape=jax.ShapeDtypeStruct((n, d), F32),
        compiler_params=_cparams(1),
        name="proj_out",
    )(a, b, x2d, w2)


def _moba_prompt_kernel(tab_ref, q_ref, k_ref, v_ref, g_ref, qg_ref, kg_ref,
                        o_ref, ka_ref,
                        kn_ref, vb_ref, km_ref, bias_ref):
    p = pl.program_id(1)
    i = pl.program_id(2)
    t = ATT_TILE
    nblk = k_ref.shape[1] // t
    lo = _lo_mask()

    @pl.when(i == 0)
    def _prep():
        def body(c, carry):
            r0 = pl.multiple_of(c * t, t)
            kn = _pair_rmsnorm(k_ref[0, pl.ds(r0, t), :], kg_ref[...], lo)
            ka_ref[pl.ds(r0, t), :] = kn
            kn_ref[pl.ds(r0, t), :] = kn.astype(BF16)
            km_ref[pl.ds(c, 1), :] = jnp.sum(kn, axis=0, keepdims=True) * (1.0 / t)
            vb_ref[pl.ds(r0, t), :] = v_ref[0, pl.ds(r0, t), :].astype(BF16)
            return carry
        lax.fori_loop(0, nblk, body, 0)
        rel = (lax.broadcasted_iota(jnp.int32, (t, t), 0) - lax.broadcasted_iota(jnp.int32, (t, t), 1))
        for hh in range(2):
            h = 2 * p + hh
            bias_ref[hh, 0] = jnp.where(rel >= 0, _t5_bias(rel, tab_ref, h), NEG)
            bias_ref[hh, 1] = _t5_bias(rel + t, tab_ref, h)
            bias_ref[hh, 2] = jnp.full((t, t), tab_ref[T5_BUCKETS - 1, h], F32)

    qn = _pair_rmsnorm(q_ref[0], qg_ref[...], lo)
    km = km_ref[...]
    outs = []
    for hh in range(2):
        hm = lo if hh == 0 else jnp.logical_not(lo)
        qh = jnp.where(hm, qn, 0.0)
        gate = _nt_dot(qh, km, precision=HIGHEST)
        sel = _top_blocks(gate, i, i, nblk)
        selpen = jnp.where(sel, 0.0, NEG).astype(BF16)
        qb = (qh * (HEAD_DIM ** -0.5)).astype(BF16)

        def body(step, carry, hh=hh, qb=qb, selpen=selpen):
            m, l, acc = carry
            j = i - step
            r0 = pl.multiple_of(j * t, t)
            kb = kn_ref[pl.ds(r0, t), :]
            onehot = (lax.broadcasted_iota(jnp.int32, (nblk, t), 0) == j).astype(BF16)
            s = _nt_dot(qb, kb) + jnp.dot(selpen, onehot, preferred_element_type=F32)
            s = s + bias_ref[hh, jnp.minimum(step, 2)]
            m_new = jnp.maximum(m, jnp.max(s, axis=-1, keepdims=True))
            alpha = jnp.exp(m - m_new)
            pr = jnp.exp(s - m_new)
            l = alpha * l + jnp.sum(pr, axis=-1, keepdims=True)
            acc = alpha * acc + jnp.dot(pr.astype(BF16), vb_ref[pl.ds(r0, t), :],
                                        preferred_element_type=F32)
            return m_new, l, acc

        init = (jnp.full((t, 1), NEG, F32), jnp.zeros((t, 1), F32), jnp.zeros((t, PAIR), F32))
        _, l, acc = lax.fori_loop(0, i + 1, body, init)
        outs.append(acc / l)
    o = jnp.where(lo, outs[0], outs[1])
    o_ref[...] = (o * _silu(g_ref[0])).astype(BF16)


def _moba_prompt(z3, batch, seq, t5_table, q_gain, k_gain):
    n = batch * seq
    t = ATT_TILE
    nq = seq // t
    npair = SEG // PAIR
    gain2 = lambda g: jnp.tile(g.astype(F32), 2).reshape(1, PAIR)
    return pl.pallas_call(
        _moba_prompt_kernel,
        grid=(batch, npair, nq),
        in_specs=[pl.BlockSpec(memory_space=pltpu.SMEM),
                  pl.BlockSpec((1, t, PAIR), lambda b, p, i: (0, b * nq + i, p)),
                  pl.BlockSpec((1, seq, PAIR), lambda b, p, i: (1, b, p)),
                  pl.BlockSpec((1, seq, PAIR), lambda b, p, i: (2, b, p)),
                  pl.BlockSpec((1, t, PAIR), lambda b, p, i: (3, b * nq + i, p)),
                  pl.BlockSpec((1, PAIR), lambda b, p, i: (0, 0)),
                  pl.BlockSpec((1, PAIR), lambda b, p, i: (0, 0))],
        out_specs=[pl.BlockSpec((t, PAIR), lambda b, p, i: (b * nq + i, p)),
                   pl.BlockSpec((seq, PAIR), lambda b, p, i: (b, p))],
        out_shape=[jax.ShapeDtypeStruct((n, SEG), BF16),
                   jax.ShapeDtypeStruct((n, SEG), F32)],
        scratch_shapes=[pltpu.VMEM((seq, PAIR), BF16),
                        pltpu.VMEM((seq, PAIR), BF16),
                        pltpu.VMEM((seq // t, PAIR), F32),
                        pltpu.VMEM((2, 3, t, t), F32)],
        compiler_params=_cparams(3),
        name="moba_prompt",
    )(t5_table.astype(F32), z3, z3, z3, z3, gain2(q_gain), gain2(k_gain))


def _rope_pair(x, cos, sin_signed):
    up = pltpu.roll(x, PAIR - HEAD_DIM // 2, 1)
    dn = pltpu.roll(x, HEAD_DIM // 2, 1)
    lane = lax.broadcasted_iota(jnp.int32, (1, PAIR), 1)
    first_half = (lane % HEAD_DIM) < (HEAD_DIM // 2)
    return x * cos + jnp.where(first_half, up, dn) * sin_signed


def _retention_kernel(lg_ref, q_ref, k_ref, v_ref, g_ref, cos_ref, sin_ref, gain_ref, s0_ref,
                      o_ref, sout_ref, s_ref, *, groups, mm_dtype):
    p = pl.program_id(1)
    c = pl.program_id(2)
    n = q_ref.shape[0]
    glen = n // groups
    lo = _lo_mask()

    @pl.when(c == 0)
    def _():
        s_ref[...] = s0_ref[...]

    cos = cos_ref[...]
    sin = sin_ref[...]
    qr = _rope_pair(q_ref[...], cos, sin)
    kr = _rope_pair(k_ref[...], cos, sin) * (HEAD_DIM ** -0.5)
    vb = v_ref[...].astype(mm_dtype)
    lg0 = lg_ref[2 * p]
    lg1 = lg_ref[2 * p + 1]
    lg_lane = jnp.where(lo, lg0, lg1)
    row = lax.broadcasted_iota(jnp.int32, (n, 1), 0)
    pos = (row % glen).astype(F32)
    grp = row // glen
    q_dec = jnp.exp(lg_lane * (pos + 1.0))
    k_dec = jnp.exp(lg_lane * (float(glen - 1) - pos))
    qd = (qr * q_dec).astype(mm_dtype)
    o = jnp.zeros((n, PAIR), F32)
    for g in range(groups):
        og = jnp.dot(qd, s_ref[g].astype(mm_dtype), preferred_element_type=F32)
        o = o + (og if groups == 1 else jnp.where(grp == g, og, 0.0))
    ri = lax.broadcasted_iota(jnp.int32, (n, n), 0)
    ci = lax.broadcasted_iota(jnp.int32, (n, n), 1)
    causal = (ri >= ci) & ((ri // glen) == (ci // glen))
    dpos = jnp.where(causal, ri - ci, 0).astype(F32)
    krb = kr.astype(mm_dtype)
    for hh in range(2):
        hm = lo if hh == 0 else jnp.logical_not(lo)
        lgh = lg0 if hh == 0 else lg1
        intra = jnp.where(causal, jnp.exp(lgh * dpos), 0.0)
        a = _nt_dot(jnp.where(hm, qr, 0.0).astype(mm_dtype), krb) * intra
        oh = jnp.dot(a.astype(mm_dtype), vb, preferred_element_type=F32)
        o = o + jnp.where(hm, oh, 0.0)
    srow = lax.broadcasted_iota(jnp.int32, (PAIR, 1), 0) < HEAD_DIM
    c_dec = jnp.exp(jnp.where(srow, lg0, lg1) * float(glen))
    same_head = srow == lo
    kd = kr * k_dec
    for g in range(groups):
        kg = kd if groups == 1 else jnp.where(grp == g, kd, 0.0)
        kv = lax.dot_general(kg.astype(mm_dtype), vb, (((0,), (0,)), ((), ())),
                             preferred_element_type=F32)
        new_state = jnp.where(same_head, s_ref[g] * c_dec + kv, 0.0)
        s_ref[g] = new_state
        sout_ref[g] = new_state

    mu = _pair_sum(o, lo) * (1.0 / HEAD_DIM)
    oc = o - mu
    var = _pair_sum(oc * oc, lo) * (1.0 / HEAD_DIM)
    on = oc * lax.rsqrt(var + EPS) * gain_ref[...]
    o_ref[...] = (on * _silu(g_ref[...])).astype(BF16)


def _pair_blockdiag(s):
    b = s.shape[0]
    s = s.astype(F32).reshape(b, N_HEADS // 2, 2, HEAD_DIM, HEAD_DIM)
    z = jnp.zeros_like(s[:, :, 0])
    top = jnp.concatenate([s[:, :, 0], z], axis=-1)
    bot = jnp.concatenate([z, s[:, :, 1]], axis=-1)
    return jnp.concatenate([top, bot], axis=-2)


def _pair_unblockdiag(sp):
    b = sp.shape[0]
    s0 = sp[:, :, :HEAD_DIM, :HEAD_DIM]
    s1 = sp[:, :, HEAD_DIM:, HEAD_DIM:]
    return jnp.stack([s0, s1], axis=2).reshape(b, N_HEADS, HEAD_DIM, HEAD_DIM)


def _rope_tables(pos):
    half = HEAD_DIM // 2
    inv = ROPE_BASE ** (-jnp.arange(half, dtype=F32) / half)
    ang = pos.astype(F32)[:, None] * inv[None, :]
    cos, sin = jnp.cos(ang), jnp.sin(ang)
    cos_t = jnp.tile(cos, (1, PAIR // half))
    sin_t = jnp.tile(jnp.concatenate([-sin, sin], axis=-1), (1, 2))
    return cos_t, sin_t


def _log_gamma():
    return jnp.log1p(-jnp.exp2(-5.0 - jnp.arange(N_HEADS, dtype=F32)))


def _retention(z3, n_seq, seq, rows, groups, pos0, ret_gain, state_pairs, mm_dtype, name):
    n = n_seq * seq
    nc = (seq * groups) // rows
    nb = n_seq // groups
    npair = SEG // PAIR
    cos_t, sin_t = _rope_tables(pos0 + jnp.arange(seq, dtype=jnp.int32))
    if groups > 1:
        cos_t, sin_t = jnp.tile(cos_t, (groups, 1)), jnp.tile(sin_t, (groups, 1))
    z4 = z3.reshape(z3.shape[0], n // rows, rows, SEG)
    tok = lambda s: pl.BlockSpec((None, None, rows, PAIR), lambda b, p, c, s=s: (s, b * nc + c, 0, p))
    o, sp = pl.pallas_call(
        functools.partial(_retention_kernel, groups=groups, mm_dtype=mm_dtype),
        grid=(nb, npair, nc),
        in_specs=[pl.BlockSpec(memory_space=pltpu.SMEM),
                  tok(4), tok(5), tok(6), tok(7),
                  pl.BlockSpec((rows, PAIR), lambda b, p, c: (c, 0)),
                  pl.BlockSpec((rows, PAIR), lambda b, p, c: (c, 0)),
                  pl.BlockSpec((1, PAIR), lambda b, p, c: (0, p)),
                  pl.BlockSpec((groups, None, PAIR, PAIR), lambda b, p, c: (b, p, 0, 0))],
        out_specs=[pl.BlockSpec((None, rows, PAIR), lambda b, p, c: (b * nc + c, 0, p)),
                   pl.BlockSpec((groups, None, PAIR, PAIR), lambda b, p, c: (b, p, 0, 0))],
        out_shape=[jax.ShapeDtypeStruct((n // rows, rows, SEG), BF16),
                   jax.ShapeDtypeStruct((n_seq, npair, PAIR, PAIR), F32)],
        scratch_shapes=[pltpu.VMEM((groups, PAIR, PAIR), F32)],
        compiler_params=_cparams(3),
        name=name,
    )(_log_gamma(), z4, z4, z4, z4, cos_t, sin_t, ret_gain.astype(F32).reshape(1, SEG), state_pairs)
    return o.reshape(n, SEG), sp


def _even_weights(w_in, w_out):
    d = w_in.shape[0]
    w3 = w_in.reshape(d, 8, SEG).transpose(1, 0, 2).astype(BF16)
    w2 = w_out.reshape(2, SEG, d).astype(BF16)
    return w3, w2


def _even_layer_prompt(x, norm_g, w_in, q_gain, k_gain, t5_table, ret_gain, w_out):
    b, l, d = x.shape
    x2d = x.reshape(b * l, d)
    w3, w2 = _even_weights(w_in, w_out)
    z3 = _proj_in(x2d, norm_g, w3)
    oa, ka = _moba_prompt(z3, b, l, t5_table, q_gain, k_gain)
    zero_state = jnp.zeros((b, SEG // PAIR, PAIR, PAIR), F32)
    orr, sp = _retention(z3, b, l, min(l, 256), 1, 0, ret_gain, zero_state, BF16, "retention_prompt")
    y = _proj_out(oa, orr, x2d, w2)
    return (y.reshape(b, l, d), ka.reshape(b, l, N_HEADS, HEAD_DIM),
            z3[2].reshape(b, l, N_HEADS, HEAD_DIM), _pair_unblockdiag(sp))


def _log_sigmoid(x):
    y = -x
    return -(jnp.maximum(y, 0.0) + jnp.log1p(jnp.exp(-jnp.abs(y))))


def _logf_kernel(fz_ref, b_ref, lf_ref, fc_ref):
    seq = fz_ref.shape[1]
    t = ATT_TILE
    tri = (lax.broadcasted_iota(jnp.int32, (t, t), 0) >= lax.broadcasted_iota(jnp.int32, (t, t), 1)).astype(F32)
    carry = jnp.zeros((1, PAIR), F32)
    for c in range(seq // t):
        lf = _log_sigmoid(fz_ref[0, c * t:(c + 1) * t, :] + b_ref[...])
        cs = jnp.dot(tri, lf, precision=HIGHEST, preferred_element_type=F32) + carry
        lf_ref[0, c * t:(c + 1) * t, :] = lf[:, :N_HEADS]
        fc_ref[0, c * t:(c + 1) * t, :] = cs[:, :N_HEADS]
        carry = cs[t - 1:t, :]


def _logf_prompt(z3, batch, seq, b_forget):
    bpad = jnp.zeros((1, PAIR), F32).at[0, :N_HEADS].set(b_forget.astype(F32))
    return pl.pallas_call(
        _logf_kernel,
        grid=(batch,),
        in_specs=[pl.BlockSpec((1, seq, PAIR), lambda b: (7, b, 0)),
                  pl.BlockSpec((1, PAIR), lambda b: (0, 0))],
        out_specs=[pl.BlockSpec((1, seq, N_HEADS), lambda b: (b, 0, 0)),
                   pl.BlockSpec((1, seq, N_HEADS), lambda b: (b, 0, 0))],
        out_shape=[jax.ShapeDtypeStruct((batch, seq, N_HEADS), F32),
                   jax.ShapeDtypeStruct((batch, seq, N_HEADS), F32)],
        compiler_params=_cparams(1),
        name="logf_prompt",
    )(z3, bpad)


def _fox_prompt_kernel(q_ref, k_ref, v_ref, g_ref, fq_ref, fk_ref, qg_ref, kg_ref,
                       o_ref, kc_ref,
                       kn_ref, vb_ref):
    p = pl.program_id(1)
    i = pl.program_id(2)
    t = ATT_TILE
    nblk = k_ref.shape[1] // t
    lo = _lo_mask()

    @pl.when(i == 0)
    def _prep():
        def body(c, carry):
            r0 = pl.multiple_of(c * t, t)
            kn = _pair_rmsnorm(k_ref[0, pl.ds(r0, t), :], kg_ref[...], lo)
            kc_ref[pl.ds(r0, t), :] = kn
            kn_ref[pl.ds(r0, t), :] = kn.astype(BF16)
            vb_ref[pl.ds(r0, t), :] = v_ref[0, pl.ds(r0, t), :].astype(BF16)
            return carry
        lax.fori_loop(0, nblk, body, 0)

    qn = _pair_rmsnorm(q_ref[0], qg_ref[...], lo)
    causal = (lax.broadcasted_iota(jnp.int32, (t, t), 0) >= lax.broadcasted_iota(jnp.int32, (t, t), 1))
    fq_all = fq_ref[0]
    outs = []
    for hh in range(2):
        hm = lo if hh == 0 else jnp.logical_not(lo)
        h = 2 * p + hh
        qb = (jnp.where(hm, qn, 0.0) * (HEAD_DIM ** -0.5)).astype(BF16)
        fq = jnp.sum(jnp.where(lax.broadcasted_iota(jnp.int32, (1, N_HEADS), 1) == h, fq_all, 0.0),
                     axis=-1, keepdims=True)

        def tile(j, carry, own, qb=qb, fq=fq, h=h):
            m, l, acc = carry
            r0 = pl.multiple_of(j * t, t)
            fk = fk_ref[0, h, pl.ds(j, 1), :]
            s = _nt_dot(qb, kn_ref[pl.ds(r0, t), :]) + (fq - fk)
            if own:
                s = jnp.where(causal, s, NEG)
            m_new = jnp.maximum(m, jnp.max(s, axis=-1, keepdims=True))
            alpha = jnp.exp(m - m_new)
            pr = jnp.exp(s - m_new)
            l = alpha * l + jnp.sum(pr, axis=-1, keepdims=True)
            acc = alpha * acc + jnp.dot(pr.astype(BF16), vb_ref[pl.ds(r0, t), :],
                                        preferred_element_type=F32)
            return m_new, l, acc

        init = (jnp.full((t, 1), NEG, F32), jnp.zeros((t, 1), F32), jnp.zeros((t, PAIR), F32))
        carry = tile(i, init, True)
        _, l, acc = lax.fori_loop(0, i, lambda j, c: tile(j, c, False), carry)
        outs.append(acc / l)
    o = jnp.where(lo, outs[0], outs[1])
    o_ref[...] = (o * _silu(g_ref[0])).astype(BF16)


def _fox_prompt(z3, batch, seq, fcum, q_gain, k_gain):
    n = batch * seq
    t = ATT_TILE
    nq = seq // t
    npair = SEG // PAIR
    gain2 = lambda g: jnp.tile(g.astype(F32), 2).reshape(1, PAIR)
    fk = fcum.transpose(0, 2, 1).reshape(batch, N_HEADS, nq, t)
    return pl.pallas_call(
        _fox_prompt_kernel,
        grid=(batch, npair, nq),
        in_specs=[pl.BlockSpec((1, t, PAIR), lambda b, p, i: (0, b * nq + i, p)),
                  pl.BlockSpec((1, seq, PAIR), lambda b, p, i: (1, b, p)),
                  pl.BlockSpec((1, seq, PAIR), lambda b, p, i: (2, b, p)),
                  pl.BlockSpec((1, t, PAIR), lambda b, p, i: (3, b * nq + i, p)),
                  pl.BlockSpec((1, t, N_HEADS), lambda b, p, i: (b, i, 0)),
                  pl.BlockSpec((1, N_HEADS, nq, t), lambda b, p, i: (b, 0, 0, 0)),
                  pl.BlockSpec((1, PAIR), lambda b, p, i: (0, 0)),
                  pl.BlockSpec((1, PAIR), lambda b, p, i: (0, 0))],
        out_specs=[pl.BlockSpec((t, PAIR), lambda b, p, i: (b * nq + i, p)),
                   pl.BlockSpec((seq, PAIR), lambda b, p, i: (b, p))],
        out_shape=[jax.ShapeDtypeStruct((n, SEG), BF16),
                   jax.ShapeDtypeStruct((n, SEG), F32)],
        scratch_shapes=[pltpu.VMEM((seq, PAIR), BF16),
                        pltpu.VMEM((seq, PAIR), BF16)],
        compiler_params=_cparams(3),
        name="fox_prompt",
    )(z3, z3, z3, z3, fcum, fk, gain2(q_gain), gain2(k_gain))


CONV_HALO = 32
CONV_ROWS = 32


def _conv_prompt_kernel(ua_ref, ub_ref, gd_ref, uah_ref, ubh_ref, st_ref, w_ref, cb_ref, cg_ref,
                        d_ref, cs_ref, buf_ref, y_ref):
    ti = pl.program_id(1)
    nt = pl.num_programs(1)
    t = ua_ref.shape[1]
    pad = CONV_HALO - (CONV_WIDTH - 1)
    buf_ref[CONV_HALO:CONV_HALO + t, :] = ua_ref[0] * jax.nn.sigmoid(ub_ref[0])

    @pl.when(ti == 0)
    def _():
        buf_ref[0:CONV_HALO, :] = st_ref[0]

    @pl.when(ti > 0)
    def _():
        buf_ref[0:CONV_HALO, :] = uah_ref[0] * jax.nn.sigmoid(ubh_ref[0])

    for lg in range(SEG // PAIR):
        ls = slice(lg * PAIR, (lg + 1) * PAIR)
        for rc in range(t // CONV_ROWS):
            r0 = rc * CONV_ROWS
            acc = jnp.zeros((CONV_ROWS, PAIR), F32) + cb_ref[:, ls]
            for k in range(CONV_WIDTH):
                acc = acc + w_ref[k:k + 1, ls] * buf_ref[r0 + pad + k:r0 + pad + k + CONV_ROWS, ls]
            y_ref[r0:r0 + CONV_ROWS, ls] = acc

    y = y_ref[...]
    mu = jnp.mean(y, axis=-1, keepdims=True)
    yc = y - mu
    var = jnp.mean(yc * yc, axis=-1, keepdims=True)
    yn = yc * lax.rsqrt(var + EPS) * cg_ref[...]
    d_ref[...] = (_silu(yn) * _silu(gd_ref[0])).astype(BF16)

    @pl.when(ti == nt - 1)
    def _():
        cs_ref[0] = buf_ref[t + pad:t + CONV_HALO, :]


def _conv_prompt(z3, batch, seq, state, conv_w, conv_b, conv_gain):
    n = batch * seq
    t = min(seq, 256)
    nt = seq // t
    hb = t // CONV_HALO
    st = jnp.pad(state.astype(F32), ((0, 0), (CONV_HALO - (CONV_WIDTH - 1), 0), (0, 0)))
    tok = lambda s: pl.BlockSpec((1, t, SEG), lambda b, i, s=s: (s, b * nt + i, 0))
    halo = lambda s: pl.BlockSpec((1, CONV_HALO, SEG),
                                  lambda b, i, s=s: (s, jnp.maximum((b * nt + i) * hb - 1, 0), 0))
    row = pl.BlockSpec((1, SEG), lambda b, i: (0, 0))
    return pl.pallas_call(
        _conv_prompt_kernel,
        grid=(batch, nt),
        in_specs=[tok(4), tok(5), tok(6), halo(4), halo(5),
                  pl.BlockSpec((1, CONV_HALO, SEG), lambda b, i: (b, 0, 0)),
                  pl.BlockSpec((CONV_WIDTH, SEG), lambda b, i: (0, 0)), row, row],
        out_specs=[pl.BlockSpec((t, SEG), lambda b, i: (b * nt + i, 0)),
                   pl.BlockSpec((1, CONV_WIDTH - 1, SEG), lambda b, i: (b, 0, 0))],
        out_shape=[jax.ShapeDtypeStruct((n, SEG), BF16),
                   jax.ShapeDtypeStruct((batch, CONV_WIDTH - 1, SEG), F32)],
        scratch_shapes=[pltpu.VMEM((t + CONV_HALO, SEG), F32),
                        pltpu.VMEM((t, SEG), F32)],
        compiler_params=_cparams(2),
        name="conv_prompt",
    )(z3, z3, z3, z3, z3, st, conv_w.astype(F32), conv_b.astype(F32).reshape(1, SEG),
      conv_gain.astype(F32).reshape(1, SEG))


def _odd_weights(w_in, w_out):
    d = w_in.shape[0]
    w = SEG
    fz = w_in[:, 4 * w:4 * w + N_HEADS]
    rest = jnp.concatenate([w_in[:, :4 * w], w_in[:, 4 * w + N_HEADS:]], axis=1)
    fz_pad = jnp.pad(fz, ((0, 0), (0, w - N_HEADS)))
    w3 = jnp.concatenate([rest, fz_pad], axis=1).reshape(d, 8, w).transpose(1, 0, 2).astype(BF16)
    w2 = w_out.reshape(2, w, d).astype(BF16)
    return w3, w2


def _odd_layer_prompt(x, norm_g, w_in, b_forget, q_gain, k_gain, conv_w, conv_b, conv_gain, w_out):
    b, l, d = x.shape
    x2d = x.reshape(b * l, d)
    w3, w2 = _odd_weights(w_in, w_out)
    z3 = _proj_in(x2d, norm_g, w3)
    logf, fcum = _logf_prompt(z3, b, l, b_forget)
    oc, kc = _fox_prompt(z3, b, l, fcum, q_gain, k_gain)
    dd, cs = _conv_prompt(z3, b, l, jnp.zeros((b, CONV_WIDTH - 1, SEG), F32), conv_w, conv_b, conv_gain)
    y = _proj_out(oc, dd, x2d, w2)
    return (y.reshape(b, l, d), kc.reshape(b, l, N_HEADS, HEAD_DIM),
            z3[2].reshape(b, l, N_HEADS, HEAD_DIM), logf, cs)


def _heads_rmsnorm(x, gain):
    lo = _lo_mask()
    parts = [_pair_rmsnorm(x[:, g * PAIR:(g + 1) * PAIR], gain[:, g * PAIR:(g + 1) * PAIR], lo)
             for g in range(SEG // PAIR)]
    return jnp.concatenate(parts, axis=-1)


def _query_rows(qn, n_tok):
    sub = lax.broadcasted_iota(jnp.int32, (N_HEADS, SEG), 0)
    head_of_lane = lax.broadcasted_iota(jnp.int32, (N_HEADS, SEG), 1) // HEAD_DIM
    own = sub == head_of_lane
    return jnp.concatenate([jnp.where(own, jnp.broadcast_to(qn[q:q + 1, :], (N_HEADS, SEG)), 0.0)
                            for q in range(n_tok)], axis=0)


def _rows_to_tokens(rows_out, n_tok):
    sub = lax.broadcasted_iota(jnp.int32, (N_HEADS, SEG), 0)
    head_of_lane = lax.broadcasted_iota(jnp.int32, (N_HEADS, SEG), 1) // HEAD_DIM
    own = sub == head_of_lane
    tok = lax.broadcasted_iota(jnp.int32, (n_tok, SEG), 0)
    out = jnp.zeros((n_tok, SEG), F32)
    for q in range(n_tok):
        o_q = jnp.sum(jnp.where(own, rows_out[q * N_HEADS:(q + 1) * N_HEADS, :], 0.0), axis=0, keepdims=True)
        out = jnp.where(tok == q, o_q, out)
    return out


def _t5_bias_rows(rel, tab):
    bias = jnp.broadcast_to(tab[:, T5_BUCKETS - 1:T5_BUCKETS], rel.shape)
    for b in range(T5_BUCKETS - 2, -1, -1):
        bias = jnp.where(rel <= T5_UPPER[b], tab[:, b:b + 1], bias)
    return bias


def _finish_rows(s_past, s_new, v_pages, v_new, n_tok):
    m = jnp.max(s_past, axis=-1, keepdims=True)
    for col in s_new:
        m = jnp.maximum(m, col)
    p_past = jnp.exp(s_past - m)
    l = jnp.sum(p_past, axis=-1, keepdims=True)
    pb = p_past.astype(BF16)
    acc = jnp.zeros((s_past.shape[0], SEG), F32)
    for pg, v_ref in enumerate(v_pages):
        acc = acc + jnp.dot(pb[:, pg * PAGE_SIZE:(pg + 1) * PAGE_SIZE], v_ref[0].astype(BF16),
                            preferred_element_type=F32)
    for kj, col in enumerate(s_new):
        p_new = jnp.exp(col - m)
        l = l + p_new
        acc = acc + p_new * v_new[kj:kj + 1, :]
    return _rows_to_tokens(acc / l, n_tok)


def _moba_sample_kernel(pt_ref, tab_ref, q_ref, k_ref, v_ref, g_ref, qg_ref, kg_ref, *rest, n_pages, n_tok):
    k_pages = rest[:n_pages]
    v_pages = rest[n_pages:2 * n_pages]
    o_ref, ka_ref, km_ref = rest[2 * n_pages:]
    pages_per_block = MOBA_BLOCK // PAGE_SIZE
    nb_past = n_pages // pages_per_block
    past_len = n_pages * PAGE_SIZE
    n_rows = n_tok * N_HEADS

    qn = _heads_rmsnorm(q_ref[0, 0], qg_ref[...])
    kn = _heads_rmsnorm(k_ref[0, 0], kg_ref[...])
    ka_ref[0] = kn
    v_new = v_ref[0, 0]
    qrows = _query_rows(qn, n_tok)
    qb = (qrows * (HEAD_DIM ** -0.5)).astype(BF16)
    tab = tab_ref[...]
    row_q = lax.broadcasted_iota(jnp.int32, (n_rows, 1), 0) // N_HEADS

    for n in range(nb_past):
        tot = jnp.zeros((1, SEG), F32)
        for pg in range(n * pages_per_block, (n + 1) * pages_per_block):
            tot = tot + jnp.sum(k_pages[pg][0], axis=0, keepdims=True)
        km_ref[n:n + 1, :] = tot * (1.0 / MOBA_BLOCK)
    gate = _nt_dot(qrows, km_ref[...], precision=HIGHEST)
    selpen = jnp.where(_top_blocks(gate, nb_past, nb_past, nb_past), 0.0, NEG)

    far = tab[:, T5_BUCKETS - 1:T5_BUCKETS]
    lane = lax.broadcasted_iota(jnp.int32, (n_rows, PAGE_SIZE), 1)
    tiles = []
    for pg in range(n_pages):
        n = pg // pages_per_block
        s = _nt_dot(qb, k_pages[pg][0].astype(BF16))
        min_rel = past_len - (pg + 1) * PAGE_SIZE + 1
        if min_rel > T5_UPPER[-1]:
            bias = far
        else:
            bias = _t5_bias_rows(past_len + row_q - (pg * PAGE_SIZE + lane), tab)
        tiles.append(s + bias + selpen[:, n:n + 1])
    s_past = jnp.concatenate(tiles, axis=-1)

    s_new = []
    for kj in range(n_tok):
        dot = jnp.sum(qrows * kn[kj:kj + 1, :], axis=-1, keepdims=True) * (HEAD_DIM ** -0.5)
        rel = row_q - kj
        s_new.append(jnp.where(rel >= 0, dot + _t5_bias_rows(rel, tab), NEG))

    o = _finish_rows(s_past, s_new, v_pages, v_new, n_tok)
    o_ref[0] = (o * _silu(g_ref[0, 0])).astype(BF16)


def _page_specs(n_pages, width):
    return [pl.BlockSpec((1, PAGE_SIZE, width), lambda b, pt, pg=pg: (pt[b, pg], 0, 0)) for pg in range(n_pages)]


def _moba_sample(z3, n_seq, n_tok, cache_k, cache_v, page_table, t5_table, q_gain, k_gain):
    n_pages = page_table.shape[1]
    n_pool = cache_k.shape[0]
    assert MOBA_BLOCK % PAGE_SIZE == 0 and (n_pages * PAGE_SIZE) % MOBA_BLOCK == 0
    assert n_tok <= MOBA_BLOCK and n_pages // (MOBA_BLOCK // PAGE_SIZE) >= MOBA_TOPK
    z4 = z3.reshape(z3.shape[0], n_seq, n_tok, SEG)
    tok = lambda s: pl.BlockSpec((1, 1, n_tok, SEG), lambda b, pt, s=s: (s, b, 0, 0))
    row = pl.BlockSpec((1, SEG), lambda b, pt: (0, 0))
    gain8 = lambda g: jnp.tile(g.astype(F32), N_HEADS).reshape(1, SEG)
    tab_rows = jnp.tile(t5_table.astype(F32).T, (n_tok, 1))
    grid_spec = pltpu.PrefetchScalarGridSpec(
        num_scalar_prefetch=1,
        grid=(n_seq,),
        in_specs=[pl.BlockSpec((n_tok * N_HEADS, T5_BUCKETS), lambda b, pt: (0, 0)),
                  tok(0), tok(1), tok(2), tok(3), row, row]
                 + _page_specs(n_pages, SEG) + _page_specs(n_pages, SEG),
        out_specs=[pl.BlockSpec((1, n_tok, SEG), lambda b, pt: (b, 0, 0)),
                   pl.BlockSpec((1, n_tok, SEG), lambda b, pt: (b, 0, 0))],
        scratch_shapes=[pltpu.VMEM((n_pages * PAGE_SIZE // MOBA_BLOCK, SEG), F32)])
    ck = cache_k.reshape(n_pool, PAGE_SIZE, SEG)
    cv = cache_v.reshape(n_pool, PAGE_SIZE, SEG)
    return pl.pallas_call(
        functools.partial(_moba_sample_kernel, n_pages=n_pages, n_tok=n_tok),
        grid_spec=grid_spec,
        out_shape=[jax.ShapeDtypeStruct((n_seq, n_tok, SEG), BF16),
                   jax.ShapeDtypeStruct((n_seq, n_tok, SEG), F32)],
        compiler_params=_cparams(1),
        name="moba_sample",
    )(page_table, tab_rows, z4, z4, z4, z4, gain8(q_gain), gain8(k_gain),
      *([ck] * n_pages), *([cv] * n_pages))


def _fox_sample_kernel(pt_ref, q_ref, k_ref, v_ref, g_ref, fz_ref, fzr_ref, bf_ref, bfr_ref, qg_ref, kg_ref,
                       *rest, n_pages, n_tok):
    k_pages = rest[:n_pages]
    v_pages = rest[n_pages:2 * n_pages]
    f_pages = rest[2 * n_pages:3 * n_pages]
    o_ref, kc_ref, lf_ref = rest[3 * n_pages:]
    n_rows = n_tok * N_HEADS

    qn = _heads_rmsnorm(q_ref[0, 0], qg_ref[...])
    kn = _heads_rmsnorm(k_ref[0, 0], kg_ref[...])
    kc_ref[0] = kn
    v_new = v_ref[0, 0]
    lf_ref[0] = _log_sigmoid(fz_ref[0, 0][:, :N_HEADS] + bf_ref[...])
    qrows = _query_rows(qn, n_tok) * (HEAD_DIM ** -0.5)
    qb = qrows.astype(BF16)
    row_q = lax.broadcasted_iota(jnp.int32, (n_rows, 1), 0) // N_HEADS

    pick = (lax.broadcasted_iota(jnp.int32, (n_rows, N_HEADS), 0) % N_HEADS
            == lax.broadcasted_iota(jnp.int32, (n_rows, N_HEADS), 1)).astype(F32)
    upper = (lax.broadcasted_iota(jnp.int32, (PAGE_SIZE, PAGE_SIZE), 0)
             <= lax.broadcasted_iota(jnp.int32, (PAGE_SIZE, PAGE_SIZE), 1)).astype(F32)
    carry = jnp.zeros((n_rows, 1), F32)
    fk_tiles = []
    for pg in range(n_pages):
        rows_lf = _nt_dot(pick, f_pages[pg][0], precision=HIGHEST)
        cs = jnp.dot(rows_lf, upper, precision=HIGHEST, preferred_element_type=F32) + carry
        fk_tiles.append(cs)
        carry = cs[:, PAGE_SIZE - 1:PAGE_SIZE]
    lfr = _log_sigmoid(fzr_ref[0] + bfr_ref[...])
    cum_new = []
    run = carry
    for tk in range(n_tok):
        run = run + lfr[:, tk:tk + 1]
        cum_new.append(run)
    fq = cum_new[n_tok - 1]
    for tk in range(n_tok - 2, -1, -1):
        fq = jnp.where(row_q == tk, cum_new[tk], fq)

    tiles = [_nt_dot(qb, k_pages[pg][0].astype(BF16)) + (fq - fk_tiles[pg]) for pg in range(n_pages)]
    s_past = jnp.concatenate(tiles, axis=-1)
    s_new = []
    for kj in range(n_tok):
        dot = jnp.sum(qrows * kn[kj:kj + 1, :], axis=-1, keepdims=True)
        s_new.append(jnp.where(row_q >= kj, dot + (fq - cum_new[kj]), NEG))

    o = _finish_rows(s_past, s_new, v_pages, v_new, n_tok)
    o_ref[0] = (o * _silu(g_ref[0, 0])).astype(BF16)


def _fox_sample(z3, n_seq, n_tok, cache_k, cache_v, cache_logf, page_table, b_forget, q_gain, k_gain):
    n_pages = page_table.shape[1]
    n_pool = cache_k.shape[0]
    z4 = z3.reshape(z3.shape[0], n_seq, n_tok, SEG)
    tok = lambda s: pl.BlockSpec((1, 1, n_tok, SEG), lambda b, pt, s=s: (s, b, 0, 0))
    row = pl.BlockSpec((1, SEG), lambda b, pt: (0, 0))
    gain8 = lambda g: jnp.tile(g.astype(F32), N_HEADS).reshape(1, SEG)
    fz = z4[7, :, :, :N_HEADS]
    fz_rows = jnp.tile(fz.transpose(0, 2, 1), (1, n_tok, 1))
    bf = b_forget.astype(F32)
    grid_spec = pltpu.PrefetchScalarGridSpec(
        num_scalar_prefetch=1,
        grid=(n_seq,),
        in_specs=[tok(0), tok(1), tok(2), tok(3),
                  pl.BlockSpec((1, 1, n_tok, PAIR), lambda b, pt: (7, b, 0, 0)),
                  pl.BlockSpec((1, n_tok * N_HEADS, n_tok), lambda b, pt: (b, 0, 0)),
                  pl.BlockSpec((1, N_HEADS), lambda b, pt: (0, 0)),
                  pl.BlockSpec((n_tok * N_HEADS, 1), lambda b, pt: (0, 0)),
                  row, row]
                 + _page_specs(n_pages, SEG) + _page_specs(n_pages, SEG) + _page_specs(n_pages, N_HEADS),
        out_specs=[pl.BlockSpec((1, n_tok, SEG), lambda b, pt: (b, 0, 0)),
                   pl.BlockSpec((1, n_tok, SEG), lambda b, pt: (b, 0, 0)),
                   pl.BlockSpec((1, n_tok, N_HEADS), lambda b, pt: (b, 0, 0))])
    ck = cache_k.reshape(n_pool, PAGE_SIZE, SEG)
    cv = cache_v.reshape(n_pool, PAGE_SIZE, SEG)
    return pl.pallas_call(
        functools.partial(_fox_sample_kernel, n_pages=n_pages, n_tok=n_tok),
        grid_spec=grid_spec,
        out_shape=[jax.ShapeDtypeStruct((n_seq, n_tok, SEG), BF16),
                   jax.ShapeDtypeStruct((n_seq, n_tok, SEG), F32),
                   jax.ShapeDtypeStruct((n_seq, n_tok, N_HEADS), F32)],
        compiler_params=_cparams(1),
        name="fox_sample",
    )(page_table, z4, z4, z4, z4, z4, fz_rows, bf.reshape(1, N_HEADS),
      jnp.tile(bf, n_tok).reshape(n_tok * N_HEADS, 1), gain8(q_gain), gain8(k_gain),
      *([ck] * n_pages), *([cv] * n_pages), *([cache_logf.astype(F32)] * n_pages))


CONV_SAMPLE_BATCH = 8


def _conv_sample_kernel(ua_ref, ub_ref, gd_ref, st_ref, w_ref, cb_ref, cg_ref, d_ref, cs_ref, buf_ref):
    n_tok = ua_ref.shape[2]
    hist = CONV_WIDTH - 1
    for e in range(ua_ref.shape[1]):
        buf_ref[0:hist, :] = st_ref[e]
        buf_ref[hist:hist + n_tok, :] = ua_ref[0, e] * jax.nn.sigmoid(ub_ref[0, e])
        y = jnp.zeros((n_tok, SEG), F32) + cb_ref[...]
        for k in range(CONV_WIDTH):
            y = y + w_ref[k:k + 1, :] * buf_ref[k:k + n_tok, :]
        mu = jnp.mean(y, axis=-1, keepdims=True)
        yc = y - mu
        var = jnp.mean(yc * yc, axis=-1, keepdims=True)
        yn = yc * lax.rsqrt(var + EPS) * cg_ref[...]
        d_ref[e] = (_silu(yn) * _silu(gd_ref[0, e])).astype(BF16)
        cs_ref[e] = buf_ref[n_tok:n_tok + hist, :]


def _conv_sample(z3, n_seq, n_tok, state, conv_w, conv_b, conv_gain):
    bt = math.gcd(n_seq, CONV_SAMPLE_BATCH)
    hist = CONV_WIDTH - 1
    z4 = z3.reshape(z3.shape[0], n_seq, n_tok, SEG)
    tok = lambda s: pl.BlockSpec((1, bt, n_tok, SEG), lambda b, s=s: (s, b, 0, 0))
    row = pl.BlockSpec((1, SEG), lambda b: (0, 0))
    return pl.pallas_call(
        _conv_sample_kernel,
        grid=(n_seq // bt,),
        in_specs=[tok(4), tok(5), tok(6),
                  pl.BlockSpec((bt, hist, SEG), lambda b: (b, 0, 0)),
                  pl.BlockSpec((CONV_WIDTH, SEG), lambda b: (0, 0)), row, row],
        out_specs=[pl.BlockSpec((bt, n_tok, SEG), lambda b: (b, 0, 0)),
                   pl.BlockSpec((bt, hist, SEG), lambda b: (b, 0, 0))],
        out_shape=[jax.ShapeDtypeStruct((n_seq, n_tok, SEG), BF16),
                   jax.ShapeDtypeStruct((n_seq, hist, SEG), F32)],
        scratch_shapes=[pltpu.VMEM((hist + n_tok + 6, SEG), F32)],
        compiler_params=_cparams(1),
        name="conv_sample",
    )(z4, z4, z4, state.astype(F32), conv_w.astype(F32), conv_b.astype(F32).reshape(1, SEG),
      conv_gain.astype(F32).reshape(1, SEG))


RET_SAMPLE_GROUPS = 2


def _even_layer_sample(x, cache_k, cache_v, ret_state, page_table, norm_g, w_in, q_gain, k_gain, t5_table,
                       ret_gain, w_out):
    b, l, d = x.shape
    x2d = x.reshape(b * l, d)
    w3, w2 = _even_weights(w_in, w_out)
    z3 = _proj_in(x2d, norm_g, w3)
    oa, ka = _moba_sample(z3, b, l, cache_k, cache_v, page_table, t5_table, q_gain, k_gain)
    past_len = page_table.shape[1] * PAGE_SIZE
    groups = math.gcd(b, RET_SAMPLE_GROUPS)
    orr, sp = _retention(z3, b, l, l * groups, groups, past_len, ret_gain, _pair_blockdiag(ret_state), F32,
                         "retention_sample")
    y = _proj_out(oa.reshape(b * l, SEG), orr, x2d, w2)
    return (y.reshape(b, l, d), ka.reshape(b, l, N_HEADS, HEAD_DIM),
            z3[2].reshape(b, l, N_HEADS, HEAD_DIM), _pair_unblockdiag(sp))


def _odd_layer_sample(x, cache_k, cache_v, cache_logf, conv_state, page_table, norm_g, w_in, b_forget, q_gain,
                      k_gain, conv_w, conv_b, conv_gain, w_out):
    b, l, d = x.shape
    x2d = x.reshape(b * l, d)
    w3, w2 = _odd_weights(w_in, w_out)
    z3 = _proj_in(x2d, norm_g, w3)
    oc, kc, logf = _fox_sample(z3, b, l, cache_k, cache_v, cache_logf, page_table, b_forget, q_gain, k_gain)
    dd, cs = _conv_sample(z3, b, l, conv_state, conv_w, conv_b, conv_gain)
    y = _proj_out(oc.reshape(b * l, SEG), dd.reshape(b * l, SEG), x2d, w2)
    return (y.reshape(b, l, d), kc.reshape(b, l, N_HEADS, HEAD_DIM),
            z3[2].reshape(b, l, N_HEADS, HEAD_DIM), logf, cs)


def kernel(x_prompt, x_sample, cache_moba_k, cache_moba_v, cache_fox_k, cache_fox_v, cache_fox_logf, state_ret, state_conv, page_table, norm_g_even, w_in_even, moba_q_gain, moba_k_gain, t5_table, ret_gain, w_out_even, norm_g_odd, w_in_odd, b_forget, fox_q_gain, fox_k_gain, conv_w, conv_b, conv_gain, w_out_odd):
    depth = norm_g_even.shape[0] + norm_g_odd.shape[0]
    xp, xs = x_prompt, x_sample
    outs = {name: [] for name in ("mk_p", "mv_p", "mk_s", "mv_s", "rs_p", "rs_s",
                                  "fk_p", "fv_p", "fl_p", "fk_s", "fv_s", "fl_s", "cs_p", "cs_s")}
    for layer in range(depth):
        i = layer // 2
        if layer % 2 == 0:
            w = (norm_g_even[i], w_in_even[i], moba_q_gain[i], moba_k_gain[i], t5_table, ret_gain[i], w_out_even[i])
            xp, k1, v1, s1 = _even_layer_prompt(xp, *w)
            xs, k2, v2, s2 = _even_layer_sample(xs, cache_moba_k[i], cache_moba_v[i], state_ret[i], page_table, *w)
            for name, val in zip(("mk_p", "mv_p", "rs_p", "mk_s", "mv_s", "rs_s"), (k1, v1, s1, k2, v2, s2)):
                outs[name].append(val)
        else:
            w = (norm_g_odd[i], w_in_odd[i], b_forget[i], fox_q_gain[i], fox_k_gain[i],
                 conv_w[i], conv_b[i], conv_gain[i], w_out_odd[i])
            xp, k1, v1, f1, c1 = _odd_layer_prompt(xp, *w)
            xs, k2, v2, f2, c2 = _odd_layer_sample(xs, cache_fox_k[i], cache_fox_v[i], cache_fox_logf[i],
                                                   state_conv[i], page_table, *w)
            for name, val in zip(("fk_p", "fv_p", "fl_p", "cs_p", "fk_s", "fv_s", "fl_s", "cs_s"),
                                 (k1, v1, f1, c1, k2, v2, f2, c2)):
                outs[name].append(val)
    st = lambda name: jnp.stack(outs[name])
    return (xp, xs, st("mk_p"), st("mv_p"), st("mk_s"), st("mv_s"), st("rs_p"), st("rs_s"),
            st("fk_p"), st("fv_p"), st("fl_p"), st("fk_s"), st("fv_s"), st("fl_s"), st("cs_p"), st("cs_s"))
```

```python
import functools
import math

import numpy as np
import jax
import jax.numpy as jnp
from jax import lax
from jax.experimental import pallas as pl
from jax.experimental.pallas import tpu as pltpu

F32 = jnp.float32
BF16 = jnp.bfloat16
HIGHEST = lax.Precision.HIGHEST

HEAD_DIM = 64
PAIR = 2 * HEAD_DIM
SEG = 512
N_HEADS = SEG // HEAD_DIM
EPS = 1e-6
MOBA_BLOCK = 256
MOBA_TOPK = 3
T5_BUCKETS = 32
T5_MAX_DIST = 128
ROPE_BASE = 10000.0
CONV_WIDTH = 31
PAGE_SIZE = 128
NEG = -1e30
ATT_TILE = 256
V7X_VMEM_LIMIT = 48 * 1024 * 1024


def _t5_bucket_upper_bounds():
    max_exact = T5_BUCKETS // 2
    rel = np.arange(0, 4 * T5_MAX_DIST)
    relf = np.maximum(rel, 1).astype(np.float64)
    large = max_exact + np.trunc(np.log(relf / max_exact) / math.log(T5_MAX_DIST / max_exact)
                                 * (T5_BUCKETS - max_exact)).astype(np.int64)
    bucket = np.where(rel < max_exact, rel, np.minimum(large, T5_BUCKETS - 1))
    return tuple(int(rel[bucket <= b].max()) for b in range(T5_BUCKETS - 1))


T5_UPPER = _t5_bucket_upper_bounds()


def _cparams(n_axes):
    return pltpu.CompilerParams(dimension_semantics=("arbitrary",) * n_axes,
                                vmem_limit_bytes=V7X_VMEM_LIMIT)


def _silu(x):
    return x * jax.nn.sigmoid(x)


def _lo_mask():
    return lax.broadcasted_iota(jnp.int32, (1, PAIR), 1) < HEAD_DIM


def _pair_sum(x, lo):
    s0 = jnp.sum(jnp.where(lo, x, 0.0), axis=-1, keepdims=True)
    s1 = jnp.sum(jnp.where(lo, 0.0, x), axis=-1, keepdims=True)
    return jnp.where(lo, s0, s1)


def _pair_rmsnorm(x, gain, lo):
    ms = _pair_sum(x * x, lo) * (1.0 / HEAD_DIM)
    return x * lax.rsqrt(ms + EPS) * gain


def _t5_bias(rel, tab_ref, h):
    bias = jnp.full(rel.shape, tab_ref[T5_BUCKETS - 1, h], F32)
    for b in range(T5_BUCKETS - 2, -1, -1):
        bias = jnp.where(rel <= T5_UPPER[b], tab_ref[b, h], bias)
    return bias


def _top_blocks(gate, n_valid, own, width):
    lane = lax.broadcasted_iota(jnp.int32, gate.shape, 1)
    g = jnp.where(lane < n_valid, gate, -jnp.inf)
    sel = lane == own
    for _ in range(MOBA_TOPK):
        m = jnp.max(g, axis=-1, keepdims=True)
        idx = jnp.min(jnp.where(g == m, lane, width), axis=-1, keepdims=True)
        pick = (lane == idx) & (m > -jnp.inf)
        sel = sel | pick
        g = jnp.where(pick, -jnp.inf, g)
    return sel


def _nt_dot(a, b, precision=None):
    return lax.dot_general(a, b, (((1,), (1,)), ((), ())), precision=precision,
                           preferred_element_type=F32)


def _proj_in_kernel(x_ref, g_ref, w_ref, o_ref, xn_ref):
    j = pl.program_id(1)

    @pl.when(j == 0)
    def _():
        x = x_ref[...]
        ms = jnp.mean(x * x, axis=-1, keepdims=True)
        xn_ref[...] = (x * lax.rsqrt(ms + EPS) * g_ref[...]).astype(BF16)

    o_ref[0] = jnp.dot(xn_ref[...], w_ref[j], preferred_element_type=F32)


def _proj_in(x2d, gain, w3):
    n, d = x2d.shape
    nseg = w3.shape[0]
    tm = min(n, 1024)
    return pl.pallas_call(
        _proj_in_kernel,
        grid=(n // tm, nseg),
        in_specs=[pl.BlockSpec((tm, d), lambda i, j: (i, 0)),
                  pl.BlockSpec((1, d), lambda i, j: (0, 0)),
                  pl.BlockSpec((nseg, d, SEG), lambda i, j: (0, 0, 0))],
        out_specs=pl.BlockSpec((1, tm, SEG), lambda i, j: (j, i, 0)),
        out_shape=jax.ShapeDtypeStruct((nseg, n, SEG), F32),
        scratch_shapes=[pltpu.VMEM((tm, d), BF16)],
        compiler_params=_cparams(2),
        name="proj_in",
    )(x2d, gain.reshape(1, d), w3)


def _proj_out_kernel(a_ref, b_ref, x_ref, w_ref, o_ref):
    acc = jnp.dot(a_ref[...], w_ref[0], preferred_element_type=F32)
    acc = acc + jnp.dot(b_ref[...], w_ref[1], preferred_element_type=F32)
    o_ref[...] = x_ref[...] + acc


def _proj_out(a, b, x2d, w2):
    n, d = x2d.shape
    tm = min(n, 512)
    return pl.pallas_call(
        _proj_out_kernel,
        grid=(n // tm,),
        in_specs=[pl.BlockSpec((tm, SEG), lambda i: (i, 0)),
                  pl.BlockSpec((tm, SEG), lambda i: (i, 0)),
                  pl.BlockSpec((tm, d), lambda i: (i, 0)),
                  pl.BlockSpec((2, SEG, d), lambda i: (0, 0, 0))],
        out_specs=pl.BlockSpec((tm, d), lambda i: (i, 0)),
        out_shape=jax.ShapeDtypeStruct((n, d), F32),
        compiler_params=_cparams(1),
        name="proj_out",
    )(a, b, x2d, w2)


def _top_blocks_t(gate_t, n_valid, own, width):
    blk = lax.broadcasted_iota(jnp.int32, gate_t.shape, 0)
    g = jnp.where(blk < n_valid, gate_t, -jnp.inf)
    sel = blk == own
    for _ in range(MOBA_TOPK):
        m = jnp.max(g, axis=0, keepdims=True)
        idx = jnp.min(jnp.where(g == m, blk, width), axis=0, keepdims=True)
        pick = (blk == idx) & (m > -jnp.inf)
        sel = sel | pick
        g = jnp.where(pick, -jnp.inf, g)
    return sel


def _tn_dot(a, b):
    return lax.dot_general(a, b, (((0,), (0,)), ((), ())), preferred_element_type=F32)


def _online_softmax_step(carry, s, v):
    m, l, acc = carry
    m_new = jnp.maximum(m, jnp.max(s, axis=-1, keepdims=True))
    alpha = jnp.exp(m - m_new)
    pr = jnp.exp(s - m_new)
    l = alpha * l + jnp.sum(pr, axis=-1, keepdims=True)
    acc = alpha * acc + jnp.dot(pr.astype(BF16), v, preferred_element_type=F32)
    return m_new, l, acc


def _softmax_init(rows):
    return (jnp.full((rows, 1), NEG, F32), jnp.zeros((rows, 1), F32), jnp.zeros((rows, PAIR), F32))


def _flash_attention(n_tiles, logits_fn, v_tile_fn, rows):
    carry = _online_softmax_step(_softmax_init(rows), logits_fn(0, True), v_tile_fn(0))
    _, l, acc = lax.fori_loop(
        1, n_tiles, lambda step, c: _online_softmax_step(c, logits_fn(step, False), v_tile_fn(step)), carry)
    return acc / l


def _moba_prompt_kernel(tab_ref, q_ref, k_ref, v_ref, g_ref, qg_ref, kg_ref,
                        o_ref, ka_ref,
                        kn_ref, vb_ref, km_ref, bias_ref):
    p = pl.program_id(1)
    i = pl.program_id(2)
    t = ATT_TILE
    nblk = k_ref.shape[1] // t
    lo = _lo_mask()

    @pl.when(i == 0)
    def _prep():
        lane = lax.broadcasted_iota(jnp.int32, (t, PAIR), 1)
        kn_ref[0:t, 0:PAIR] = jnp.zeros((t, PAIR), BF16)
        kn_ref[0:t, PAIR:2 * PAIR] = (lane == nblk).astype(BF16)
        vb_ref[0:t, :] = jnp.zeros((t, PAIR), BF16)

        def body(c, carry):
            r0 = pl.multiple_of(c * t, t)
            kn = _pair_rmsnorm(k_ref[0, pl.ds(r0, t), :], kg_ref[...], lo)
            ka_ref[pl.ds(r0, t), :] = kn
            kn_ref[pl.ds(r0 + t, t), 0:PAIR] = kn.astype(BF16)
            kn_ref[pl.ds(r0 + t, t), PAIR:2 * PAIR] = (lane == c).astype(BF16)
            km_ref[pl.ds(c, 1), :] = jnp.sum(kn, axis=0, keepdims=True) * (1.0 / t)
            vb_ref[pl.ds(r0 + t, t), :] = v_ref[0, pl.ds(r0, t), :].astype(BF16)
            return carry
        lax.fori_loop(0, nblk, body, 0)
        rel = (lax.broadcasted_iota(jnp.int32, (t, t), 0) - lax.broadcasted_iota(jnp.int32, (t, t), 1))
        for hh in range(2):
            h = 2 * p + hh
            bias_ref[0, hh * t:(hh + 1) * t, 0:t] = _t5_bias(rel + t, tab_ref, h)
            bias_ref[0, hh * t:(hh + 1) * t, t:2 * t] = jnp.where(rel >= 0, _t5_bias(rel, tab_ref, h), NEG)
            bias_ref[1, hh * t:(hh + 1) * t, :] = jnp.full((t, 2 * t), tab_ref[T5_BUCKETS - 1, h], F32)

    qn = _pair_rmsnorm(q_ref[0], qg_ref[...], lo)
    km = km_ref[...]
    no_block = jnp.where(lax.broadcasted_iota(jnp.int32, (PAIR - nblk, t), 0) == 0, NEG, 0.0)
    qs, pens = [], []
    for hh in range(2):
        hm = lo if hh == 0 else jnp.logical_not(lo)
        qh = jnp.where(hm, qn, 0.0)
        gate_t = _nt_dot(km, qh, precision=HIGHEST)
        sel = _top_blocks_t(gate_t, i, i, nblk)
        pen_t = jnp.concatenate([jnp.where(sel, 0.0, NEG), no_block], axis=0)
        pens.append(pen_t.T)
        qs.append(qh * (HEAD_DIM ** -0.5))
    q2 = jnp.concatenate([jnp.concatenate(qs, axis=0), jnp.concatenate(pens, axis=0)], axis=1).astype(BF16)

    def tile_rows(step):
        return pl.ds(pl.multiple_of((i - 2 * step) * t, t), 2 * t)

    def logits(step, first):
        return _nt_dot(q2, kn_ref[tile_rows(step), :]) + bias_ref[0 if first else 1]

    o = _flash_attention((i + 2) // 2, logits, lambda step: vb_ref[tile_rows(step), :], 2 * t)
    o = jnp.where(lo, o[0:t], o[t:2 * t])
    o_ref[...] = (o * _silu(g_ref[0])).astype(BF16)


def _moba_prompt(z3, batch, seq, t5_table, q_gain, k_gain):
    n = batch * seq
    t = ATT_TILE
    nq = seq // t
    npair = SEG // PAIR
    gain2 = lambda g: jnp.tile(g.astype(F32), 2).reshape(1, PAIR)
    return pl.pallas_call(
        _moba_prompt_kernel,
        grid=(batch, npair, nq),
        in_specs=[pl.BlockSpec(memory_space=pltpu.SMEM),
                  pl.BlockSpec((1, t, PAIR), lambda b, p, i: (0, b * nq + i, p)),
                  pl.BlockSpec((1, seq, PAIR), lambda b, p, i: (1, b, p)),
                  pl.BlockSpec((1, seq, PAIR), lambda b, p, i: (2, b, p)),
                  pl.BlockSpec((1, t, PAIR), lambda b, p, i: (3, b * nq + i, p)),
                  pl.BlockSpec((1, PAIR), lambda b, p, i: (0, 0)),
                  pl.BlockSpec((1, PAIR), lambda b, p, i: (0, 0))],
        out_specs=[pl.BlockSpec((t, PAIR), lambda b, p, i: (b * nq + i, p)),
                   pl.BlockSpec((seq, PAIR), lambda b, p, i: (b, p))],
        out_shape=[jax.ShapeDtypeStruct((n, SEG), BF16),
                   jax.ShapeDtypeStruct((n, SEG), F32)],
        scratch_shapes=[pltpu.VMEM((seq + t, 2 * PAIR), BF16),
                        pltpu.VMEM((seq + t, PAIR), BF16),
                        pltpu.VMEM((seq // t, PAIR), F32),
                        pltpu.VMEM((2, 2 * t, 2 * t), F32)],
        compiler_params=_cparams(3),
        name="moba_prompt",
    )(t5_table.astype(F32), z3, z3, z3, z3, gain2(q_gain), gain2(k_gain))


def _rope_pair(x, cos, sin_signed):
    up = pltpu.roll(x, PAIR - HEAD_DIM // 2, 1)
    dn = pltpu.roll(x, HEAD_DIM // 2, 1)
    lane = lax.broadcasted_iota(jnp.int32, (1, PAIR), 1)
    first_half = (lane % HEAD_DIM) < (HEAD_DIM // 2)
    return x * cos + jnp.where(first_half, up, dn) * sin_signed


def _retention_kernel(lg_ref, q_ref, k_ref, v_ref, g_ref, cos_ref, sin_ref, gain_ref, s0_ref,
                      o_ref, sout_ref, s_ref, *, groups, mm_dtype):
    p = pl.program_id(1)
    c = pl.program_id(2)
    n = q_ref.shape[0]
    glen = n // groups
    lo = _lo_mask()

    @pl.when(c == 0)
    def _():
        s_ref[...] = s0_ref[...]

    cos = cos_ref[...]
    sin = sin_ref[...]
    qr = _rope_pair(q_ref[...], cos, sin)
    kr = _rope_pair(k_ref[...], cos, sin) * (HEAD_DIM ** -0.5)
    vb = v_ref[...].astype(mm_dtype)
    lg0 = lg_ref[2 * p]
    lg1 = lg_ref[2 * p + 1]
    lg_lane = jnp.where(lo, lg0, lg1)
    row = lax.broadcasted_iota(jnp.int32, (n, 1), 0)
    pos = (row % glen).astype(F32)
    grp = row // glen
    q_dec = jnp.exp(lg_lane * (pos + 1.0))
    k_dec = jnp.exp(lg_lane * (float(glen - 1) - pos))
    qd = (qr * q_dec).astype(mm_dtype)
    o = jnp.zeros((n, PAIR), F32)
    for g in range(groups):
        og = jnp.dot(qd, s_ref[g].astype(mm_dtype), preferred_element_type=F32)
        o = o + (og if groups == 1 else jnp.where(grp == g, og, 0.0))
    ri = lax.broadcasted_iota(jnp.int32, (n, n), 0)
    ci = lax.broadcasted_iota(jnp.int32, (n, n), 1)
    causal = (ri >= ci) & ((ri // glen) == (ci // glen))
    dpos = jnp.where(causal, ri - ci, 0).astype(F32)
    krb = kr.astype(mm_dtype)
    for hh in range(2):
        hm = lo if hh == 0 else jnp.logical_not(lo)
        lgh = lg0 if hh == 0 else lg1
        intra = jnp.where(causal, jnp.exp(lgh * dpos), 0.0)
        a = _nt_dot(jnp.where(hm, qr, 0.0).astype(mm_dtype), krb) * intra
        oh = jnp.dot(a.astype(mm_dtype), vb, preferred_element_type=F32)
        o = o + jnp.where(hm, oh, 0.0)
    srow = lax.broadcasted_iota(jnp.int32, (PAIR, 1), 0) < HEAD_DIM
    c_dec = jnp.exp(jnp.where(srow, lg0, lg1) * float(glen))
    same_head = srow == lo
    kd = kr * k_dec
    for g in range(groups):
        kg = kd if groups == 1 else jnp.where(grp == g, kd, 0.0)
        kv = lax.dot_general(kg.astype(mm_dtype), vb, (((0,), (0,)), ((), ())),
                             preferred_element_type=F32)
        new_state = jnp.where(same_head, s_ref[g] * c_dec + kv, 0.0)
        s_ref[g] = new_state
        sout_ref[g] = new_state

    mu = _pair_sum(o, lo) * (1.0 / HEAD_DIM)
    oc = o - mu
    var = _pair_sum(oc * oc, lo) * (1.0 / HEAD_DIM)
    on = oc * lax.rsqrt(var + EPS) * gain_ref[...]
    o_ref[...] = (on * _silu(g_ref[...])).astype(BF16)


def _pair_blockdiag(s):
    b = s.shape[0]
    s = s.astype(F32).reshape(b, N_HEADS // 2, 2, HEAD_DIM, HEAD_DIM)
    z = jnp.zeros_like(s[:, :, 0])
    top = jnp.concatenate([s[:, :, 0], z], axis=-1)
    bot = jnp.concatenate([z, s[:, :, 1]], axis=-1)
    return jnp.concatenate([top, bot], axis=-2)


def _pair_unblockdiag(sp):
    b = sp.shape[0]
    s0 = sp[:, :, :HEAD_DIM, :HEAD_DIM]
    s1 = sp[:, :, HEAD_DIM:, HEAD_DIM:]
    return jnp.stack([s0, s1], axis=2).reshape(b, N_HEADS, HEAD_DIM, HEAD_DIM)


def _rope_tables(pos):
    half = HEAD_DIM // 2
    inv = ROPE_BASE ** (-jnp.arange(half, dtype=F32) / half)
    ang = pos.astype(F32)[:, None] * inv[None, :]
    cos, sin = jnp.cos(ang), jnp.sin(ang)
    cos_t = jnp.tile(cos, (1, PAIR // half))
    sin_t = jnp.tile(jnp.concatenate([-sin, sin], axis=-1), (1, 2))
    return cos_t, sin_t


def _log_gamma():
    return jnp.log1p(-jnp.exp2(-5.0 - jnp.arange(N_HEADS, dtype=F32)))


def _retention(z3, n_seq, seq, rows, groups, pos0, ret_gain, state_pairs, mm_dtype, name):
    n = n_seq * seq
    nc = (seq * groups) // rows
    nb = n_seq // groups
    npair = SEG // PAIR
    cos_t, sin_t = _rope_tables(pos0 + jnp.arange(seq, dtype=jnp.int32))
    if groups > 1:
        cos_t, sin_t = jnp.tile(cos_t, (groups, 1)), jnp.tile(sin_t, (groups, 1))
    z4 = z3.reshape(z3.shape[0], n // rows, rows, SEG)
    tok = lambda s: pl.BlockSpec((None, None, rows, PAIR), lambda b, p, c, s=s: (s, b * nc + c, 0, p))
    o, sp = pl.pallas_call(
        functools.partial(_retention_kernel, groups=groups, mm_dtype=mm_dtype),
        grid=(nb, npair, nc),
        in_specs=[pl.BlockSpec(memory_space=pltpu.SMEM),
                  tok(4), tok(5), tok(6), tok(7),
                  pl.BlockSpec((rows, PAIR), lambda b, p, c: (c, 0)),
                  pl.BlockSpec((rows, PAIR), lambda b, p, c: (c, 0)),
                  pl.BlockSpec((1, PAIR), lambda b, p, c: (0, p)),
                  pl.BlockSpec((groups, None, PAIR, PAIR), lambda b, p, c: (b, p, 0, 0))],
        out_specs=[pl.BlockSpec((None, rows, PAIR), lambda b, p, c: (b * nc + c, 0, p)),
                   pl.BlockSpec((groups, None, PAIR, PAIR), lambda b, p, c: (b, p, 0, 0))],
        out_shape=[jax.ShapeDtypeStruct((n // rows, rows, SEG), BF16),
                   jax.ShapeDtypeStruct((n_seq, npair, PAIR, PAIR), F32)],
        scratch_shapes=[pltpu.VMEM((groups, PAIR, PAIR), F32)],
        compiler_params=_cparams(3),
        name=name,
    )(_log_gamma(), z4, z4, z4, z4, cos_t, sin_t, ret_gain.astype(F32).reshape(1, SEG), state_pairs)
    return o.reshape(n, SEG), sp


def _even_weights(w_in, w_out):
    d = w_in.shape[0]
    w3 = w_in.reshape(d, 8, SEG).transpose(1, 0, 2).astype(BF16)
    w2 = w_out.reshape(2, SEG, d).astype(BF16)
    return w3, w2


def _even_layer_prompt(x, norm_g, w_in, q_gain, k_gain, t5_table, ret_gain, w_out):
    b, l, d = x.shape
    x2d = x.reshape(b * l, d)
    w3, w2 = _even_weights(w_in, w_out)
    z3 = _proj_in(x2d, norm_g, w3)
    oa, ka = _moba_prompt(z3, b, l, t5_table, q_gain, k_gain)
    zero_state = jnp.zeros((b, SEG // PAIR, PAIR, PAIR), F32)
    orr, sp = _retention(z3, b, l, min(l, 256), 1, 0, ret_gain, zero_state, BF16, "retention_prompt")
    y = _proj_out(oa, orr, x2d, w2)
    return (y.reshape(b, l, d), ka.reshape(b, l, N_HEADS, HEAD_DIM),
            z3[2].reshape(b, l, N_HEADS, HEAD_DIM), _pair_unblockdiag(sp))


def _log_sigmoid(x):
    y = -x
    return -(jnp.maximum(y, 0.0) + jnp.log1p(jnp.exp(-jnp.abs(y))))


def _logf_kernel(fz_ref, b_ref, lf_ref, fc_ref):
    seq = fz_ref.shape[1]
    t = ATT_TILE
    tri = (lax.broadcasted_iota(jnp.int32, (t, t), 0) >= lax.broadcasted_iota(jnp.int32, (t, t), 1)).astype(F32)
    carry = jnp.zeros((1, PAIR), F32)
    for c in range(seq // t):
        lf = _log_sigmoid(fz_ref[0, c * t:(c + 1) * t, :] + b_ref[...])
        cs = jnp.dot(tri, lf, precision=HIGHEST, preferred_element_type=F32) + carry
        lf_ref[0, c * t:(c + 1) * t, :] = lf[:, :N_HEADS]
        fc_ref[0, c * t:(c + 1) * t, :] = cs[:, :N_HEADS]
        carry = cs[t - 1:t, :]


def _logf_prompt(z3, batch, seq, b_forget):
    bpad = jnp.zeros((1, PAIR), F32).at[0, :N_HEADS].set(b_forget.astype(F32))
    return pl.pallas_call(
        _logf_kernel,
        grid=(batch,),
        in_specs=[pl.BlockSpec((1, seq, PAIR), lambda b: (7, b, 0)),
                  pl.BlockSpec((1, PAIR), lambda b: (0, 0))],
        out_specs=[pl.BlockSpec((1, seq, N_HEADS), lambda b: (b, 0, 0)),
                   pl.BlockSpec((1, seq, N_HEADS), lambda b: (b, 0, 0))],
        out_shape=[jax.ShapeDtypeStruct((batch, seq, N_HEADS), F32),
                   jax.ShapeDtypeStruct((batch, seq, N_HEADS), F32)],
        compiler_params=_cparams(1),
        name="logf_prompt",
    )(z3, bpad)


def _lane_split3(x):
    hi = x.astype(BF16).astype(F32)
    rest = x - hi
    mid = rest.astype(BF16).astype(F32)
    r = lax.broadcasted_iota(jnp.int32, x.shape, 1) % 3
    return jnp.where(r == 0, hi, jnp.where(r == 1, mid, rest - mid))


def _fox_prompt_kernel(q_ref, k_ref, v_ref, g_ref, fq_ref, fk_ref, qg_ref, kg_ref,
                       o_ref, kc_ref,
                       kn_ref, vb_ref):
    p = pl.program_id(1)
    i = pl.program_id(2)
    t = ATT_TILE
    nblk = k_ref.shape[1] // t
    lo = _lo_mask()
    lane = lax.broadcasted_iota(jnp.int32, (t, PAIR), 1)
    prow = lax.broadcasted_iota(jnp.int32, (N_HEADS, PAIR), 0)
    plane = lax.broadcasted_iota(jnp.int32, (N_HEADS, PAIR), 1)

    @pl.when(i == 0)
    def _prep():
        kn_ref[0:t, 0:PAIR] = jnp.zeros((t, PAIR), BF16)
        kn_ref[0:t, PAIR:2 * PAIR] = jnp.where((lane == 0) | (lane == 3), NEG, 0.0).astype(BF16)
        vb_ref[0:t, :] = jnp.zeros((t, PAIR), BF16)
        place = jnp.where(((prow == 2 * p) & (plane < 3)) | ((prow == 2 * p + 1) & (plane >= 3) & (plane < 6)),
                          -1.0, 0.0)

        def body(c, carry):
            r0 = pl.multiple_of(c * t, t)
            kn = _pair_rmsnorm(k_ref[0, pl.ds(r0, t), :], kg_ref[...], lo)
            kc_ref[pl.ds(r0, t), :] = kn
            kn_ref[pl.ds(r0 + t, t), 0:PAIR] = kn.astype(BF16)
            neg_fk = jnp.dot(fk_ref[0, pl.ds(r0, t), :], place, precision=HIGHEST, preferred_element_type=F32)
            up = jnp.where((lane >= 6) & (lane < 9), 1.0, _lane_split3(neg_fk))
            kn_ref[pl.ds(r0 + t, t), PAIR:2 * PAIR] = up.astype(BF16)
            vb_ref[pl.ds(r0 + t, t), :] = v_ref[0, pl.ds(r0, t), :].astype(BF16)
            return carry
        lax.fori_loop(0, nblk, body, 0)

    qn = _pair_rmsnorm(q_ref[0], qg_ref[...], lo)
    qs, ups = [], []
    for hh in range(2):
        hm = lo if hh == 0 else jnp.logical_not(lo)
        qs.append(jnp.where(hm, qn, 0.0) * (HEAD_DIM ** -0.5))
        place_q = jnp.where((prow == 2 * p + hh) & (plane >= 6) & (plane < 9), 1.0, 0.0)
        fq = jnp.dot(fq_ref[0], place_q, precision=HIGHEST, preferred_element_type=F32)
        ups.append(jnp.where((lane >= 3 * hh) & (lane < 3 * hh + 3), 1.0, _lane_split3(fq)))
    q2 = jnp.concatenate([jnp.concatenate(qs, axis=0), jnp.concatenate(ups, axis=0)], axis=1).astype(BF16)

    def tile_rows(step):
        return pl.ds(pl.multiple_of((i - 2 * step) * t, t), 2 * t)

    def logits(step, first):
        s = _nt_dot(q2, kn_ref[tile_rows(step), :])
        if first:
            row = lax.broadcasted_iota(jnp.int32, (2 * t, 2 * t), 0) % t
            col = lax.broadcasted_iota(jnp.int32, (2 * t, 2 * t), 1) - t
            s = jnp.where(row >= col, s, NEG)
        return s

    o = _flash_attention((i + 2) // 2, logits, lambda step: vb_ref[tile_rows(step), :], 2 * t)
    o = jnp.where(lo, o[0:t], o[t:2 * t])
    o_ref[...] = (o * _silu(g_ref[0])).astype(BF16)


def _fox_prompt(z3, batch, seq, fcum, q_gain, k_gain):
    n = batch * seq
    t = ATT_TILE
    nq = seq // t
    npair = SEG // PAIR
    gain2 = lambda g: jnp.tile(g.astype(F32), 2).reshape(1, PAIR)
    return pl.pallas_call(
        _fox_prompt_kernel,
        grid=(batch, npair, nq),
        in_specs=[pl.BlockSpec((1, t, PAIR), lambda b, p, i: (0, b * nq + i, p)),
                  pl.BlockSpec((1, seq, PAIR), lambda b, p, i: (1, b, p)),
                  pl.BlockSpec((1, seq, PAIR), lambda b, p, i: (2, b, p)),
                  pl.BlockSpec((1, t, PAIR), lambda b, p, i: (3, b * nq + i, p)),
                  pl.BlockSpec((1, t, N_HEADS), lambda b, p, i: (b, i, 0)),
                  pl.BlockSpec((1, seq, N_HEADS), lambda b, p, i: (b, 0, 0)),
                  pl.BlockSpec((1, PAIR), lambda b, p, i: (0, 0)),
                  pl.BlockSpec((1, PAIR), lambda b, p, i: (0, 0))],
        out_specs=[pl.BlockSpec((t, PAIR), lambda b, p, i: (b * nq + i, p)),
                   pl.BlockSpec((seq, PAIR), lambda b, p, i: (b, p))],
        out_shape=[jax.ShapeDtypeStruct((n, SEG), BF16),
                   jax.ShapeDtypeStruct((n, SEG), F32)],
        scratch_shapes=[pltpu.VMEM((seq + t, 2 * PAIR), BF16),
                        pltpu.VMEM((seq + t, PAIR), BF16)],
        compiler_params=_cparams(3),
        name="fox_prompt",
    )(z3, z3, z3, z3, fcum, fcum, gain2(q_gain), gain2(k_gain))


CONV_HALO = 32
CONV_ROWS = 32


def _conv_prompt_kernel(ua_ref, ub_ref, gd_ref, uah_ref, ubh_ref, st_ref, w_ref, cb_ref, cg_ref,
                        d_ref, cs_ref, buf_ref, y_ref):
    ti = pl.program_id(1)
    nt = pl.num_programs(1)
    t = ua_ref.shape[1]
    pad = CONV_HALO - (CONV_WIDTH - 1)
    buf_ref[CONV_HALO:CONV_HALO + t, :] = ua_ref[0] * jax.nn.sigmoid(ub_ref[0])

    @pl.when(ti == 0)
    def _():
        buf_ref[0:CONV_HALO, :] = st_ref[0]

    @pl.when(ti > 0)
    def _():
        buf_ref[0:CONV_HALO, :] = uah_ref[0] * jax.nn.sigmoid(ubh_ref[0])

    for lg in range(SEG // PAIR):
        ls = slice(lg * PAIR, (lg + 1) * PAIR)
        for rc in range(t // CONV_ROWS):
            r0 = rc * CONV_ROWS
            acc = jnp.zeros((CONV_ROWS, PAIR), F32) + cb_ref[:, ls]
            for k in range(CONV_WIDTH):
                acc = acc + w_ref[k:k + 1, ls] * buf_ref[r0 + pad + k:r0 + pad + k + CONV_ROWS, ls]
            y_ref[r0:r0 + CONV_ROWS, ls] = acc

    y = y_ref[...]
    mu = jnp.mean(y, axis=-1, keepdims=True)
    yc = y - mu
    var = jnp.mean(yc * yc, axis=-1, keepdims=True)
    yn = yc * lax.rsqrt(var + EPS) * cg_ref[...]
    d_ref[...] = (_silu(yn) * _silu(gd_ref[0])).astype(BF16)

    @pl.when(ti == nt - 1)
    def _():
        cs_ref[0] = buf_ref[t + pad:t + CONV_HALO, :]


def _conv_prompt(z3, batch, seq, state, conv_w, conv_b, conv_gain):
    n = batch * seq
    t = min(seq, 256)
    nt = seq // t
    hb = t // CONV_HALO
    st = jnp.pad(state.astype(F32), ((0, 0), (CONV_HALO - (CONV_WIDTH - 1), 0), (0, 0)))
    tok = lambda s: pl.BlockSpec((1, t, SEG), lambda b, i, s=s: (s, b * nt + i, 0))
    halo = lambda s: pl.BlockSpec((1, CONV_HALO, SEG),
                                  lambda b, i, s=s: (s, jnp.maximum((b * nt + i) * hb - 1, 0), 0))
    row = pl.BlockSpec((1, SEG), lambda b, i: (0, 0))
    return pl.pallas_call(
        _conv_prompt_kernel,
        grid=(batch, nt),
        in_specs=[tok(4), tok(5), tok(6), halo(4), halo(5),
                  pl.BlockSpec((1, CONV_HALO, SEG), lambda b, i: (b, 0, 0)),
                  pl.BlockSpec((CONV_WIDTH, SEG), lambda b, i: (0, 0)), row, row],
        out_specs=[pl.BlockSpec((t, SEG), lambda b, i: (b * nt + i, 0)),
                   pl.BlockSpec((1, CONV_WIDTH - 1, SEG), lambda b, i: (b, 0, 0))],
        out_shape=[jax.ShapeDtypeStruct((n, SEG), BF16),
                   jax.ShapeDtypeStruct((batch, CONV_WIDTH - 1, SEG), F32)],
        scratch_shapes=[pltpu.VMEM((t + CONV_HALO, SEG), F32),
                        pltpu.VMEM((t, SEG), F32)],
        compiler_params=_cparams(2),
        name="conv_prompt",
    )(z3, z3, z3, z3, z3, st, conv_w.astype(F32), conv_b.astype(F32).reshape(1, SEG),
      conv_gain.astype(F32).reshape(1, SEG))


def _odd_weights(w_in, w_out):
    d = w_in.shape[0]
    w = SEG
    fz = w_in[:, 4 * w:4 * w + N_HEADS]
    rest = jnp.concatenate([w_in[:, :4 * w], w_in[:, 4 * w + N_HEADS:]], axis=1)
    fz_pad = jnp.pad(fz, ((0, 0), (0, w - N_HEADS)))
    w3 = jnp.concatenate([rest, fz_pad], axis=1).reshape(d, 8, w).transpose(1, 0, 2).astype(BF16)
    w2 = w_out.reshape(2, w, d).astype(BF16)
    return w3, w2


def _odd_layer_prompt(x, norm_g, w_in, b_forget, q_gain, k_gain, conv_w, conv_b, conv_gain, w_out):
    b, l, d = x.shape
    x2d = x.reshape(b * l, d)
    w3, w2 = _odd_weights(w_in, w_out)
    z3 = _proj_in(x2d, norm_g, w3)
    logf, fcum = _logf_prompt(z3, b, l, b_forget)
    oc, kc = _fox_prompt(z3, b, l, fcum, q_gain, k_gain)
    dd, cs = _conv_prompt(z3, b, l, jnp.zeros((b, CONV_WIDTH - 1, SEG), F32), conv_w, conv_b, conv_gain)
    y = _proj_out(oc, dd, x2d, w2)
    return (y.reshape(b, l, d), kc.reshape(b, l, N_HEADS, HEAD_DIM),
            z3[2].reshape(b, l, N_HEADS, HEAD_DIM), logf, cs)


def _heads_rmsnorm(x, gain):
    lo = _lo_mask()
    parts = [_pair_rmsnorm(x[:, g * PAIR:(g + 1) * PAIR], gain[:, g * PAIR:(g + 1) * PAIR], lo)
             for g in range(SEG // PAIR)]
    return jnp.concatenate(parts, axis=-1)


def _query_rows(qn, n_tok):
    sub = lax.broadcasted_iota(jnp.int32, (N_HEADS, SEG), 0)
    head_of_lane = lax.broadcasted_iota(jnp.int32, (N_HEADS, SEG), 1) // HEAD_DIM
    own = sub == head_of_lane
    return jnp.concatenate([jnp.where(own, jnp.broadcast_to(qn[q:q + 1, :], (N_HEADS, SEG)), 0.0)
                            for q in range(n_tok)], axis=0)


def _rows_to_tokens(rows_out, n_tok):
    sub = lax.broadcasted_iota(jnp.int32, (N_HEADS, SEG), 0)
    head_of_lane = lax.broadcasted_iota(jnp.int32, (N_HEADS, SEG), 1) // HEAD_DIM
    own = sub == head_of_lane
    tok = lax.broadcasted_iota(jnp.int32, (n_tok, SEG), 0)
    out = jnp.zeros((n_tok, SEG), F32)
    for q in range(n_tok):
        o_q = jnp.sum(jnp.where(own, rows_out[q * N_HEADS:(q + 1) * N_HEADS, :], 0.0), axis=0, keepdims=True)
        out = jnp.where(tok == q, o_q, out)
    return out


def _t5_bias_rows(rel, tab):
    bias = jnp.broadcast_to(tab[:, T5_BUCKETS - 1:T5_BUCKETS], rel.shape)
    for b in range(T5_BUCKETS - 2, -1, -1):
        bias = jnp.where(rel <= T5_UPPER[b], tab[:, b:b + 1], bias)
    return bias


def _finish_rows(s_past, s_new, v_pages, v_new, n_tok):
    m = jnp.max(s_past, axis=-1, keepdims=True)
    for col in s_new:
        m = jnp.maximum(m, col)
    p_past = jnp.exp(s_past - m)
    l = jnp.sum(p_past, axis=-1, keepdims=True)
    pb = p_past.astype(BF16)
    acc = jnp.zeros((s_past.shape[0], SEG), F32)
    for pg, v_ref in enumerate(v_pages):
        acc = acc + jnp.dot(pb[:, pg * PAGE_SIZE:(pg + 1) * PAGE_SIZE], v_ref[0].astype(BF16),
                            preferred_element_type=F32)
    for kj, col in enumerate(s_new):
        p_new = jnp.exp(col - m)
        l = l + p_new
        acc = acc + p_new * v_new[kj:kj + 1, :]
    return _rows_to_tokens(acc / l, n_tok)


def _moba_sample_kernel(pt_ref, tab_ref, q_ref, k_ref, v_ref, g_ref, qg_ref, kg_ref, *rest, n_pages, n_tok):
    k_pages = rest[:n_pages]
    v_pages = rest[n_pages:2 * n_pages]
    o_ref, ka_ref, km_ref = rest[2 * n_pages:]
    pages_per_block = MOBA_BLOCK // PAGE_SIZE
    nb_past = n_pages // pages_per_block
    past_len = n_pages * PAGE_SIZE
    n_rows = n_tok * N_HEADS

    qn = _heads_rmsnorm(q_ref[0, 0], qg_ref[...])
    kn = _heads_rmsnorm(k_ref[0, 0], kg_ref[...])
    ka_ref[0] = kn
    v_new = v_ref[0, 0]
    qrows = _query_rows(qn, n_tok)
    qb = (qrows * (HEAD_DIM ** -0.5)).astype(BF16)
    tab = tab_ref[...]
    row_q = lax.broadcasted_iota(jnp.int32, (n_rows, 1), 0) // N_HEADS

    for n in range(nb_past):
        tot = jnp.zeros((1, SEG), F32)
        for pg in range(n * pages_per_block, (n + 1) * pages_per_block):
            tot = tot + jnp.sum(k_pages[pg][0], axis=0, keepdims=True)
        km_ref[n:n + 1, :] = tot * (1.0 / MOBA_BLOCK)
    gate = _nt_dot(qrows, km_ref[...], precision=HIGHEST)
    selpen = jnp.where(_top_blocks(gate, nb_past, nb_past, nb_past), 0.0, NEG)

    far = tab[:, T5_BUCKETS - 1:T5_BUCKETS]
    lane = lax.broadcasted_iota(jnp.int32, (n_rows, PAGE_SIZE), 1)
    tiles = []
    for pg in range(n_pages):
        n = pg // pages_per_block
        s = _nt_dot(qb, k_pages[pg][0].astype(BF16))
        min_rel = past_len - (pg + 1) * PAGE_SIZE + 1
        if min_rel > T5_UPPER[-1]:
            bias = far
        else:
            bias = _t5_bias_rows(past_len + row_q - (pg * PAGE_SIZE + lane), tab)
        tiles.append(s + bias + selpen[:, n:n + 1])
    s_past = jnp.concatenate(tiles, axis=-1)

    s_new = []
    for kj in range(n_tok):
        dot = jnp.sum(qrows * kn[kj:kj + 1, :], axis=-1, keepdims=True) * (HEAD_DIM ** -0.5)
        rel = row_q - kj
        s_new.append(jnp.where(rel >= 0, dot + _t5_bias_rows(rel, tab), NEG))

    o = _finish_rows(s_past, s_new, v_pages, v_new, n_tok)
    o_ref[0] = (o * _silu(g_ref[0, 0])).astype(BF16)


def _page_specs(n_pages, width):
    return [pl.BlockSpec((1, PAGE_SIZE, width), lambda b, pt, pg=pg: (pt[b, pg], 0, 0)) for pg in range(n_pages)]


def _moba_sample(z3, n_seq, n_tok, cache_k, cache_v, page_table, t5_table, q_gain, k_gain):
    n_pages = page_table.shape[1]
    n_pool = cache_k.shape[0]
    assert MOBA_BLOCK % PAGE_SIZE == 0 and (n_pages * PAGE_SIZE) % MOBA_BLOCK == 0
    assert n_tok <= MOBA_BLOCK and n_pages // (MOBA_BLOCK // PAGE_SIZE) >= MOBA_TOPK
    z4 = z3.reshape(z3.shape[0], n_seq, n_tok, SEG)
    tok = lambda s: pl.BlockSpec((1, 1, n_tok, SEG), lambda b, pt, s=s: (s, b, 0, 0))
    row = pl.BlockSpec((1, SEG), lambda b, pt: (0, 0))
    gain8 = lambda g: jnp.tile(g.astype(F32), N_HEADS).reshape(1, SEG)
    tab_rows = jnp.tile(t5_table.astype(F32).T, (n_tok, 1))
    grid_spec = pltpu.PrefetchScalarGridSpec(
        num_scalar_prefetch=1,
        grid=(n_seq,),
        in_specs=[pl.BlockSpec((n_tok * N_HEADS, T5_BUCKETS), lambda b, pt: (0, 0)),
                  tok(0), tok(1), tok(2), tok(3), row, row]
                 + _page_specs(n_pages, SEG) + _page_specs(n_pages, SEG),
        out_specs=[pl.BlockSpec((1, n_tok, SEG), lambda b, pt: (b, 0, 0)),
                   pl.BlockSpec((1, n_tok, SEG), lambda b, pt: (b, 0, 0))],
        scratch_shapes=[pltpu.VMEM((n_pages * PAGE_SIZE // MOBA_BLOCK, SEG), F32)])
    ck = cache_k.reshape(n_pool, PAGE_SIZE, SEG)
    cv = cache_v.reshape(n_pool, PAGE_SIZE, SEG)
    return pl.pallas_call(
        functools.partial(_moba_sample_kernel, n_pages=n_pages, n_tok=n_tok),
        grid_spec=grid_spec,
        out_shape=[jax.ShapeDtypeStruct((n_seq, n_tok, SEG), BF16),
                   jax.ShapeDtypeStruct((n_seq, n_tok, SEG), F32)],
        compiler_params=_cparams(1),
        name="moba_sample",
    )(page_table, tab_rows, z4, z4, z4, z4, gain8(q_gain), gain8(k_gain),
      *([ck] * n_pages), *([cv] * n_pages))


def _fox_sample_kernel(pt_ref, q_ref, k_ref, v_ref, g_ref, fz_ref, fzr_ref, bf_ref, bfr_ref, qg_ref, kg_ref,
                       *rest, n_pages, n_tok):
    k_pages = rest[:n_pages]
    v_pages = rest[n_pages:2 * n_pages]
    f_pages = rest[2 * n_pages:3 * n_pages]
    o_ref, kc_ref, lf_ref = rest[3 * n_pages:]
    n_rows = n_tok * N_HEADS

    qn = _heads_rmsnorm(q_ref[0, 0], qg_ref[...])
    kn = _heads_rmsnorm(k_ref[0, 0], kg_ref[...])
    kc_ref[0] = kn
    v_new = v_ref[0, 0]
    lf_ref[0] = _log_sigmoid(fz_ref[0, 0][:, :N_HEADS] + bf_ref[...])
    qrows = _query_rows(qn, n_tok) * (HEAD_DIM ** -0.5)
    qb = qrows.astype(BF16)
    row_q = lax.broadcasted_iota(jnp.int32, (n_rows, 1), 0) // N_HEADS

    pick = (lax.broadcasted_iota(jnp.int32, (n_rows, N_HEADS), 0) % N_HEADS
            == lax.broadcasted_iota(jnp.int32, (n_rows, N_HEADS), 1)).astype(F32)
    upper = (lax.broadcasted_iota(jnp.int32, (PAGE_SIZE, PAGE_SIZE), 0)
             <= lax.broadcasted_iota(jnp.int32, (PAGE_SIZE, PAGE_SIZE), 1)).astype(F32)
    carry = jnp.zeros((n_rows, 1), F32)
    fk_tiles = []
    for pg in range(n_pages):
        rows_lf = _nt_dot(pick, f_pages[pg][0], precision=HIGHEST)
        cs = jnp.dot(rows_lf, upper, precision=HIGHEST, preferred_element_type=F32) + carry
        fk_tiles.append(cs)
        carry = cs[:, PAGE_SIZE - 1:PAGE_SIZE]
    lfr = _log_sigmoid(fzr_ref[0] + bfr_ref[...])
    cum_new = []
    run = carry
    for tk in range(n_tok):
        run = run + lfr[:, tk:tk + 1]
        cum_new.append(run)
    fq = cum_new[n_tok - 1]
    for tk in range(n_tok - 2, -1, -1):
        fq = jnp.where(row_q == tk, cum_new[tk], fq)

    tiles = [_nt_dot(qb, k_pages[pg][0].astype(BF16)) + (fq - fk_tiles[pg]) for pg in range(n_pages)]
    s_past = jnp.concatenate(tiles, axis=-1)
    s_new = []
    for kj in range(n_tok):
        dot = jnp.sum(qrows * kn[kj:kj + 1, :], axis=-1, keepdims=True)
        s_new.append(jnp.where(row_q >= kj, dot + (fq - cum_new[kj]), NEG))

    o = _finish_rows(s_past, s_new, v_pages, v_new, n_tok)
    o_ref[0] = (o * _silu(g_ref[0, 0])).astype(BF16)


def _fox_sample(z3, n_seq, n_tok, cache_k, cache_v, cache_logf, page_table, b_forget, q_gain, k_gain):
    n_pages = page_table.shape[1]
    n_pool = cache_k.shape[0]
    z4 = z3.reshape(z3.shape[0], n_seq, n_tok, SEG)
    tok = lambda s: pl.BlockSpec((1, 1, n_tok, SEG), lambda b, pt, s=s: (s, b, 0, 0))
    row = pl.BlockSpec((1, SEG), lambda b, pt: (0, 0))
    gain8 = lambda g: jnp.tile(g.astype(F32), N_HEADS).reshape(1, SEG)
    fz = z4[7, :, :, :N_HEADS]
    fz_rows = jnp.tile(fz.transpose(0, 2, 1), (1, n_tok, 1))
    bf = b_forget.astype(F32)
    grid_spec = pltpu.PrefetchScalarGridSpec(
        num_scalar_prefetch=1,
        grid=(n_seq,),
        in_specs=[tok(0), tok(1), tok(2), tok(3),
                  pl.BlockSpec((1, 1, n_tok, PAIR), lambda b, pt: (7, b, 0, 0)),
                  pl.BlockSpec((1, n_tok * N_HEADS, n_tok), lambda b, pt: (b, 0, 0)),
                  pl.BlockSpec((1, N_HEADS), lambda b, pt: (0, 0)),
                  pl.BlockSpec((n_tok * N_HEADS, 1), lambda b, pt: (0, 0)),
                  row, row]
                 + _page_specs(n_pages, SEG) + _page_specs(n_pages, SEG) + _page_specs(n_pages, N_HEADS),
        out_specs=[pl.BlockSpec((1, n_tok, SEG), lambda b, pt: (b, 0, 0)),
                   pl.BlockSpec((1, n_tok, SEG), lambda b, pt: (b, 0, 0)),
                   pl.BlockSpec((1, n_tok, N_HEADS), lambda b, pt: (b, 0, 0))])
    ck = cache_k.reshape(n_pool, PAGE_SIZE, SEG)
    cv = cache_v.reshape(n_pool, PAGE_SIZE, SEG)
    return pl.pallas_call(
        functools.partial(_fox_sample_kernel, n_pages=n_pages, n_tok=n_tok),
        grid_spec=grid_spec,
        out_shape=[jax.ShapeDtypeStruct((n_seq, n_tok, SEG), BF16),
                   jax.ShapeDtypeStruct((n_seq, n_tok, SEG), F32),
                   jax.ShapeDtypeStruct((n_seq, n_tok, N_HEADS), F32)],
        compiler_params=_cparams(1),
        name="fox_sample",
    )(page_table, z4, z4, z4, z4, z4, fz_rows, bf.reshape(1, N_HEADS),
      jnp.tile(bf, n_tok).reshape(n_tok * N_HEADS, 1), gain8(q_gain), gain8(k_gain),
      *([ck] * n_pages), *([cv] * n_pages), *([cache_logf.astype(F32)] * n_pages))


FLAT = PAGE_SIZE * N_HEADS


def _flat_rmsnorm(x, gain):
    return x * lax.rsqrt(jnp.mean(x * x, axis=-1, keepdims=True) + EPS) * gain


def _flat_softmax_pv(tiles, v_pages, s_new, v_new):
    m = jnp.max(s_new, axis=-1, keepdims=True)
    for s in tiles:
        m = jnp.maximum(m, jnp.max(s, axis=-1, keepdims=True))
    p_new = jnp.exp(s_new - m)
    l = jnp.sum(p_new, axis=-1, keepdims=True)
    acc = jnp.dot(p_new, v_new, precision=HIGHEST, preferred_element_type=F32)
    for s, v_ref in zip(tiles, v_pages):
        pr = jnp.exp(s - m)
        l = l + jnp.sum(pr, axis=-1, keepdims=True)
        acc = acc + jnp.dot(pr.astype(BF16), v_ref[0].astype(BF16), preferred_element_type=F32)
    return acc / l


def _moba_flat_kernel(pt_ref, tab_ref, q_ref, k_ref, v_ref, g_ref, qg_ref, kg_ref, *rest, n_pages, n_tok):
    k_pages = rest[:n_pages]
    v_pages = rest[n_pages:2 * n_pages]
    o_ref, ka_ref = rest[2 * n_pages:]
    pages_per_block = MOBA_BLOCK // PAGE_SIZE
    nb_past = n_pages // pages_per_block
    past_len = n_pages * PAGE_SIZE
    n_rows = n_tok * N_HEADS
    scale = HEAD_DIM ** -0.5

    qn = _flat_rmsnorm(q_ref[0], qg_ref[...])
    kn = _flat_rmsnorm(k_ref[0], kg_ref[...])
    ka_ref[0] = kn
    tab = tab_ref[...]
    row = lax.broadcasted_iota(jnp.int32, (n_rows, 1), 0)
    row_q, row_h = row // N_HEADS, row % N_HEADS

    blk_lane = lax.broadcasted_iota(jnp.int32, (n_rows, nb_past), 1)
    gate = jnp.zeros((n_rows, nb_past), F32)
    for n in range(nb_past):
        tot = jnp.zeros((N_HEADS, HEAD_DIM), F32)
        for pg in range(n * pages_per_block, (n + 1) * pages_per_block):
            tot = tot + jnp.sum(k_pages[pg][0].reshape(PAGE_SIZE, N_HEADS, HEAD_DIM), axis=0)
        k_mean = jnp.concatenate([tot * (1.0 / MOBA_BLOCK)] * n_tok, axis=0)
        gate = jnp.where(blk_lane == n, jnp.sum(qn * k_mean, axis=-1, keepdims=True), gate)
    selpen = jnp.where(_top_blocks(gate, nb_past, nb_past, nb_past), 0.0, NEG)

    qb = (qn * scale).astype(BF16)
    col = lax.broadcasted_iota(jnp.int32, (n_rows, FLAT), 1)
    own_head = (col % N_HEADS) == row_h
    far = tab[:, T5_BUCKETS - 1:T5_BUCKETS]
    tiles = []
    for pg in range(n_pages):
        n = pg // pages_per_block
        s = _nt_dot(qb, k_pages[pg][0].astype(BF16))
        if past_len - (pg + 1) * PAGE_SIZE + 1 > T5_UPPER[-1]:
            bias = far
        else:
            bias = _t5_bias_rows(past_len + row_q - (pg * PAGE_SIZE + col // N_HEADS), tab)
        tiles.append(jnp.where(own_head, s + (bias + selpen[:, n:n + 1]), NEG))

    ncol = lax.broadcasted_iota(jnp.int32, (n_rows, n_rows), 1)
    rel = row_q - ncol // N_HEADS
    s_new = _nt_dot(qn * scale, kn, precision=HIGHEST) + _t5_bias_rows(rel, tab)
    s_new = jnp.where(((ncol % N_HEADS) == row_h) & (rel >= 0), s_new, NEG)

    o = _flat_softmax_pv(tiles, v_pages, s_new, v_ref[0])
    o_ref[0] = (o * _silu(g_ref[0])).astype(BF16)


def _flat_page_specs(n_pages, rows, width):
    return [pl.BlockSpec((1, rows, width), lambda b, pt, pg=pg: (pt[b, pg], 0, 0)) for pg in range(n_pages)]


def _flat_tokens(z3, seg, n_seq, n_tok):
    return z3[seg].reshape(n_seq, n_tok * N_HEADS, HEAD_DIM)


def _moba_flat(z3, n_seq, n_tok, cache_k, cache_v, page_table, t5_table, q_gain, k_gain):
    n_pages = page_table.shape[1]
    n_pool = cache_k.shape[0]
    assert MOBA_BLOCK % PAGE_SIZE == 0 and (n_pages * PAGE_SIZE) % MOBA_BLOCK == 0
    assert n_tok <= MOBA_BLOCK and n_pages // (MOBA_BLOCK // PAGE_SIZE) >= MOBA_TOPK
    n_rows = n_tok * N_HEADS
    tok = pl.BlockSpec((1, n_rows, HEAD_DIM), lambda b, pt: (b, 0, 0))
    gain = pl.BlockSpec((1, HEAD_DIM), lambda b, pt: (0, 0))
    tab_rows = jnp.tile(t5_table.astype(F32).T, (n_tok, 1))
    grid_spec = pltpu.PrefetchScalarGridSpec(
        num_scalar_prefetch=1,
        grid=(n_seq,),
        in_specs=[pl.BlockSpec((n_rows, T5_BUCKETS), lambda b, pt: (0, 0)), tok, tok, tok, tok, gain, gain]
                 + _flat_page_specs(n_pages, FLAT, HEAD_DIM) + _flat_page_specs(n_pages, FLAT, HEAD_DIM),
        out_specs=[tok, tok])
    ck = cache_k.reshape(n_pool, FLAT, HEAD_DIM)
    cv = cache_v.reshape(n_pool, FLAT, HEAD_DIM)
    flat = lambda seg: _flat_tokens(z3, seg, n_seq, n_tok)
    return pl.pallas_call(
        functools.partial(_moba_flat_kernel, n_pages=n_pages, n_tok=n_tok),
        grid_spec=grid_spec,
        out_shape=[jax.ShapeDtypeStruct((n_seq, n_rows, HEAD_DIM), BF16),
                   jax.ShapeDtypeStruct((n_seq, n_rows, HEAD_DIM), F32)],
        compiler_params=_cparams(1),
        name="moba_sample",
    )(page_table, tab_rows, flat(0), flat(1), flat(2), flat(3),
      q_gain.astype(F32).reshape(1, HEAD_DIM), k_gain.astype(F32).reshape(1, HEAD_DIM),
      *([ck] * n_pages), *([cv] * n_pages))


LOGF_CHUNK = PAIR // N_HEADS


def _fox_flat_kernel(pt_ref, q_ref, k_ref, v_ref, g_ref, fz_ref, fzr_ref, bf_ref, bfr_ref, qg_ref, kg_ref,
                     *rest, n_pages, n_tok):
    k_pages = rest[:n_pages]
    v_pages = rest[n_pages:2 * n_pages]
    f_pages = rest[2 * n_pages:3 * n_pages]
    o_ref, kc_ref, lf_ref = rest[3 * n_pages:]
    n_rows = n_tok * N_HEADS
    chunks = PAGE_SIZE // LOGF_CHUNK
    scale = HEAD_DIM ** -0.5

    qn = _flat_rmsnorm(q_ref[0], qg_ref[...])
    kn = _flat_rmsnorm(k_ref[0], kg_ref[...])
    kc_ref[0] = kn
    lf_ref[0] = _log_sigmoid(fz_ref[0] + bf_ref[...])
    row = lax.broadcasted_iota(jnp.int32, (n_rows, 1), 0)
    row_q, row_h = row // N_HEADS, row % N_HEADS

    lf = jnp.concatenate([f_pages[pg][0] for pg in range(n_pages)], axis=0)
    nr = lf.shape[0]
    li = lax.broadcasted_iota(jnp.int32, (PAIR, PAIR), 0)
    lj = lax.broadcasted_iota(jnp.int32, (PAIR, PAIR), 1)
    scan = ((li % N_HEADS == lj % N_HEADS) & (li <= lj)).astype(F32)
    last = (li == PAIR - N_HEADS + lj % N_HEADS).astype(F32)
    ri = lax.broadcasted_iota(jnp.int32, (nr, nr), 0)
    rj = lax.broadcasted_iota(jnp.int32, (nr, nr), 1)
    before = (rj < ri).astype(F32)
    in_row = jnp.dot(lf, scan, precision=HIGHEST, preferred_element_type=F32)
    totals = jnp.dot(in_row, last, precision=HIGHEST, preferred_element_type=F32)
    fk = in_row + jnp.dot(before, totals, precision=HIGHEST, preferred_element_type=F32)

    lane = lax.broadcasted_iota(jnp.int32, (n_rows, PAIR), 1)
    past_total = jnp.sum(jnp.where(lane == PAIR - N_HEADS + row_h, fk[nr - 1:nr, :], 0.0),
                         axis=-1, keepdims=True)
    lfr = _log_sigmoid(fzr_ref[0] + bfr_ref[...])
    cum_new = []
    run = past_total
    for tk in range(n_tok):
        run = run + lfr[:, tk:tk + 1]
        cum_new.append(run)
    fq = cum_new[n_tok - 1]
    for tk in range(n_tok - 2, -1, -1):
        fq = jnp.where(row_q == tk, cum_new[tk], fq)

    qb = (qn * scale).astype(BF16)
    col = lax.broadcasted_iota(jnp.int32, (n_rows, FLAT), 1)
    own_head = (col % N_HEADS) == row_h
    tiles = []
    for pg in range(n_pages):
        s = _nt_dot(qb, k_pages[pg][0].astype(BF16))
        fk_row = jnp.concatenate([fk[pg * chunks + c:pg * chunks + c + 1, :] for c in range(chunks)], axis=1)
        tiles.append(jnp.where(own_head, s + (fq - fk_row), NEG))

    ncol = lax.broadcasted_iota(jnp.int32, (n_rows, n_rows), 1)
    fk_new = cum_new[n_tok - 1]
    for tk in range(n_tok - 2, -1, -1):
        fk_new = jnp.where(ncol // N_HEADS == tk, cum_new[tk], fk_new)
    s_new = _nt_dot(qn * scale, kn, precision=HIGHEST) + (fq - fk_new)
    s_new = jnp.where(((ncol % N_HEADS) == row_h) & (ncol // N_HEADS <= row_q), s_new, NEG)

    o = _flat_softmax_pv(tiles, v_pages, s_new, v_ref[0])
    o_ref[0] = (o * _silu(g_ref[0])).astype(BF16)


def _fox_flat(z3, n_seq, n_tok, cache_k, cache_v, cache_logf, page_table, b_forget, q_gain, k_gain):
    n_pages = page_table.shape[1]
    n_pool = cache_k.shape[0]
    n_rows = n_tok * N_HEADS
    tok = pl.BlockSpec((1, n_rows, HEAD_DIM), lambda b, pt: (b, 0, 0))
    gain = pl.BlockSpec((1, HEAD_DIM), lambda b, pt: (0, 0))
    fz = z3[7].reshape(n_seq, n_tok, SEG)[:, :, :N_HEADS]
    fz_rows = jnp.tile(fz.transpose(0, 2, 1), (1, n_tok, 1))
    bf = b_forget.astype(F32)
    grid_spec = pltpu.PrefetchScalarGridSpec(
        num_scalar_prefetch=1,
        grid=(n_seq,),
        in_specs=[tok, tok, tok, tok,
                  pl.BlockSpec((1, n_tok, N_HEADS), lambda b, pt: (b, 0, 0)),
                  pl.BlockSpec((1, n_rows, n_tok), lambda b, pt: (b, 0, 0)),
                  pl.BlockSpec((1, N_HEADS), lambda b, pt: (0, 0)),
                  pl.BlockSpec((n_rows, 1), lambda b, pt: (0, 0)),
                  gain, gain]
                 + _flat_page_specs(n_pages, FLAT, HEAD_DIM) + _flat_page_specs(n_pages, FLAT, HEAD_DIM)
                 + _flat_page_specs(n_pages, PAGE_SIZE // LOGF_CHUNK, PAIR),
        out_specs=[tok, tok, pl.BlockSpec((1, n_tok, N_HEADS), lambda b, pt: (b, 0, 0))])
    ck = cache_k.reshape(n_pool, FLAT, HEAD_DIM)
    cv = cache_v.reshape(n_pool, FLAT, HEAD_DIM)
    cf = cache_logf.astype(F32).reshape(n_pool, PAGE_SIZE // LOGF_CHUNK, PAIR)
    flat = lambda seg: _flat_tokens(z3, seg, n_seq, n_tok)
    return pl.pallas_call(
        functools.partial(_fox_flat_kernel, n_pages=n_pages, n_tok=n_tok),
        grid_spec=grid_spec,
        out_shape=[jax.ShapeDtypeStruct((n_seq, n_rows, HEAD_DIM), BF16),
                   jax.ShapeDtypeStruct((n_seq, n_rows, HEAD_DIM), F32),
                   jax.ShapeDtypeStruct((n_seq, n_tok, N_HEADS), F32)],
        compiler_params=_cparams(1),
        name="fox_sample",
    )(page_table, flat(0), flat(1), flat(2), flat(3), fz, fz_rows, bf.reshape(1, N_HEADS),
      jnp.tile(bf, n_tok).reshape(n_rows, 1),
      q_gain.astype(F32).reshape(1, HEAD_DIM), k_gain.astype(F32).reshape(1, HEAD_DIM),
      *([ck] * n_pages), *([cv] * n_pages), *([cf] * n_pages))


CONV_SAMPLE_BATCH = 8


def _conv_sample_kernel(ua_ref, ub_ref, gd_ref, st_ref, w_ref, cb_ref, cg_ref, d_ref, cs_ref, buf_ref):
    n_tok = ua_ref.shape[2]
    hist = CONV_WIDTH - 1
    for e in range(ua_ref.shape[1]):
        buf_ref[0:hist, :] = st_ref[e]
        buf_ref[hist:hist + n_tok, :] = ua_ref[0, e] * jax.nn.sigmoid(ub_ref[0, e])
        y = jnp.zeros((n_tok, SEG), F32) + cb_ref[...]
        for k in range(CONV_WIDTH):
            y = y + w_ref[k:k + 1, :] * buf_ref[k:k + n_tok, :]
        mu = jnp.mean(y, axis=-1, keepdims=True)
        yc = y - mu
        var = jnp.mean(yc * yc, axis=-1, keepdims=True)
        yn = yc * lax.rsqrt(var + EPS) * cg_ref[...]
        d_ref[e] = (_silu(yn) * _silu(gd_ref[0, e])).astype(BF16)
        cs_ref[e] = buf_ref[n_tok:n_tok + hist, :]


def _conv_sample(z3, n_seq, n_tok, state, conv_w, conv_b, conv_gain):
    bt = math.gcd(n_seq, CONV_SAMPLE_BATCH)
    hist = CONV_WIDTH - 1
    z4 = z3.reshape(z3.shape[0], n_seq, n_tok, SEG)
    tok = lambda s: pl.BlockSpec((1, bt, n_tok, SEG), lambda b, s=s: (s, b, 0, 0))
    row = pl.BlockSpec((1, SEG), lambda b: (0, 0))
    return pl.pallas_call(
        _conv_sample_kernel,
        grid=(n_seq // bt,),
        in_specs=[tok(4), tok(5), tok(6),
                  pl.BlockSpec((bt, hist, SEG), lambda b: (b, 0, 0)),
                  pl.BlockSpec((CONV_WIDTH, SEG), lambda b: (0, 0)), row, row],
        out_specs=[pl.BlockSpec((bt, n_tok, SEG), lambda b: (b, 0, 0)),
                   pl.BlockSpec((bt, hist, SEG), lambda b: (b, 0, 0))],
        out_shape=[jax.ShapeDtypeStruct((n_seq, n_tok, SEG), BF16),
                   jax.ShapeDtypeStruct((n_seq, hist, SEG), F32)],
        scratch_shapes=[pltpu.VMEM((hist + n_tok + 6, SEG), F32)],
        compiler_params=_cparams(1),
        name="conv_sample",
    )(z4, z4, z4, state.astype(F32), conv_w.astype(F32), conv_b.astype(F32).reshape(1, SEG),
      conv_gain.astype(F32).reshape(1, SEG))


RET_SAMPLE_GROUPS = 2


def _even_layer_sample(x, cache_k, cache_v, ret_state, page_table, norm_g, w_in, q_gain, k_gain, t5_table,
                       ret_gain, w_out):
    b, l, d = x.shape
    x2d = x.reshape(b * l, d)
    w3, w2 = _even_weights(w_in, w_out)
    z3 = _proj_in(x2d, norm_g, w3)
    oa, ka = _moba_flat(z3, b, l, cache_k, cache_v, page_table, t5_table, q_gain, k_gain)
    past_len = page_table.shape[1] * PAGE_SIZE
    groups = math.gcd(b, RET_SAMPLE_GROUPS)
    orr, sp = _retention(z3, b, l, l * groups, groups, past_len, ret_gain, _pair_blockdiag(ret_state), F32,
                         "retention_sample")
    y = _proj_out(oa.reshape(b * l, SEG), orr, x2d, w2)
    return (y.reshape(b, l, d), ka.reshape(b, l, N_HEADS, HEAD_DIM),
            z3[2].reshape(b, l, N_HEADS, HEAD_DIM), _pair_unblockdiag(sp))


def _odd_layer_sample(x, cache_k, cache_v, cache_logf, conv_state, page_table, norm_g, w_in, b_forget, q_gain,
                      k_gain, conv_w, conv_b, conv_gain, w_out):
    b, l, d = x.shape
    x2d = x.reshape(b * l, d)
    w3, w2 = _odd_weights(w_in, w_out)
    z3 = _proj_in(x2d, norm_g, w3)
    oc, kc, logf = _fox_flat(z3, b, l, cache_k, cache_v, cache_logf, page_table, b_forget, q_gain, k_gain)
    dd, cs = _conv_sample(z3, b, l, conv_state, conv_w, conv_b, conv_gain)
    y = _proj_out(oc.reshape(b * l, SEG), dd.reshape(b * l, SEG), x2d, w2)
    return (y.reshape(b, l, d), kc.reshape(b, l, N_HEADS, HEAD_DIM),
            z3[2].reshape(b, l, N_HEADS, HEAD_DIM), logf, cs)


def kernel(x_prompt, x_sample, cache_moba_k, cache_moba_v, cache_fox_k, cache_fox_v, cache_fox_logf, state_ret, state_conv, page_table, norm_g_even, w_in_even, moba_q_gain, moba_k_gain, t5_table, ret_gain, w_out_even, norm_g_odd, w_in_odd, b_forget, fox_q_gain, fox_k_gain, conv_w, conv_b, conv_gain, w_out_odd):
    depth = norm_g_even.shape[0] + norm_g_odd.shape[0]
    xp, xs = x_prompt, x_sample
    outs = {name: [] for name in ("mk_p", "mv_p", "mk_s", "mv_s", "rs_p", "rs_s",
                                  "fk_p", "fv_p", "fl_p", "fk_s", "fv_s", "fl_s", "cs_p", "cs_s")}
    for layer in range(depth):
        i = layer // 2
        if layer % 2 == 0:
            w = (norm_g_even[i], w_in_even[i], moba_q_gain[i], moba_k_gain[i], t5_table, ret_gain[i], w_out_even[i])
            xp, k1, v1, s1 = _even_layer_prompt(xp, *w)
            xs, k2, v2, s2 = _even_layer_sample(xs, cache_moba_k[i], cache_moba_v[i], state_ret[i], page_table, *w)
            for name, val in zip(("mk_p", "mv_p", "rs_p", "mk_s", "mv_s", "rs_s"), (k1, v1, s1, k2, v2, s2)):
                outs[name].append(val)
        else:
            w = (norm_g_odd[i], w_in_odd[i], b_forget[i], fox_q_gain[i], fox_k_gain[i],
                 conv_w[i], conv_b[i], conv_gain[i], w_out_odd[i])
            xp, k1, v1, f1, c1 = _odd_layer_prompt(xp, *w)
            xs, k2, v2, f2, c2 = _odd_layer_sample(xs, cache_fox_k[i], cache_fox_v[i], cache_fox_logf[i],
                                                   state_conv[i], page_table, *w)
            for name, val in zip(("fk_p", "fv_p", "fl_p", "cs_p", "fk_s", "fv_s", "fl_s", "cs_s"),
                                 (k1, v1, f1, c1, k2, v2, f2, c2)):
                outs[name].append(val)
    st = lambda name: jnp.stack(outs[name])
    return (xp, xs, st("mk_p"), st("mv_p"), st("mk_s"), st("mv_s"), st("rs_p"), st("rs_s"),
            st("fk_p"), st("fv_p"), st("fl_p"), st("fk_s"), st("fv_s"), st("fl_s"), st("cs_p"), st("cs_s"))
```

```python
import functools
import math

import numpy as np
import jax
import jax.numpy as jnp
from jax import lax
from jax.experimental import pallas as pl
from jax.experimental.pallas import tpu as pltpu

F32 = jnp.float32
BF16 = jnp.bfloat16
HIGHEST = lax.Precision.HIGHEST

HEAD_DIM = 64
PAIR = 2 * HEAD_DIM
SEG = 512
N_HEADS = SEG // HEAD_DIM
EPS = 1e-6
MOBA_BLOCK = 256
MOBA_TOPK = 3
T5_BUCKETS = 32
T5_MAX_DIST = 128
ROPE_BASE = 10000.0
CONV_WIDTH = 31
PAGE_SIZE = 128
NEG = -1e30
ATT_TILE = 256
V7X_VMEM_LIMIT = 48 * 1024 * 1024


def _t5_bucket_upper_bounds():
    max_exact = T5_BUCKETS // 2
    rel = np.arange(0, 4 * T5_MAX_DIST)
    relf = np.maximum(rel, 1).astype(np.float64)
    large = max_exact + np.trunc(np.log(relf / max_exact) / math.log(T5_MAX_DIST / max_exact)
                                 * (T5_BUCKETS - max_exact)).astype(np.int64)
    bucket = np.where(rel < max_exact, rel, np.minimum(large, T5_BUCKETS - 1))
    return tuple(int(rel[bucket <= b].max()) for b in range(T5_BUCKETS - 1))


T5_UPPER = _t5_bucket_upper_bounds()


def _cparams(n_axes):
    return pltpu.CompilerParams(dimension_semantics=("arbitrary",) * n_axes,
                                vmem_limit_bytes=V7X_VMEM_LIMIT)


def _silu(x):
    return x * jax.nn.sigmoid(x)


def _lo_mask():
    return lax.broadcasted_iota(jnp.int32, (1, PAIR), 1) < HEAD_DIM


def _pair_sum(x, lo):
    s0 = jnp.sum(jnp.where(lo, x, 0.0), axis=-1, keepdims=True)
    s1 = jnp.sum(jnp.where(lo, 0.0, x), axis=-1, keepdims=True)
    return jnp.where(lo, s0, s1)


def _pair_rmsnorm(x, gain, lo):
    ms = _pair_sum(x * x, lo) * (1.0 / HEAD_DIM)
    return x * lax.rsqrt(ms + EPS) * gain


def _t5_bias(rel, tab_ref, h):
    bias = jnp.full(rel.shape, tab_ref[T5_BUCKETS - 1, h], F32)
    for b in range(T5_BUCKETS - 2, -1, -1):
        bias = jnp.where(rel <= T5_UPPER[b], tab_ref[b, h], bias)
    return bias


def _top_blocks(gate, n_valid, own, width):
    lane = lax.broadcasted_iota(jnp.int32, gate.shape, 1)
    g = jnp.where(lane < n_valid, gate, -jnp.inf)
    sel = lane == own
    for _ in range(MOBA_TOPK):
        m = jnp.max(g, axis=-1, keepdims=True)
        idx = jnp.min(jnp.where(g == m, lane, width), axis=-1, keepdims=True)
        pick = (lane == idx) & (m > -jnp.inf)
        sel = sel | pick
        g = jnp.where(pick, -jnp.inf, g)
    return sel


def _nt_dot(a, b, precision=None):
    return lax.dot_general(a, b, (((1,), (1,)), ((), ())), precision=precision,
                           preferred_element_type=F32)


def _proj_in_kernel(x_ref, g_ref, w_ref, o_ref, xn_ref):
    j = pl.program_id(1)

    @pl.when(j == 0)
    def _():
        x = x_ref[...]
        ms = jnp.mean(x * x, axis=-1, keepdims=True)
        xn_ref[...] = (x * lax.rsqrt(ms + EPS) * g_ref[...]).astype(BF16)

    o_ref[0] = jnp.dot(xn_ref[...], w_ref[j], preferred_element_type=F32)


def _proj_in(x2d, gain, w3):
    n, d = x2d.shape
    nseg = w3.shape[0]
    tm = min(n, 1024)
    return pl.pallas_call(
        _proj_in_kernel,
        grid=(n // tm, nseg),
        in_specs=[pl.BlockSpec((tm, d), lambda i, j: (i, 0)),
                  pl.BlockSpec((1, d), lambda i, j: (0, 0)),
                  pl.BlockSpec((nseg, d, SEG), lambda i, j: (0, 0, 0))],
        out_specs=pl.BlockSpec((1, tm, SEG), lambda i, j: (j, i, 0)),
        out_shape=jax.ShapeDtypeStruct((nseg, n, SEG), F32),
        scratch_shapes=[pltpu.VMEM((tm, d), BF16)],
        compiler_params=_cparams(2),
        name="proj_in",
    )(x2d, gain.reshape(1, d), w3)


def _proj_out_kernel(a_ref, b_ref, x_ref, w_ref, o_ref):
    acc = jnp.dot(a_ref[...], w_ref[0], preferred_element_type=F32)
    acc = acc + jnp.dot(b_ref[...], w_ref[1], preferred_element_type=F32)
    o_ref[...] = x_ref[...] + acc


def _proj_out(a, b, x2d, w2):
    n, d = x2d.shape
    tm = min(n, 512)
    return pl.pallas_call(
        _proj_out_kernel,
        grid=(n // tm,),
        in_specs=[pl.BlockSpec((tm, SEG), lambda i: (i, 0)),
                  pl.BlockSpec((tm, SEG), lambda i: (i, 0)),
                  pl.BlockSpec((tm, d), lambda i: (i, 0)),
                  pl.BlockSpec((2, SEG, d), lambda i: (0, 0, 0))],
        out_specs=pl.BlockSpec((tm, d), lambda i: (i, 0)),
        out_shape=jax.ShapeDtypeStruct((n, d), F32),
        compiler_params=_cparams(1),
        name="proj_out",
    )(a, b, x2d, w2)


def _top_blocks_t(gate_t, n_valid, own, width):
    blk = lax.broadcasted_iota(jnp.int32, gate_t.shape, 0)
    g = jnp.where(blk < n_valid, gate_t, -jnp.inf)
    sel = blk == own
    for _ in range(MOBA_TOPK):
        m = jnp.max(g, axis=0, keepdims=True)
        idx = jnp.min(jnp.where(g == m, blk, width), axis=0, keepdims=True)
        pick = (blk == idx) & (m > -jnp.inf)
        sel = sel | pick
        g = jnp.where(pick, -jnp.inf, g)
    return sel


def _tn_dot(a, b):
    return lax.dot_general(a, b, (((0,), (0,)), ((), ())), preferred_element_type=F32)


def _online_softmax_step(carry, s, v):
    m, l, acc = carry
    m_new = jnp.maximum(m, jnp.max(s, axis=-1, keepdims=True))
    alpha = jnp.exp(m - m_new)
    pr = jnp.exp(s - m_new)
    l = alpha * l + jnp.sum(pr, axis=-1, keepdims=True)
    acc = alpha * acc + jnp.dot(pr.astype(BF16), v, preferred_element_type=F32)
    return m_new, l, acc


def _softmax_init(rows):
    return (jnp.full((rows, 1), NEG, F32), jnp.zeros((rows, 1), F32), jnp.zeros((rows, PAIR), F32))


def _flash_attention(n_tiles, logits_fn, v_tile_fn, m_ref, l_ref, acc_ref):
    s = logits_fn(0, True)
    reps = s.shape[1] // PAIR
    m = jnp.max(s, axis=-1, keepdims=True)
    pr = jnp.exp(s - m)
    m_ref[...] = jnp.broadcast_to(m, m_ref.shape)
    l_ref[...] = jnp.broadcast_to(jnp.sum(pr, axis=-1, keepdims=True), l_ref.shape)
    acc_ref[...] = jnp.dot(pr.astype(BF16), v_tile_fn(0), preferred_element_type=F32)

    def body(step, carry):
        s = logits_fn(step, False)
        m_prev = m_ref[...]
        m_new = jnp.maximum(m_prev, jnp.max(s, axis=-1, keepdims=True))
        alpha = jnp.exp(m_prev - m_new)
        pr = jnp.exp(s - jnp.concatenate([m_new] * reps, axis=1))
        m_ref[...] = m_new
        l_ref[...] = alpha * l_ref[...] + jnp.sum(pr, axis=-1, keepdims=True)
        acc_ref[...] = alpha * acc_ref[...] + jnp.dot(pr.astype(BF16), v_tile_fn(step),
                                                      preferred_element_type=F32)
        return carry

    lax.fori_loop(1, n_tiles, body, 0)
    return acc_ref[...] / l_ref[...]


def _moba_prompt_kernel(tab_ref, q_ref, k_ref, v_ref, g_ref, qg_ref, kg_ref,
                        o_ref, ka_ref,
                        kn_ref, vb_ref, km_ref, bias_ref, m_ref, l_ref, acc_ref):
    p = pl.program_id(1)
    i = pl.program_id(2)
    t = ATT_TILE
    nblk = k_ref.shape[1] // t
    lo = _lo_mask()

    @pl.when(i == 0)
    def _prep():
        lane = lax.broadcasted_iota(jnp.int32, (t, PAIR), 1)
        kn_ref[0:t, 0:PAIR] = jnp.zeros((t, PAIR), BF16)
        kn_ref[0:t, PAIR:2 * PAIR] = (lane == nblk).astype(BF16)
        vb_ref[0:t, :] = jnp.zeros((t, PAIR), BF16)

        def body(c, carry):
            r0 = pl.multiple_of(c * t, t)
            kn = _pair_rmsnorm(k_ref[0, pl.ds(r0, t), :], kg_ref[...], lo)
            ka_ref[pl.ds(r0, t), :] = kn
            kn_ref[pl.ds(r0 + t, t), 0:PAIR] = kn.astype(BF16)
            kn_ref[pl.ds(r0 + t, t), PAIR:2 * PAIR] = (lane == c).astype(BF16)
            km_ref[pl.ds(c, 1), :] = jnp.sum(kn, axis=0, keepdims=True) * (1.0 / t)
            vb_ref[pl.ds(r0 + t, t), :] = v_ref[0, pl.ds(r0, t), :].astype(BF16)
            return carry
        lax.fori_loop(0, nblk, body, 0)
        rel = (lax.broadcasted_iota(jnp.int32, (t, t), 0) - lax.broadcasted_iota(jnp.int32, (t, t), 1))
        for hh in range(2):
            h = 2 * p + hh
            bias_ref[0, hh * t:(hh + 1) * t, 0:t] = _t5_bias(rel + t, tab_ref, h)
            bias_ref[0, hh * t:(hh + 1) * t, t:2 * t] = jnp.where(rel >= 0, _t5_bias(rel, tab_ref, h), NEG)
            bias_ref[1, hh * t:(hh + 1) * t, :] = jnp.full((t, 2 * t), tab_ref[T5_BUCKETS - 1, h], F32)

    qn = _pair_rmsnorm(q_ref[0], qg_ref[...], lo)
    km = km_ref[...]
    no_block = jnp.where(lax.broadcasted_iota(jnp.int32, (PAIR - nblk, t), 0) == 0, NEG, 0.0)
    qs, pens = [], []
    for hh in range(2):
        hm = lo if hh == 0 else jnp.logical_not(lo)
        qh = jnp.where(hm, qn, 0.0)
        gate_t = _nt_dot(km, qh, precision=HIGHEST)
        sel = _top_blocks_t(gate_t, i, i, nblk)
        pen_t = jnp.concatenate([jnp.where(sel, 0.0, NEG), no_block], axis=0)
        pens.append(pen_t.T)
        qs.append(qh * (HEAD_DIM ** -0.5))
    q2 = jnp.concatenate([jnp.concatenate(qs, axis=0), jnp.concatenate(pens, axis=0)], axis=1).astype(BF16)

    def tile_rows(step):
        return pl.ds(pl.multiple_of((i - 2 * step) * t, t), 2 * t)

    def logits(step, first):
        return _nt_dot(q2, kn_ref[tile_rows(step), :]) + bias_ref[0 if first else 1]

    o = _flash_attention((i + 2) // 2, logits, lambda step: vb_ref[tile_rows(step), :], m_ref, l_ref, acc_ref)
    o = jnp.where(lo, o[0:t], o[t:2 * t])
    o_ref[...] = (o * _silu(g_ref[0])).astype(BF16)


def _moba_prompt(z3, batch, seq, t5_table, q_gain, k_gain):
    n = batch * seq
    t = ATT_TILE
    nq = seq // t
    npair = SEG // PAIR
    gain2 = lambda g: jnp.tile(g.astype(F32), 2).reshape(1, PAIR)
    return pl.pallas_call(
        _moba_prompt_kernel,
        grid=(batch, npair, nq),
        in_specs=[pl.BlockSpec(memory_space=pltpu.SMEM),
                  pl.BlockSpec((1, t, PAIR), lambda b, p, i: (0, b * nq + i, p)),
                  pl.BlockSpec((1, seq, PAIR), lambda b, p, i: (1, b, p)),
                  pl.BlockSpec((1, seq, PAIR), lambda b, p, i: (2, b, p)),
                  pl.BlockSpec((1, t, PAIR), lambda b, p, i: (3, b * nq + i, p)),
                  pl.BlockSpec((1, PAIR), lambda b, p, i: (0, 0)),
                  pl.BlockSpec((1, PAIR), lambda b, p, i: (0, 0))],
        out_specs=[pl.BlockSpec((t, PAIR), lambda b, p, i: (b * nq + i, p)),
                   pl.BlockSpec((seq, PAIR), lambda b, p, i: (b, p))],
        out_shape=[jax.ShapeDtypeStruct((n, SEG), BF16),
                   jax.ShapeDtypeStruct((n, SEG), F32)],
        scratch_shapes=[pltpu.VMEM((seq + t, 2 * PAIR), BF16),
                        pltpu.VMEM((seq + t, PAIR), BF16),
                        pltpu.VMEM((seq // t, PAIR), F32),
                        pltpu.VMEM((2, 2 * t, 2 * t), F32),
                        pltpu.VMEM((2 * t, PAIR), F32),
                        pltpu.VMEM((2 * t, PAIR), F32),
                        pltpu.VMEM((2 * t, PAIR), F32)],
        compiler_params=_cparams(3),
        name="moba_prompt",
    )(t5_table.astype(F32), z3, z3, z3, z3, gain2(q_gain), gain2(k_gain))


def _rope_pair(x, cos, sin_signed):
    up = pltpu.roll(x, PAIR - HEAD_DIM // 2, 1)
    dn = pltpu.roll(x, HEAD_DIM // 2, 1)
    lane = lax.broadcasted_iota(jnp.int32, (1, PAIR), 1)
    first_half = (lane % HEAD_DIM) < (HEAD_DIM // 2)
    return x * cos + jnp.where(first_half, up, dn) * sin_signed


def _retention_kernel(lg_ref, q_ref, k_ref, v_ref, g_ref, cos_ref, sin_ref, gain_ref, s0_ref,
                      o_ref, sout_ref, s_ref, *, groups, mm_dtype):
    p = pl.program_id(1)
    c = pl.program_id(2)
    n = q_ref.shape[0]
    glen = n // groups
    lo = _lo_mask()

    @pl.when(c == 0)
    def _():
        s_ref[...] = s0_ref[...]

    cos = cos_ref[...]
    sin = sin_ref[...]
    qr = _rope_pair(q_ref[...], cos, sin)
    kr = _rope_pair(k_ref[...], cos, sin) * (HEAD_DIM ** -0.5)
    vb = v_ref[...].astype(mm_dtype)
    lg0 = lg_ref[2 * p]
    lg1 = lg_ref[2 * p + 1]
    lg_lane = jnp.where(lo, lg0, lg1)
    row = lax.broadcasted_iota(jnp.int32, (n, 1), 0)
    pos = (row % glen).astype(F32)
    grp = row // glen
    q_dec = jnp.exp(lg_lane * (pos + 1.0))
    k_dec = jnp.exp(lg_lane * (float(glen - 1) - pos))
    qd = (qr * q_dec).astype(mm_dtype)
    o = jnp.zeros((n, PAIR), F32)
    for g in range(groups):
        og = jnp.dot(qd, s_ref[g].astype(mm_dtype), preferred_element_type=F32)
        o = o + (og if groups == 1 else jnp.where(grp == g, og, 0.0))
    ri = lax.broadcasted_iota(jnp.int32, (n, n), 0)
    ci = lax.broadcasted_iota(jnp.int32, (n, n), 1)
    causal = (ri >= ci) & ((ri // glen) == (ci // glen))
    dpos = jnp.where(causal, ri - ci, 0).astype(F32)
    krb = kr.astype(mm_dtype)
    for hh in range(2):
        hm = lo if hh == 0 else jnp.logical_not(lo)
        lgh = lg0 if hh == 0 else lg1
        intra = jnp.where(causal, jnp.exp(lgh * dpos), 0.0)
        a = _nt_dot(jnp.where(hm, qr, 0.0).astype(mm_dtype), krb) * intra
        oh = jnp.dot(a.astype(mm_dtype), vb, preferred_element_type=F32)
        o = o + jnp.where(hm, oh, 0.0)
    srow = lax.broadcasted_iota(jnp.int32, (PAIR, 1), 0) < HEAD_DIM
    c_dec = jnp.exp(jnp.where(srow, lg0, lg1) * float(glen))
    same_head = srow == lo
    kd = kr * k_dec
    for g in range(groups):
        kg = kd if groups == 1 else jnp.where(grp == g, kd, 0.0)
        kv = lax.dot_general(kg.astype(mm_dtype), vb, (((0,), (0,)), ((), ())),
                             preferred_element_type=F32)
        new_state = jnp.where(same_head, s_ref[g] * c_dec + kv, 0.0)
        s_ref[g] = new_state
        sout_ref[g] = new_state

    mu = _pair_sum(o, lo) * (1.0 / HEAD_DIM)
    oc = o - mu
    var = _pair_sum(oc * oc, lo) * (1.0 / HEAD_DIM)
    on = oc * lax.rsqrt(var + EPS) * gain_ref[...]
    o_ref[...] = (on * _silu(g_ref[...])).astype(BF16)


def _pair_blockdiag(s):
    b = s.shape[0]
    s = s.astype(F32).reshape(b, N_HEADS // 2, 2, HEAD_DIM, HEAD_DIM)
    z = jnp.zeros_like(s[:, :, 0])
    top = jnp.concatenate([s[:, :, 0], z], axis=-1)
    bot = jnp.concatenate([z, s[:, :, 1]], axis=-1)
    return jnp.concatenate([top, bot], axis=-2)


def _pair_unblockdiag(sp):
    b = sp.shape[0]
    s0 = sp[:, :, :HEAD_DIM, :HEAD_DIM]
    s1 = sp[:, :, HEAD_DIM:, HEAD_DIM:]
    return jnp.stack([s0, s1], axis=2).reshape(b, N_HEADS, HEAD_DIM, HEAD_DIM)


def _rope_tables(pos):
    half = HEAD_DIM // 2
    inv = ROPE_BASE ** (-jnp.arange(half, dtype=F32) / half)
    ang = pos.astype(F32)[:, None] * inv[None, :]
    cos, sin = jnp.cos(ang), jnp.sin(ang)
    cos_t = jnp.tile(cos, (1, PAIR // half))
    sin_t = jnp.tile(jnp.concatenate([-sin, sin], axis=-1), (1, 2))
    return cos_t, sin_t


def _log_gamma():
    return jnp.log1p(-jnp.exp2(-5.0 - jnp.arange(N_HEADS, dtype=F32)))


def _retention(z3, n_seq, seq, rows, groups, pos0, ret_gain, state_pairs, mm_dtype, name):
    n = n_seq * seq
    nc = (seq * groups) // rows
    nb = n_seq // groups
    npair = SEG // PAIR
    cos_t, sin_t = _rope_tables(pos0 + jnp.arange(seq, dtype=jnp.int32))
    if groups > 1:
        cos_t, sin_t = jnp.tile(cos_t, (groups, 1)), jnp.tile(sin_t, (groups, 1))
    z4 = z3.reshape(z3.shape[0], n // rows, rows, SEG)
    tok = lambda s: pl.BlockSpec((None, None, rows, PAIR), lambda b, p, c, s=s: (s, b * nc + c, 0, p))
    o, sp = pl.pallas_call(
        functools.partial(_retention_kernel, groups=groups, mm_dtype=mm_dtype),
        grid=(nb, npair, nc),
        in_specs=[pl.BlockSpec(memory_space=pltpu.SMEM),
                  tok(4), tok(5), tok(6), tok(7),
                  pl.BlockSpec((rows, PAIR), lambda b, p, c: (c, 0)),
                  pl.BlockSpec((rows, PAIR), lambda b, p, c: (c, 0)),
                  pl.BlockSpec((1, PAIR), lambda b, p, c: (0, p)),
                  pl.BlockSpec((groups, None, PAIR, PAIR), lambda b, p, c: (b, p, 0, 0))],
        out_specs=[pl.BlockSpec((None, rows, PAIR), lambda b, p, c: (b * nc + c, 0, p)),
                   pl.BlockSpec((groups, None, PAIR, PAIR), lambda b, p, c: (b, p, 0, 0))],
        out_shape=[jax.ShapeDtypeStruct((n // rows, rows, SEG), BF16),
                   jax.ShapeDtypeStruct((n_seq, npair, PAIR, PAIR), F32)],
        scratch_shapes=[pltpu.VMEM((groups, PAIR, PAIR), F32)],
        compiler_params=_cparams(3),
        name=name,
    )(_log_gamma(), z4, z4, z4, z4, cos_t, sin_t, ret_gain.astype(F32).reshape(1, SEG), state_pairs)
    return o.reshape(n, SEG), sp


def _even_weights(w_in, w_out):
    d = w_in.shape[0]
    w3 = w_in.reshape(d, 8, SEG).transpose(1, 0, 2).astype(BF16)
    w2 = w_out.reshape(2, SEG, d).astype(BF16)
    return w3, w2


def _even_layer_prompt(x, norm_g, w_in, q_gain, k_gain, t5_table, ret_gain, w_out):
    b, l, d = x.shape
    x2d = x.reshape(b * l, d)
    w3, w2 = _even_weights(w_in, w_out)
    z3 = _proj_in(x2d, norm_g, w3)
    oa, ka = _moba_prompt(z3, b, l, t5_table, q_gain, k_gain)
    zero_state = jnp.zeros((b, SEG // PAIR, PAIR, PAIR), F32)
    orr, sp = _retention(z3, b, l, min(l, 256), 1, 0, ret_gain, zero_state, BF16, "retention_prompt")
    y = _proj_out(oa, orr, x2d, w2)
    return (y.reshape(b, l, d), ka.reshape(b, l, N_HEADS, HEAD_DIM),
            z3[2].reshape(b, l, N_HEADS, HEAD_DIM), _pair_unblockdiag(sp))


def _log_sigmoid(x):
    y = -x
    return -(jnp.maximum(y, 0.0) + jnp.log1p(jnp.exp(-jnp.abs(y))))


def _logf_kernel(fz_ref, b_ref, lf_ref, fc_ref):
    seq = fz_ref.shape[1]
    t = ATT_TILE
    tri = (lax.broadcasted_iota(jnp.int32, (t, t), 0) >= lax.broadcasted_iota(jnp.int32, (t, t), 1)).astype(F32)
    carry = jnp.zeros((1, PAIR), F32)
    for c in range(seq // t):
        lf = _log_sigmoid(fz_ref[0, c * t:(c + 1) * t, :] + b_ref[...])
        cs = jnp.dot(tri, lf, precision=HIGHEST, preferred_element_type=F32) + carry
        lf_ref[0, c * t:(c + 1) * t, :] = lf[:, :N_HEADS]
        fc_ref[0, c * t:(c + 1) * t, :] = cs[:, :N_HEADS]
        carry = cs[t - 1:t, :]


def _logf_prompt(z3, batch, seq, b_forget):
    bpad = jnp.zeros((1, PAIR), F32).at[0, :N_HEADS].set(b_forget.astype(F32))
    return pl.pallas_call(
        _logf_kernel,
        grid=(batch,),
        in_specs=[pl.BlockSpec((1, seq, PAIR), lambda b: (7, b, 0)),
                  pl.BlockSpec((1, PAIR), lambda b: (0, 0))],
        out_specs=[pl.BlockSpec((1, seq, N_HEADS), lambda b: (b, 0, 0)),
                   pl.BlockSpec((1, seq, N_HEADS), lambda b: (b, 0, 0))],
        out_shape=[jax.ShapeDtypeStruct((batch, seq, N_HEADS), F32),
                   jax.ShapeDtypeStruct((batch, seq, N_HEADS), F32)],
        compiler_params=_cparams(1),
        name="logf_prompt",
    )(z3, bpad)


def _lane_split3(x):
    hi = x.astype(BF16).astype(F32)
    rest = x - hi
    mid = rest.astype(BF16).astype(F32)
    r = lax.broadcasted_iota(jnp.int32, x.shape, 1) % 3
    return jnp.where(r == 0, hi, jnp.where(r == 1, mid, rest - mid))


def _fox_prompt_kernel(q_ref, k_ref, v_ref, g_ref, fq_ref, fk_ref, qg_ref, kg_ref,
                       o_ref, kc_ref,
                       kn_ref, vb_ref, m_ref, l_ref, acc_ref):
    p = pl.program_id(1)
    i = pl.program_id(2)
    t = ATT_TILE
    nblk = k_ref.shape[1] // t
    lo = _lo_mask()
    lane = lax.broadcasted_iota(jnp.int32, (t, PAIR), 1)
    prow = lax.broadcasted_iota(jnp.int32, (N_HEADS, PAIR), 0)
    plane = lax.broadcasted_iota(jnp.int32, (N_HEADS, PAIR), 1)

    @pl.when(i == 0)
    def _prep():
        kn_ref[0:t, 0:PAIR] = jnp.zeros((t, PAIR), BF16)
        kn_ref[0:t, PAIR:2 * PAIR] = jnp.where((lane == 0) | (lane == 3), NEG, 0.0).astype(BF16)
        vb_ref[0:t, :] = jnp.zeros((t, PAIR), BF16)
        place = jnp.where(((prow == 2 * p) & (plane < 3)) | ((prow == 2 * p + 1) & (plane >= 3) & (plane < 6)),
                          -1.0, 0.0)

        def body(c, carry):
            r0 = pl.multiple_of(c * t, t)
            kn = _pair_rmsnorm(k_ref[0, pl.ds(r0, t), :], kg_ref[...], lo)
            kc_ref[pl.ds(r0, t), :] = kn
            kn_ref[pl.ds(r0 + t, t), 0:PAIR] = kn.astype(BF16)
            neg_fk = jnp.dot(fk_ref[0, pl.ds(r0, t), :], place, precision=HIGHEST, preferred_element_type=F32)
            up = jnp.where((lane >= 6) & (lane < 9), 1.0, _lane_split3(neg_fk))
            kn_ref[pl.ds(r0 + t, t), PAIR:2 * PAIR] = up.astype(BF16)
            vb_ref[pl.ds(r0 + t, t), :] = v_ref[0, pl.ds(r0, t), :].astype(BF16)
            return carry
        lax.fori_loop(0, nblk, body, 0)

    qn = _pair_rmsnorm(q_ref[0], qg_ref[...], lo)
    qs, ups = [], []
    for hh in range(2):
        hm = lo if hh == 0 else jnp.logical_not(lo)
        qs.append(jnp.where(hm, qn, 0.0) * (HEAD_DIM ** -0.5))
        place_q = jnp.where((prow == 2 * p + hh) & (plane >= 6) & (plane < 9), 1.0, 0.0)
        fq = jnp.dot(fq_ref[0], place_q, precision=HIGHEST, preferred_element_type=F32)
        ups.append(jnp.where((lane >= 3 * hh) & (lane < 3 * hh + 3), 1.0, _lane_split3(fq)))
    q2 = jnp.concatenate([jnp.concatenate(qs, axis=0), jnp.concatenate(ups, axis=0)], axis=1).astype(BF16)

    def tile_rows(step):
        return pl.ds(pl.multiple_of((i - 2 * step) * t, t), 2 * t)

    def logits(step, first):
        s = _nt_dot(q2, kn_ref[tile_rows(step), :])
        if first:
            row = lax.broadcasted_iota(jnp.int32, (2 * t, 2 * t), 0) % t
            col = lax.broadcasted_iota(jnp.int32, (2 * t, 2 * t), 1) - t
            s = jnp.where(row >= col, s, NEG)
        return s

    o = _flash_attention((i + 2) // 2, logits, lambda step: vb_ref[tile_rows(step), :], m_ref, l_ref, acc_ref)
    o = jnp.where(lo, o[0:t], o[t:2 * t])
    o_ref[...] = (o * _silu(g_ref[0])).astype(BF16)


def _fox_prompt(z3, batch, seq, fcum, q_gain, k_gain):
    n = batch * seq
    t = ATT_TILE
    nq = seq // t
    npair = SEG // PAIR
    gain2 = lambda g: jnp.tile(g.astype(F32), 2).reshape(1, PAIR)
    return pl.pallas_call(
        _fox_prompt_kernel,
        grid=(batch, npair, nq),
        in_specs=[pl.BlockSpec((1, t, PAIR), lambda b, p, i: (0, b * nq + i, p)),
                  pl.BlockSpec((1, seq, PAIR), lambda b, p, i: (1, b, p)),
                  pl.BlockSpec((1, seq, PAIR), lambda b, p, i: (2, b, p)),
                  pl.BlockSpec((1, t, PAIR), lambda b, p, i: (3, b * nq + i, p)),
                  pl.BlockSpec((1, t, N_HEADS), lambda b, p, i: (b, i, 0)),
                  pl.BlockSpec((1, seq, N_HEADS), lambda b, p, i: (b, 0, 0)),
                  pl.BlockSpec((1, PAIR), lambda b, p, i: (0, 0)),
                  pl.BlockSpec((1, PAIR), lambda b, p, i: (0, 0))],
        out_specs=[pl.BlockSpec((t, PAIR), lambda b, p, i: (b * nq + i, p)),
                   pl.BlockSpec((seq, PAIR), lambda b, p, i: (b, p))],
        out_shape=[jax.ShapeDtypeStruct((n, SEG), BF16),
                   jax.ShapeDtypeStruct((n, SEG), F32)],
        scratch_shapes=[pltpu.VMEM((seq + t, 2 * PAIR), BF16),
                        pltpu.VMEM((seq + t, PAIR), BF16),
                        pltpu.VMEM((2 * t, PAIR), F32),
                        pltpu.VMEM((2 * t, PAIR), F32),
                        pltpu.VMEM((2 * t, PAIR), F32)],
        compiler_params=_cparams(3),
        name="fox_prompt",
    )(z3, z3, z3, z3, fcum, fcum, gain2(q_gain), gain2(k_gain))


CONV_HALO = 32
CONV_ROWS = 32


def _conv_prompt_kernel(ua_ref, ub_ref, gd_ref, uah_ref, ubh_ref, st_ref, w_ref, cb_ref, cg_ref,
                        d_ref, cs_ref, buf_ref, y_ref):
    ti = pl.program_id(1)
    nt = pl.num_programs(1)
    t = ua_ref.shape[1]
    pad = CONV_HALO - (CONV_WIDTH - 1)
    buf_ref[CONV_HALO:CONV_HALO + t, :] = ua_ref[0] * jax.nn.sigmoid(ub_ref[0])

    @pl.when(ti == 0)
    def _():
        buf_ref[0:CONV_HALO, :] = st_ref[0]

    @pl.when(ti > 0)
    def _():
        buf_ref[0:CONV_HALO, :] = uah_ref[0] * jax.nn.sigmoid(ubh_ref[0])

    for lg in range(SEG // PAIR):
        ls = slice(lg * PAIR, (lg + 1) * PAIR)
        for rc in range(t // CONV_ROWS):
            r0 = rc * CONV_ROWS
            acc = jnp.zeros((CONV_ROWS, PAIR), F32) + cb_ref[:, ls]
            for k in range(CONV_WIDTH):
                acc = acc + w_ref[k:k + 1, ls] * buf_ref[r0 + pad + k:r0 + pad + k + CONV_ROWS, ls]
            y_ref[r0:r0 + CONV_ROWS, ls] = acc

    y = y_ref[...]
    mu = jnp.mean(y, axis=-1, keepdims=True)
    yc = y - mu
    var = jnp.mean(yc * yc, axis=-1, keepdims=True)
    yn = yc * lax.rsqrt(var + EPS) * cg_ref[...]
    d_ref[...] = (_silu(yn) * _silu(gd_ref[0])).astype(BF16)

    @pl.when(ti == nt - 1)
    def _():
        cs_ref[0] = buf_ref[t + pad:t + CONV_HALO, :]


def _conv_prompt(z3, batch, seq, state, conv_w, conv_b, conv_gain):
    n = batch * seq
    t = min(seq, 256)
    nt = seq // t
    hb = t // CONV_HALO
    st = jnp.pad(state.astype(F32), ((0, 0), (CONV_HALO - (CONV_WIDTH - 1), 0), (0, 0)))
    tok = lambda s: pl.BlockSpec((1, t, SEG), lambda b, i, s=s: (s, b * nt + i, 0))
    halo = lambda s: pl.BlockSpec((1, CONV_HALO, SEG),
                                  lambda b, i, s=s: (s, jnp.maximum((b * nt + i) * hb - 1, 0), 0))
    row = pl.BlockSpec((1, SEG), lambda b, i: (0, 0))
    return pl.pallas_call(
        _conv_prompt_kernel,
        grid=(batch, nt),
        in_specs=[tok(4), tok(5), tok(6), halo(4), halo(5),
                  pl.BlockSpec((1, CONV_HALO, SEG), lambda b, i: (b, 0, 0)),
                  pl.BlockSpec((CONV_WIDTH, SEG), lambda b, i: (0, 0)), row, row],
        out_specs=[pl.BlockSpec((t, SEG), lambda b, i: (b * nt + i, 0)),
                   pl.BlockSpec((1, CONV_WIDTH - 1, SEG), lambda b, i: (b, 0, 0))],
        out_shape=[jax.ShapeDtypeStruct((n, SEG), BF16),
                   jax.ShapeDtypeStruct((batch, CONV_WIDTH - 1, SEG), F32)],
        scratch_shapes=[pltpu.VMEM((t + CONV_HALO, SEG), F32),
                        pltpu.VMEM((t, SEG), F32)],
        compiler_params=_cparams(2),
        name="conv_prompt",
    )(z3, z3, z3, z3, z3, st, conv_w.astype(F32), conv_b.astype(F32).reshape(1, SEG),
      conv_gain.astype(F32).reshape(1, SEG))


def _odd_weights(w_in, w_out):
    d = w_in.shape[0]
    w = SEG
    fz = w_in[:, 4 * w:4 * w + N_HEADS]
    rest = jnp.concatenate([w_in[:, :4 * w], w_in[:, 4 * w + N_HEADS:]], axis=1)
    fz_pad = jnp.pad(fz, ((0, 0), (0, w - N_HEADS)))
    w3 = jnp.concatenate([rest, fz_pad], axis=1).reshape(d, 8, w).transpose(1, 0, 2).astype(BF16)
    w2 = w_out.reshape(2, w, d).astype(BF16)
    return w3, w2


def _odd_layer_prompt(x, norm_g, w_in, b_forget, q_gain, k_gain, conv_w, conv_b, conv_gain, w_out):
    b, l, d = x.shape
    x2d = x.reshape(b * l, d)
    w3, w2 = _odd_weights(w_in, w_out)
    z3 = _proj_in(x2d, norm_g, w3)
    logf, fcum = _logf_prompt(z3, b, l, b_forget)
    oc, kc = _fox_prompt(z3, b, l, fcum, q_gain, k_gain)
    dd, cs = _conv_prompt(z3, b, l, jnp.zeros((b, CONV_WIDTH - 1, SEG), F32), conv_w, conv_b, conv_gain)
    y = _proj_out(oc, dd, x2d, w2)
    return (y.reshape(b, l, d), kc.reshape(b, l, N_HEADS, HEAD_DIM),
            z3[2].reshape(b, l, N_HEADS, HEAD_DIM), logf, cs)


def _heads_rmsnorm(x, gain):
    lo = _lo_mask()
    parts = [_pair_rmsnorm(x[:, g * PAIR:(g + 1) * PAIR], gain[:, g * PAIR:(g + 1) * PAIR], lo)
             for g in range(SEG // PAIR)]
    return jnp.concatenate(parts, axis=-1)


def _query_rows(qn, n_tok):
    sub = lax.broadcasted_iota(jnp.int32, (N_HEADS, SEG), 0)
    head_of_lane = lax.broadcasted_iota(jnp.int32, (N_HEADS, SEG), 1) // HEAD_DIM
    own = sub == head_of_lane
    return jnp.concatenate([jnp.where(own, jnp.broadcast_to(qn[q:q + 1, :], (N_HEADS, SEG)), 0.0)
                            for q in range(n_tok)], axis=0)


def _rows_to_tokens(rows_out, n_tok):
    sub = lax.broadcasted_iota(jnp.int32, (N_HEADS, SEG), 0)
    head_of_lane = lax.broadcasted_iota(jnp.int32, (N_HEADS, SEG), 1) // HEAD_DIM
    own = sub == head_of_lane
    tok = lax.broadcasted_iota(jnp.int32, (n_tok, SEG), 0)
    out = jnp.zeros((n_tok, SEG), F32)
    for q in range(n_tok):
        o_q = jnp.sum(jnp.where(own, rows_out[q * N_HEADS:(q + 1) * N_HEADS, :], 0.0), axis=0, keepdims=True)
        out = jnp.where(tok == q, o_q, out)
    return out


def _t5_bias_rows(rel, tab):
    bias = jnp.broadcast_to(tab[:, T5_BUCKETS - 1:T5_BUCKETS], rel.shape)
    for b in range(T5_BUCKETS - 2, -1, -1):
        bias = jnp.where(rel <= T5_UPPER[b], tab[:, b:b + 1], bias)
    return bias


def _finish_rows(s_past, s_new, v_pages, v_new, n_tok):
    m = jnp.max(s_past, axis=-1, keepdims=True)
    for col in s_new:
        m = jnp.maximum(m, col)
    p_past = jnp.exp(s_past - m)
    l = jnp.sum(p_past, axis=-1, keepdims=True)
    pb = p_past.astype(BF16)
    acc = jnp.zeros((s_past.shape[0], SEG), F32)
    for pg, v_ref in enumerate(v_pages):
        acc = acc + _nt_dot(pb[:, pg * PAGE_SIZE:(pg + 1) * PAGE_SIZE], v_ref[0].astype(BF16))
    for kj, col in enumerate(s_new):
        p_new = jnp.exp(col - m)
        l = l + p_new
        acc = acc + p_new * v_new[kj:kj + 1, :]
    return _rows_to_tokens(acc / l, n_tok)


def _moba_sample_kernel(pt_ref, tab_ref, q_ref, k_ref, v_ref, g_ref, qg_ref, kg_ref, *rest, n_pages, n_tok):
    k_pages = rest[:n_pages]
    v_pages = rest[n_pages:2 * n_pages]
    o_ref, ka_ref = rest[2 * n_pages:]
    pages_per_block = MOBA_BLOCK // PAGE_SIZE
    nb_past = n_pages // pages_per_block
    past_len = n_pages * PAGE_SIZE
    n_rows = n_tok * N_HEADS

    qn = _heads_rmsnorm(q_ref[0, 0], qg_ref[...])
    kn = _heads_rmsnorm(k_ref[0, 0], kg_ref[...])
    ka_ref[0] = kn
    v_new = v_ref[0, 0]
    qrows = _query_rows(qn, n_tok)
    qb = (qrows * (HEAD_DIM ** -0.5)).astype(BF16)
    tab = tab_ref[...]
    row_q = lax.broadcasted_iota(jnp.int32, (n_rows, 1), 0) // N_HEADS

    raw = [jnp.dot(qb, k_pages[pg][0].astype(BF16), preferred_element_type=F32) for pg in range(n_pages)]
    blk_lane = lax.broadcasted_iota(jnp.int32, (SEG, nb_past), 1)
    k_mean = jnp.zeros((SEG, nb_past), F32)
    for n in range(nb_past):
        tot = k_pages[n * pages_per_block][0]
        for pg in range(n * pages_per_block + 1, (n + 1) * pages_per_block):
            tot = tot + k_pages[pg][0]
        k_mean = jnp.where(blk_lane == n, jnp.sum(tot, axis=-1, keepdims=True) * (1.0 / MOBA_BLOCK), k_mean)
    gate = jnp.dot(qrows, k_mean, precision=HIGHEST, preferred_element_type=F32)
    selpen = jnp.where(_top_blocks(gate, nb_past, nb_past, nb_past), 0.0, NEG)

    far = tab[:, T5_BUCKETS - 1:T5_BUCKETS]
    lane = lax.broadcasted_iota(jnp.int32, (n_rows, PAGE_SIZE), 1)
    tiles = []
    for pg in range(n_pages):
        n = pg // pages_per_block
        min_rel = past_len - (pg + 1) * PAGE_SIZE + 1
        if min_rel > T5_UPPER[-1]:
            bias = far
        else:
            bias = _t5_bias_rows(past_len + row_q - (pg * PAGE_SIZE + lane), tab)
        tiles.append(raw[pg] + bias + selpen[:, n:n + 1])
    s_past = jnp.concatenate(tiles, axis=-1)

    s_new = []
    for kj in range(n_tok):
        dot = jnp.sum(qrows * kn[kj:kj + 1, :], axis=-1, keepdims=True) * (HEAD_DIM ** -0.5)
        rel = row_q - kj
        s_new.append(jnp.where(rel >= 0, dot + _t5_bias_rows(rel, tab), NEG))

    o = _finish_rows(s_past, s_new, v_pages, v_new, n_tok)
    o_ref[0] = (o * _silu(g_ref[0, 0])).astype(BF16)


def _page_specs(n_pages, rows):
    return [pl.BlockSpec((1, rows, PAGE_SIZE), lambda b, pt, pg=pg: (pt[b, pg], 0, 0)) for pg in range(n_pages)]


def _pages_token_minor(cache):
    n_pool = cache.shape[0]
    return cache.transpose(0, 2, 3, 1).reshape(n_pool, SEG, PAGE_SIZE)


def _moba_sample(z3, n_seq, n_tok, cache_k, cache_v, page_table, t5_table, q_gain, k_gain):
    n_pages = page_table.shape[1]
    n_pool = cache_k.shape[0]
    assert MOBA_BLOCK % PAGE_SIZE == 0 and (n_pages * PAGE_SIZE) % MOBA_BLOCK == 0
    assert n_tok <= MOBA_BLOCK and n_pages // (MOBA_BLOCK // PAGE_SIZE) >= MOBA_TOPK
    z4 = z3.reshape(z3.shape[0], n_seq, n_tok, SEG)
    tok = lambda s: pl.BlockSpec((1, 1, n_tok, SEG), lambda b, pt, s=s: (s, b, 0, 0))
    row = pl.BlockSpec((1, SEG), lambda b, pt: (0, 0))
    gain8 = lambda g: jnp.tile(g.astype(F32), N_HEADS).reshape(1, SEG)
    tab_rows = jnp.tile(t5_table.astype(F32).T, (n_tok, 1))
    grid_spec = pltpu.PrefetchScalarGridSpec(
        num_scalar_prefetch=1,
        grid=(n_seq,),
        in_specs=[pl.BlockSpec((n_tok * N_HEADS, T5_BUCKETS), lambda b, pt: (0, 0)),
                  tok(0), tok(1), tok(2), tok(3), row, row]
                 + _page_specs(n_pages, SEG) + _page_specs(n_pages, SEG),
        out_specs=[pl.BlockSpec((1, n_tok, SEG), lambda b, pt: (b, 0, 0)),
                   pl.BlockSpec((1, n_tok, SEG), lambda b, pt: (b, 0, 0))])
    ck = _pages_token_minor(cache_k)
    cv = _pages_token_minor(cache_v)
    return pl.pallas_call(
        functools.partial(_moba_sample_kernel, n_pages=n_pages, n_tok=n_tok),
        grid_spec=grid_spec,
        out_shape=[jax.ShapeDtypeStruct((n_seq, n_tok, SEG), BF16),
                   jax.ShapeDtypeStruct((n_seq, n_tok, SEG), F32)],
        compiler_params=_cparams(1),
        name="moba_sample",
    )(page_table, tab_rows, z4, z4, z4, z4, gain8(q_gain), gain8(k_gain),
      *([ck] * n_pages), *([cv] * n_pages))


def _fox_sample_kernel(pt_ref, q_ref, k_ref, v_ref, g_ref, fz_ref, fzr_ref, bf_ref, bfr_ref, qg_ref, kg_ref,
                       *rest, n_pages, n_tok):
    k_pages = rest[:n_pages]
    v_pages = rest[n_pages:2 * n_pages]
    f_pages = rest[2 * n_pages:3 * n_pages]
    o_ref, kc_ref, lf_ref = rest[3 * n_pages:]
    n_rows = n_tok * N_HEADS

    qn = _heads_rmsnorm(q_ref[0, 0], qg_ref[...])
    kn = _heads_rmsnorm(k_ref[0, 0], kg_ref[...])
    kc_ref[0] = kn
    v_new = v_ref[0, 0]
    lf_ref[0] = _log_sigmoid(fz_ref[0, 0][:, :N_HEADS] + bf_ref[...])
    qrows = _query_rows(qn, n_tok) * (HEAD_DIM ** -0.5)
    qb = qrows.astype(BF16)
    row_q = lax.broadcasted_iota(jnp.int32, (n_rows, 1), 0) // N_HEADS

    upper = (lax.broadcasted_iota(jnp.int32, (PAGE_SIZE, PAGE_SIZE), 0)
             <= lax.broadcasted_iota(jnp.int32, (PAGE_SIZE, PAGE_SIZE), 1)).astype(F32)
    carry = jnp.zeros((n_rows, 1), F32)
    fk_tiles = []
    for pg in range(n_pages):
        rows_lf = jnp.concatenate([f_pages[pg][0]] * n_tok, axis=0)
        cs = jnp.dot(rows_lf, upper, precision=HIGHEST, preferred_element_type=F32) + carry
        fk_tiles.append(cs)
        carry = cs[:, PAGE_SIZE - 1:PAGE_SIZE]
    lfr = _log_sigmoid(fzr_ref[0] + bfr_ref[...])
    cum_new = []
    run = carry
    for tk in range(n_tok):
        run = run + lfr[:, tk:tk + 1]
        cum_new.append(run)
    fq = cum_new[n_tok - 1]
    for tk in range(n_tok - 2, -1, -1):
        fq = jnp.where(row_q == tk, cum_new[tk], fq)

    tiles = [jnp.dot(qb, k_pages[pg][0].astype(BF16), preferred_element_type=F32) + (fq - fk_tiles[pg])
             for pg in range(n_pages)]
    s_past = jnp.concatenate(tiles, axis=-1)
    s_new = []
    for kj in range(n_tok):
        dot = jnp.sum(qrows * kn[kj:kj + 1, :], axis=-1, keepdims=True)
        s_new.append(jnp.where(row_q >= kj, dot + (fq - cum_new[kj]), NEG))

    o = _finish_rows(s_past, s_new, v_pages, v_new, n_tok)
    o_ref[0] = (o * _silu(g_ref[0, 0])).astype(BF16)


def _fox_sample(z3, n_seq, n_tok, cache_k, cache_v, cache_logf, page_table, b_forget, q_gain, k_gain):
    n_pages = page_table.shape[1]
    n_pool = cache_k.shape[0]
    z4 = z3.reshape(z3.shape[0], n_seq, n_tok, SEG)
    tok = lambda s: pl.BlockSpec((1, 1, n_tok, SEG), lambda b, pt, s=s: (s, b, 0, 0))
    row = pl.BlockSpec((1, SEG), lambda b, pt: (0, 0))
    gain8 = lambda g: jnp.tile(g.astype(F32), N_HEADS).reshape(1, SEG)
    fz = z4[7, :, :, :N_HEADS]
    fz_rows = jnp.tile(fz.transpose(0, 2, 1), (1, n_tok, 1))
    bf = b_forget.astype(F32)
    grid_spec = pltpu.PrefetchScalarGridSpec(
        num_scalar_prefetch=1,
        grid=(n_seq,),
        in_specs=[tok(0), tok(1), tok(2), tok(3),
                  pl.BlockSpec((1, 1, n_tok, PAIR), lambda b, pt: (7, b, 0, 0)),
                  pl.BlockSpec((1, n_tok * N_HEADS, n_tok), lambda b, pt: (b, 0, 0)),
                  pl.BlockSpec((1, N_HEADS), lambda b, pt: (0, 0)),
                  pl.BlockSpec((n_tok * N_HEADS, 1), lambda b, pt: (0, 0)),
                  row, row]
                 + _page_specs(n_pages, SEG) + _page_specs(n_pages, SEG) + _page_specs(n_pages, N_HEADS),
        out_specs=[pl.BlockSpec((1, n_tok, SEG), lambda b, pt: (b, 0, 0)),
                   pl.BlockSpec((1, n_tok, SEG), lambda b, pt: (b, 0, 0)),
                   pl.BlockSpec((1, n_tok, N_HEADS), lambda b, pt: (b, 0, 0))])
    ck = _pages_token_minor(cache_k)
    cv = _pages_token_minor(cache_v)
    cf = cache_logf.astype(F32).transpose(0, 2, 1)
    return pl.pallas_call(
        functools.partial(_fox_sample_kernel, n_pages=n_pages, n_tok=n_tok),
        grid_spec=grid_spec,
        out_shape=[jax.ShapeDtypeStruct((n_seq, n_tok, SEG), BF16),
                   jax.ShapeDtypeStruct((n_seq, n_tok, SEG), F32),
                   jax.ShapeDtypeStruct((n_seq, n_tok, N_HEADS), F32)],
        compiler_params=_cparams(1),
        name="fox_sample",
    )(page_table, z4, z4, z4, z4, z4, fz_rows, bf.reshape(1, N_HEADS),
      jnp.tile(bf, n_tok).reshape(n_tok * N_HEADS, 1), gain8(q_gain), gain8(k_gain),
      *([ck] * n_pages), *([cv] * n_pages), *([cf] * n_pages))


FLAT = PAGE_SIZE * N_HEADS


def _flat_rmsnorm(x, gain):
    return x * lax.rsqrt(jnp.mean(x * x, axis=-1, keepdims=True) + EPS) * gain


def _flat_softmax_pv(tiles, v_pages, s_new, v_new):
    m = jnp.max(s_new, axis=-1, keepdims=True)
    for s in tiles:
        m = jnp.maximum(m, jnp.max(s, axis=-1, keepdims=True))
    p_new = jnp.exp(s_new - m)
    l = jnp.sum(p_new, axis=-1, keepdims=True)
    acc = jnp.dot(p_new, v_new, precision=HIGHEST, preferred_element_type=F32)
    for s, v_ref in zip(tiles, v_pages):
        pr = jnp.exp(s - m)
        l = l + jnp.sum(pr, axis=-1, keepdims=True)
        acc = acc + jnp.dot(pr.astype(BF16), v_ref[0].astype(BF16), preferred_element_type=F32)
    return acc / l


def _moba_flat_kernel(pt_ref, tab_ref, q_ref, k_ref, v_ref, g_ref, qg_ref, kg_ref, *rest, n_pages, n_tok):
    k_pages = rest[:n_pages]
    v_pages = rest[n_pages:2 * n_pages]
    o_ref, ka_ref = rest[2 * n_pages:]
    pages_per_block = MOBA_BLOCK // PAGE_SIZE
    nb_past = n_pages // pages_per_block
    past_len = n_pages * PAGE_SIZE
    n_rows = n_tok * N_HEADS
    scale = HEAD_DIM ** -0.5

    qn = _flat_rmsnorm(q_ref[0], qg_ref[...])
    kn = _flat_rmsnorm(k_ref[0], kg_ref[...])
    ka_ref[0] = kn
    tab = tab_ref[...]
    row = lax.broadcasted_iota(jnp.int32, (n_rows, 1), 0)
    row_q, row_h = row // N_HEADS, row % N_HEADS

    blk_lane = lax.broadcasted_iota(jnp.int32, (n_rows, nb_past), 1)
    gate = jnp.zeros((n_rows, nb_past), F32)
    for n in range(nb_past):
        tot = jnp.zeros((N_HEADS, HEAD_DIM), F32)
        for pg in range(n * pages_per_block, (n + 1) * pages_per_block):
            tot = tot + jnp.sum(k_pages[pg][0].reshape(PAGE_SIZE, N_HEADS, HEAD_DIM), axis=0)
        k_mean = jnp.concatenate([tot * (1.0 / MOBA_BLOCK)] * n_tok, axis=0)
        gate = jnp.where(blk_lane == n, jnp.sum(qn * k_mean, axis=-1, keepdims=True), gate)
    selpen = jnp.where(_top_blocks(gate, nb_past, nb_past, nb_past), 0.0, NEG)

    qb = (qn * scale).astype(BF16)
    col = lax.broadcasted_iota(jnp.int32, (n_rows, FLAT), 1)
    own_head = (col % N_HEADS) == row_h
    far = tab[:, T5_BUCKETS - 1:T5_BUCKETS]
    tiles = []
    for pg in range(n_pages):
        n = pg // pages_per_block
        s = _nt_dot(qb, k_pages[pg][0].astype(BF16))
        if past_len - (pg + 1) * PAGE_SIZE + 1 > T5_UPPER[-1]:
            bias = far
        else:
            bias = _t5_bias_rows(past_len + row_q - (pg * PAGE_SIZE + col // N_HEADS), tab)
        tiles.append(jnp.where(own_head, s + (bias + selpen[:, n:n + 1]), NEG))

    ncol = lax.broadcasted_iota(jnp.int32, (n_rows, n_rows), 1)
    rel = row_q - ncol // N_HEADS
    s_new = _nt_dot(qn * scale, kn, precision=HIGHEST) + _t5_bias_rows(rel, tab)
    s_new = jnp.where(((ncol % N_HEADS) == row_h) & (rel >= 0), s_new, NEG)

    o = _flat_softmax_pv(tiles, v_pages, s_new, v_ref[0])
    o_ref[0] = (o * _silu(g_ref[0])).astype(BF16)


def _flat_page_specs(n_pages, rows, width):
    return [pl.BlockSpec((1, rows, width), lambda b, pt, pg=pg: (pt[b, pg], 0, 0)) for pg in range(n_pages)]


def _flat_tokens(z3, seg, n_seq, n_tok):
    return z3[seg].reshape(n_seq, n_tok * N_HEADS, HEAD_DIM)


def _moba_flat(z3, n_seq, n_tok, cache_k, cache_v, page_table, t5_table, q_gain, k_gain):
    n_pages = page_table.shape[1]
    n_pool = cache_k.shape[0]
    assert MOBA_BLOCK % PAGE_SIZE == 0 and (n_pages * PAGE_SIZE) % MOBA_BLOCK == 0
    assert n_tok <= MOBA_BLOCK and n_pages // (MOBA_BLOCK // PAGE_SIZE) >= MOBA_TOPK
    n_rows = n_tok * N_HEADS
    tok = pl.BlockSpec((1, n_rows, HEAD_DIM), lambda b, pt: (b, 0, 0))
    gain = pl.BlockSpec((1, HEAD_DIM), lambda b, pt: (0, 0))
    tab_rows = jnp.tile(t5_table.astype(F32).T, (n_tok, 1))
    grid_spec = pltpu.PrefetchScalarGridSpec(
        num_scalar_prefetch=1,
        grid=(n_seq,),
        in_specs=[pl.BlockSpec((n_rows, T5_BUCKETS), lambda b, pt: (0, 0)), tok, tok, tok, tok, gain, gain]
                 + _flat_page_specs(n_pages, FLAT, HEAD_DIM) + _flat_page_specs(n_pages, FLAT, HEAD_DIM),
        out_specs=[tok, tok])
    ck = cache_k.reshape(n_pool, FLAT, HEAD_DIM)
    cv = cache_v.reshape(n_pool, FLAT, HEAD_DIM)
    flat = lambda seg: _flat_tokens(z3, seg, n_seq, n_tok)
    return pl.pallas_call(
        functools.partial(_moba_flat_kernel, n_pages=n_pages, n_tok=n_tok),
        grid_spec=grid_spec,
        out_shape=[jax.ShapeDtypeStruct((n_seq, n_rows, HEAD_DIM), BF16),
                   jax.ShapeDtypeStruct((n_seq, n_rows, HEAD_DIM), F32)],
        compiler_params=_cparams(1),
        name="moba_sample",
    )(page_table, tab_rows, flat(0), flat(1), flat(2), flat(3),
      q_gain.astype(F32).reshape(1, HEAD_DIM), k_gain.astype(F32).reshape(1, HEAD_DIM),
      *([ck] * n_pages), *([cv] * n_pages))


LOGF_CHUNK = PAIR // N_HEADS


def _fox_flat_kernel(pt_ref, q_ref, k_ref, v_ref, g_ref, fz_ref, fzr_ref, bf_ref, bfr_ref, qg_ref, kg_ref,
                     *rest, n_pages, n_tok):
    k_pages = rest[:n_pages]
    v_pages = rest[n_pages:2 * n_pages]
    f_pages = rest[2 * n_pages:3 * n_pages]
    o_ref, kc_ref, lf_ref = rest[3 * n_pages:]
    n_rows = n_tok * N_HEADS
    chunks = PAGE_SIZE // LOGF_CHUNK
    scale = HEAD_DIM ** -0.5

    qn = _flat_rmsnorm(q_ref[0], qg_ref[...])
    kn = _flat_rmsnorm(k_ref[0], kg_ref[...])
    kc_ref[0] = kn
    lf_ref[0] = _log_sigmoid(fz_ref[0] + bf_ref[...])
    row = lax.broadcasted_iota(jnp.int32, (n_rows, 1), 0)
    row_q, row_h = row // N_HEADS, row % N_HEADS

    lf = jnp.concatenate([f_pages[pg][0] for pg in range(n_pages)], axis=0)
    nr = lf.shape[0]
    li = lax.broadcasted_iota(jnp.int32, (PAIR, PAIR), 0)
    lj = lax.broadcasted_iota(jnp.int32, (PAIR, PAIR), 1)
    scan = ((li % N_HEADS == lj % N_HEADS) & (li <= lj)).astype(F32)
    last = (li == PAIR - N_HEADS + lj % N_HEADS).astype(F32)
    ri = lax.broadcasted_iota(jnp.int32, (nr, nr), 0)
    rj = lax.broadcasted_iota(jnp.int32, (nr, nr), 1)
    before = (rj < ri).astype(F32)
    in_row = jnp.dot(lf, scan, precision=HIGHEST, preferred_element_type=F32)
    totals = jnp.dot(in_row, last, precision=HIGHEST, preferred_element_type=F32)
    fk = in_row + jnp.dot(before, totals, precision=HIGHEST, preferred_element_type=F32)

    lane = lax.broadcasted_iota(jnp.int32, (n_rows, PAIR), 1)
    past_total = jnp.sum(jnp.where(lane == PAIR - N_HEADS + row_h, fk[nr - 1:nr, :], 0.0),
                         axis=-1, keepdims=True)
    lfr = _log_sigmoid(fzr_ref[0] + bfr_ref[...])
    cum_new = []
    run = past_total
    for tk in range(n_tok):
        run = run + lfr[:, tk:tk + 1]
        cum_new.append(run)
    fq = cum_new[n_tok - 1]
    for tk in range(n_tok - 2, -1, -1):
        fq = jnp.where(row_q == tk, cum_new[tk], fq)

    qb = (qn * scale).astype(BF16)
    col = lax.broadcasted_iota(jnp.int32, (n_rows, FLAT), 1)
    own_head = (col % N_HEADS) == row_h
    tiles = []
    for pg in range(n_pages):
        s = _nt_dot(qb, k_pages[pg][0].astype(BF16))
        fk_row = jnp.concatenate([fk[pg * chunks + c:pg * chunks + c + 1, :] for c in range(chunks)], axis=1)
        tiles.append(jnp.where(own_head, s + (fq - fk_row), NEG))

    ncol = lax.broadcasted_iota(jnp.int32, (n_rows, n_rows), 1)
    fk_new = cum_new[n_tok - 1]
    for tk in range(n_tok - 2, -1, -1):
        fk_new = jnp.where(ncol // N_HEADS == tk, cum_new[tk], fk_new)
    s_new = _nt_dot(qn * scale, kn, precision=HIGHEST) + (fq - fk_new)
    s_new = jnp.where(((ncol % N_HEADS) == row_h) & (ncol // N_HEADS <= row_q), s_new, NEG)

    o = _flat_softmax_pv(tiles, v_pages, s_new, v_ref[0])
    o_ref[0] = (o * _silu(g_ref[0])).astype(BF16)


def _fox_flat(z3, n_seq, n_tok, cache_k, cache_v, cache_logf, page_table, b_forget, q_gain, k_gain):
    n_pages = page_table.shape[1]
    n_pool = cache_k.shape[0]
    n_rows = n_tok * N_HEADS
    tok = pl.BlockSpec((1, n_rows, HEAD_DIM), lambda b, pt: (b, 0, 0))
    gain = pl.BlockSpec((1, HEAD_DIM), lambda b, pt: (0, 0))
    fz = z3[7].reshape(n_seq, n_tok, SEG)[:, :, :N_HEADS]
    fz_rows = jnp.tile(fz.transpose(0, 2, 1), (1, n_tok, 1))
    bf = b_forget.astype(F32)
    grid_spec = pltpu.PrefetchScalarGridSpec(
        num_scalar_prefetch=1,
        grid=(n_seq,),
        in_specs=[tok, tok, tok, tok,
                  pl.BlockSpec((1, n_tok, N_HEADS), lambda b, pt: (b, 0, 0)),
                  pl.BlockSpec((1, n_rows, n_tok), lambda b, pt: (b, 0, 0)),
                  pl.BlockSpec((1, N_HEADS), lambda b, pt: (0, 0)),
                  pl.BlockSpec((n_rows, 1), lambda b, pt: (0, 0)),
                  gain, gain]
                 + _flat_page_specs(n_pages, FLAT, HEAD_DIM) + _flat_page_specs(n_pages, FLAT, HEAD_DIM)
                 + _flat_page_specs(n_pages, PAGE_SIZE // LOGF_CHUNK, PAIR),
        out_specs=[tok, tok, pl.BlockSpec((1, n_tok, N_HEADS), lambda b, pt: (b, 0, 0))])
    ck = cache_k.reshape(n_pool, FLAT, HEAD_DIM)
    cv = cache_v.reshape(n_pool, FLAT, HEAD_DIM)
    cf = cache_logf.astype(F32).reshape(n_pool, PAGE_SIZE // LOGF_CHUNK, PAIR)
    flat = lambda seg: _flat_tokens(z3, seg, n_seq, n_tok)
    return pl.pallas_call(
        functools.partial(_fox_flat_kernel, n_pages=n_pages, n_tok=n_tok),
        grid_spec=grid_spec,
        out_shape=[jax.ShapeDtypeStruct((n_seq, n_rows, HEAD_DIM), BF16),
                   jax.ShapeDtypeStruct((n_seq, n_rows, HEAD_DIM), F32),
                   jax.ShapeDtypeStruct((n_seq, n_tok, N_HEADS), F32)],
        compiler_params=_cparams(1),
        name="fox_sample",
    )(page_table, flat(0), flat(1), flat(2), flat(3), fz, fz_rows, bf.reshape(1, N_HEADS),
      jnp.tile(bf, n_tok).reshape(n_rows, 1),
      q_gain.astype(F32).reshape(1, HEAD_DIM), k_gain.astype(F32).reshape(1, HEAD_DIM),
      *([ck] * n_pages), *([cv] * n_pages), *([cf] * n_pages))


CONV_SAMPLE_BATCH = 8


def _conv_sample_kernel(ua_ref, ub_ref, gd_ref, st_ref, w_ref, cb_ref, cg_ref, d_ref, cs_ref, buf_ref):
    n_tok = ua_ref.shape[2]
    hist = CONV_WIDTH - 1
    for e in range(ua_ref.shape[1]):
        buf_ref[0:hist, :] = st_ref[e]
        buf_ref[hist:hist + n_tok, :] = ua_ref[0, e] * jax.nn.sigmoid(ub_ref[0, e])
        y = jnp.zeros((n_tok, SEG), F32) + cb_ref[...]
        for k in range(CONV_WIDTH):
            y = y + w_ref[k:k + 1, :] * buf_ref[k:k + n_tok, :]
        mu = jnp.mean(y, axis=-1, keepdims=True)
        yc = y - mu
        var = jnp.mean(yc * yc, axis=-1, keepdims=True)
        yn = yc * lax.rsqrt(var + EPS) * cg_ref[...]
        d_ref[e] = (_silu(yn) * _silu(gd_ref[0, e])).astype(BF16)
        cs_ref[e] = buf_ref[n_tok:n_tok + hist, :]


def _conv_sample(z3, n_seq, n_tok, state, conv_w, conv_b, conv_gain):
    bt = math.gcd(n_seq, CONV_SAMPLE_BATCH)
    hist = CONV_WIDTH - 1
    z4 = z3.reshape(z3.shape[0], n_seq, n_tok, SEG)
    tok = lambda s: pl.BlockSpec((1, bt, n_tok, SEG), lambda b, s=s: (s, b, 0, 0))
    row = pl.BlockSpec((1, SEG), lambda b: (0, 0))
    return pl.pallas_call(
        _conv_sample_kernel,
        grid=(n_seq // bt,),
        in_specs=[tok(4), tok(5), tok(6),
                  pl.BlockSpec((bt, hist, SEG), lambda b: (b, 0, 0)),
                  pl.BlockSpec((CONV_WIDTH, SEG), lambda b: (0, 0)), row, row],
        out_specs=[pl.BlockSpec((bt, n_tok, SEG), lambda b: (b, 0, 0)),
                   pl.BlockSpec((bt, hist, SEG), lambda b: (b, 0, 0))],
        out_shape=[jax.ShapeDtypeStruct((n_seq, n_tok, SEG), BF16),
                   jax.ShapeDtypeStruct((n_seq, hist, SEG), F32)],
        scratch_shapes=[pltpu.VMEM((hist + n_tok + 6, SEG), F32)],
        compiler_params=_cparams(1),
        name="conv_sample",
    )(z4, z4, z4, state.astype(F32), conv_w.astype(F32), conv_b.astype(F32).reshape(1, SEG),
      conv_gain.astype(F32).reshape(1, SEG))


RET_SAMPLE_GROUPS = 2


def _even_layer_sample(x, cache_k, cache_v, ret_state, page_table, norm_g, w_in, q_gain, k_gain, t5_table,
                       ret_gain, w_out):
    b, l, d = x.shape
    x2d = x.reshape(b * l, d)
    w3, w2 = _even_weights(w_in, w_out)
    z3 = _proj_in(x2d, norm_g, w3)
    oa, ka = _moba_sample(z3, b, l, cache_k, cache_v, page_table, t5_table, q_gain, k_gain)
    past_len = page_table.shape[1] * PAGE_SIZE
    groups = math.gcd(b, RET_SAMPLE_GROUPS)
    orr, sp = _retention(z3, b, l, l * groups, groups, past_len, ret_gain, _pair_blockdiag(ret_state), F32,
                         "retention_sample")
    y = _proj_out(oa.reshape(b * l, SEG), orr, x2d, w2)
    return (y.reshape(b, l, d), ka.reshape(b, l, N_HEADS, HEAD_DIM),
            z3[2].reshape(b, l, N_HEADS, HEAD_DIM), _pair_unblockdiag(sp))


def _odd_layer_sample(x, cache_k, cache_v, cache_logf, conv_state, page_table, norm_g, w_in, b_forget, q_gain,
                      k_gain, conv_w, conv_b, conv_gain, w_out):
    b, l, d = x.shape
    x2d = x.reshape(b * l, d)
    w3, w2 = _odd_weights(w_in, w_out)
    z3 = _proj_in(x2d, norm_g, w3)
    oc, kc, logf = _fox_sample(z3, b, l, cache_k, cache_v, cache_logf, page_table, b_forget, q_gain, k_gain)
    dd, cs = _conv_sample(z3, b, l, conv_state, conv_w, conv_b, conv_gain)
    y = _proj_out(oc.reshape(b * l, SEG), dd.reshape(b * l, SEG), x2d, w2)
    return (y.reshape(b, l, d), kc.reshape(b, l, N_HEADS, HEAD_DIM),
            z3[2].reshape(b, l, N_HEADS, HEAD_DIM), logf, cs)


def kernel(x_prompt, x_sample, cache_moba_k, cache_moba_v, cache_fox_k, cache_fox_v, cache_fox_logf, state_ret, state_conv, page_table, norm_g_even, w_in_even, moba_q_gain, moba_k_gain, t5_table, ret_gain, w_out_even, norm_g_odd, w_in_odd, b_forget, fox_q_gain, fox_k_gain, conv_w, conv_b, conv_gain, w_out_odd):
    depth = norm_g_even.shape[0] + norm_g_odd.shape[0]
    xp, xs = x_prompt, x_sample
    outs = {name: [] for name in ("mk_p", "mv_p", "mk_s", "mv_s", "rs_p", "rs_s",
                                  "fk_p", "fv_p", "fl_p", "fk_s", "fv_s", "fl_s", "cs_p", "cs_s")}
    for layer in range(depth):
        i = layer // 2
        if layer % 2 == 0:
            w = (norm_g_even[i], w_in_even[i], moba_q_gain[i], moba_k_gain[i], t5_table, ret_gain[i], w_out_even[i])
            xp, k1, v1, s1 = _even_layer_prompt(xp, *w)
            xs, k2, v2, s2 = _even_layer_sample(xs, cache_moba_k[i], cache_moba_v[i], state_ret[i], page_table, *w)
            for name, val in zip(("mk_p", "mv_p", "rs_p", "mk_s", "mv_s", "rs_s"), (k1, v1, s1, k2, v2, s2)):
                outs[name].append(val)
        else:
            w = (norm_g_odd[i], w_in_odd[i], b_forget[i], fox_q_gain[i], fox_k_gain[i],
                 conv_w[i], conv_b[i], conv_gain[i], w_out_odd[i])
            xp, k1, v1, f1, c1 = _odd_layer_prompt(xp, *w)
            xs, k2, v2, f2, c2 = _odd_layer_sample(xs, cache_fox_k[i], cache_fox_v[i], cache_fox_logf[i],
                                                   state_conv[i], page_table, *w)
            for name, val in zip(("fk_p", "fv_p", "fl_p", "cs_p", "fk_s", "fv_s", "fl_s", "cs_s"),
                                 (k1, v1, f1, c1, k2, v2, f2, c2)):
                outs[name].append(val)
    st = lambda name: jnp.stack(outs[name])
    return (xp, xs, st("mk_p"), st("mv_p"), st("mk_s"), st("mv_s"), st("rs_p"), st("rs_s"),
            st("fk_p"), st("fv_p"), st("fl_p"), st("fk_s"), st("fv_s"), st("fl_s"), st("cs_p"), st("cs_s"))
```

```python
import functools
import math

import numpy as np
import jax
import jax.numpy as jnp
from jax import lax
from jax.experimental import pallas as pl
from jax.experimental.pallas import tpu as pltpu

F32 = jnp.float32
BF16 = jnp.bfloat16
HIGHEST = lax.Precision.HIGHEST

HEAD_DIM = 64
PAIR = 2 * HEAD_DIM
SEG = 512
N_HEADS = SEG // HEAD_DIM
EPS = 1e-6
MOBA_BLOCK = 256
MOBA_TOPK = 3
T5_BUCKETS = 32
T5_MAX_DIST = 128
ROPE_BASE = 10000.0
CONV_WIDTH = 31
PAGE_SIZE = 128
NEG = -1e30
ATT_TILE = 256
V7X_VMEM_LIMIT = 48 * 1024 * 1024


def _t5_bucket_upper_bounds():
    max_exact = T5_BUCKETS // 2
    rel = np.arange(0, 4 * T5_MAX_DIST)
    relf = np.maximum(rel, 1).astype(np.float64)
    large = max_exact + np.trunc(np.log(relf / max_exact) / math.log(T5_MAX_DIST / max_exact)
                                 * (T5_BUCKETS - max_exact)).astype(np.int64)
    bucket = np.where(rel < max_exact, rel, np.minimum(large, T5_BUCKETS - 1))
    return tuple(int(rel[bucket <= b].max()) for b in range(T5_BUCKETS - 1))


T5_UPPER = _t5_bucket_upper_bounds()


def _cparams(n_axes):
    return pltpu.CompilerParams(dimension_semantics=("arbitrary",) * n_axes,
                                vmem_limit_bytes=V7X_VMEM_LIMIT)


def _silu(x):
    return x * jax.nn.sigmoid(x)


def _lo_mask():
    return lax.broadcasted_iota(jnp.int32, (1, PAIR), 1) < HEAD_DIM


def _pair_sum(x, lo):
    s0 = jnp.sum(jnp.where(lo, x, 0.0), axis=-1, keepdims=True)
    s1 = jnp.sum(jnp.where(lo, 0.0, x), axis=-1, keepdims=True)
    return jnp.where(lo, s0, s1)


def _pair_rmsnorm(x, gain, lo):
    ms = _pair_sum(x * x, lo) * (1.0 / HEAD_DIM)
    return x * lax.rsqrt(ms + EPS) * gain


def _t5_bias(rel, tab_ref, h):
    bias = jnp.full(rel.shape, tab_ref[T5_BUCKETS - 1, h], F32)
    for b in range(T5_BUCKETS - 2, -1, -1):
        bias = jnp.where(rel <= T5_UPPER[b], tab_ref[b, h], bias)
    return bias


def _top_blocks(gate, n_valid, own, width):
    lane = lax.broadcasted_iota(jnp.int32, gate.shape, 1)
    g = jnp.where(lane < n_valid, gate, -jnp.inf)
    sel = lane == own
    for _ in range(MOBA_TOPK):
        m = jnp.max(g, axis=-1, keepdims=True)
        idx = jnp.min(jnp.where(g == m, lane, width), axis=-1, keepdims=True)
        pick = (lane == idx) & (m > -jnp.inf)
        sel = sel | pick
        g = jnp.where(pick, -jnp.inf, g)
    return sel


def _nt_dot(a, b, precision=None):
    return lax.dot_general(a, b, (((1,), (1,)), ((), ())), precision=precision,
                           preferred_element_type=F32)


def _proj_in_kernel(x_ref, g_ref, w_ref, o_ref, xn_ref):
    j = pl.program_id(1)

    @pl.when(j == 0)
    def _():
        x = x_ref[...]
        ms = jnp.mean(x * x, axis=-1, keepdims=True)
        xn_ref[...] = (x * lax.rsqrt(ms + EPS) * g_ref[...]).astype(BF16)

    o_ref[0] = jnp.dot(xn_ref[...], w_ref[j], preferred_element_type=F32)


def _proj_in(x2d, gain, w3):
    n, d = x2d.shape
    nseg = w3.shape[0]
    tm = min(n, 1024)
    return pl.pallas_call(
        _proj_in_kernel,
        grid=(n // tm, nseg),
        in_specs=[pl.BlockSpec((tm, d), lambda i, j: (i, 0)),
                  pl.BlockSpec((1, d), lambda i, j: (0, 0)),
                  pl.BlockSpec((nseg, d, SEG), lambda i, j: (0, 0, 0))],
        out_specs=pl.BlockSpec((1, tm, SEG), lambda i, j: (j, i, 0)),
        out_shape=jax.ShapeDtypeStruct((nseg, n, SEG), F32),
        scratch_shapes=[pltpu.VMEM((tm, d), BF16)],
        compiler_params=_cparams(2),
        name="proj_in",
    )(x2d, gain.reshape(1, d), w3)


def _proj_out_kernel(a_ref, b_ref, x_ref, w_ref, o_ref):
    acc = jnp.dot(a_ref[...], w_ref[0], preferred_element_type=F32)
    acc = acc + jnp.dot(b_ref[...], w_ref[1], preferred_element_type=F32)
    o_ref[...] = x_ref[...] + acc


def _proj_out(a, b, x2d, w2):
    n, d = x2d.shape
    tm = min(n, 512)
    return pl.pallas_call(
        _proj_out_kernel,
        grid=(n // tm,),
        in_specs=[pl.BlockSpec((tm, SEG), lambda i: (i, 0)),
                  pl.BlockSpec((tm, SEG), lambda i: (i, 0)),
                  pl.BlockSpec((tm, d), lambda i: (i, 0)),
                  pl.BlockSpec((2, SEG, d), lambda i: (0, 0, 0))],
        out_specs=pl.BlockSpec((tm, d), lambda i: (i, 0)),
        out_shape=jax.ShapeDtypeStruct((n, d), F32),
        compiler_params=_cparams(1),
        name="proj_out",
    )(a, b, x2d, w2)


def _top_blocks_t(gate_t, n_valid, own, width):
    blk = lax.broadcasted_iota(jnp.int32, gate_t.shape, 0)
    g = jnp.where(blk < n_valid, gate_t, -jnp.inf)
    sel = blk == own
    for _ in range(MOBA_TOPK):
        m = jnp.max(g, axis=0, keepdims=True)
        idx = jnp.min(jnp.where(g == m, blk, width), axis=0, keepdims=True)
        pick = (blk == idx) & (m > -jnp.inf)
        sel = sel | pick
        g = jnp.where(pick, -jnp.inf, g)
    return sel


def _tn_dot(a, b):
    return lax.dot_general(a, b, (((0,), (0,)), ((), ())), preferred_element_type=F32)


def _online_softmax_step(carry, s, v):
    m, l, acc = carry
    m_new = jnp.maximum(m, jnp.max(s, axis=-1, keepdims=True))
    alpha = jnp.exp(m - m_new)
    pr = jnp.exp(s - m_new)
    l = alpha * l + jnp.sum(pr, axis=-1, keepdims=True)
    acc = alpha * acc + jnp.dot(pr.astype(BF16), v, preferred_element_type=F32)
    return m_new, l, acc


def _softmax_init(rows):
    return (jnp.full((rows, 1), NEG, F32), jnp.zeros((rows, 1), F32), jnp.zeros((rows, PAIR), F32))


def _flash_attention(n_tiles, logits_fn, v_tile_fn, m_ref, l_ref, acc_ref):
    s = logits_fn(0, True)
    reps = s.shape[1] // PAIR
    m = jnp.max(s, axis=-1, keepdims=True)
    pr = jnp.exp(s - m)
    m_ref[...] = jnp.broadcast_to(m, m_ref.shape)
    l_ref[...] = jnp.broadcast_to(jnp.sum(pr, axis=-1, keepdims=True), l_ref.shape)
    acc_ref[...] = jnp.dot(pr.astype(BF16), v_tile_fn(0), preferred_element_type=F32)

    def body(step, carry):
        s = logits_fn(step, False)
        m_prev = m_ref[...]
        m_new = jnp.maximum(m_prev, jnp.max(s, axis=-1, keepdims=True))
        alpha = jnp.exp(m_prev - m_new)
        pr = jnp.exp(s - jnp.concatenate([m_new] * reps, axis=1))
        m_ref[...] = m_new
        l_ref[...] = alpha * l_ref[...] + jnp.sum(pr, axis=-1, keepdims=True)
        acc_ref[...] = alpha * acc_ref[...] + jnp.dot(pr.astype(BF16), v_tile_fn(step),
                                                      preferred_element_type=F32)
        return carry

    lax.fori_loop(1, n_tiles, body, 0)
    return acc_ref[...] / l_ref[...]


def _moba_prompt_kernel(tab_ref, q_ref, k_ref, v_ref, g_ref, qg_ref, kg_ref,
                        o_ref, kt_ref, vt_ref,
                        kn_ref, vb_ref, km_ref, bias_ref, m_ref, l_ref, acc_ref):
    p = pl.program_id(1)
    i = pl.program_id(2)
    t = ATT_TILE
    nblk = k_ref.shape[1] // t
    lo = _lo_mask()

    @pl.when(i == 0)
    def _prep():
        lane = lax.broadcasted_iota(jnp.int32, (t, PAIR), 1)
        kn_ref[0:t, 0:PAIR] = jnp.zeros((t, PAIR), BF16)
        kn_ref[0:t, PAIR:2 * PAIR] = (lane == nblk).astype(BF16)
        vb_ref[0:t, :] = jnp.zeros((t, PAIR), BF16)

        for c in range(nblk):
            r0 = c * t
            kn = _pair_rmsnorm(k_ref[0, r0:r0 + t, :], kg_ref[...], lo)
            v = v_ref[0, r0:r0 + t, :]
            kt_ref[0, :, r0:r0 + t] = kn.T
            vt_ref[0, :, r0:r0 + t] = v.T
            kn_ref[r0 + t:r0 + 2 * t, 0:PAIR] = kn.astype(BF16)
            kn_ref[r0 + t:r0 + 2 * t, PAIR:2 * PAIR] = (lane == c).astype(BF16)
            km_ref[c:c + 1, :] = jnp.sum(kn, axis=0, keepdims=True) * (1.0 / t)
            vb_ref[r0 + t:r0 + 2 * t, :] = v.astype(BF16)
        rel = (lax.broadcasted_iota(jnp.int32, (t, t), 0) - lax.broadcasted_iota(jnp.int32, (t, t), 1))
        for hh in range(2):
            h = 2 * p + hh
            bias_ref[0, hh * t:(hh + 1) * t, 0:t] = _t5_bias(rel + t, tab_ref, h)
            bias_ref[0, hh * t:(hh + 1) * t, t:2 * t] = jnp.where(rel >= 0, _t5_bias(rel, tab_ref, h), NEG)
            bias_ref[1, hh * t:(hh + 1) * t, :] = jnp.full((t, 2 * t), tab_ref[T5_BUCKETS - 1, h], F32)

    qn = _pair_rmsnorm(q_ref[0], qg_ref[...], lo)
    km = km_ref[...]
    km_heads = jnp.concatenate([jnp.where(lo, km, 0.0), jnp.where(lo, 0.0, km),
                                jnp.zeros((PAIR - 2 * nblk, PAIR), F32)], axis=0)
    gate_all = _nt_dot(qn, km_heads, precision=HIGHEST).T
    no_block = jnp.where(lax.broadcasted_iota(jnp.int32, (PAIR - nblk, t), 0) == 0, NEG, 0.0)
    qs, pens = [], []
    for hh in range(2):
        hm = lo if hh == 0 else jnp.logical_not(lo)
        qh = jnp.where(hm, qn, 0.0)
        gate_t = gate_all[hh * nblk:(hh + 1) * nblk]
        sel = _top_blocks_t(gate_t, i, i, nblk)
        pen_t = jnp.concatenate([jnp.where(sel, 0.0, NEG), no_block], axis=0)
        pens.append(pen_t.T)
        qs.append(qh * (HEAD_DIM ** -0.5))
    q2 = jnp.concatenate([jnp.concatenate(qs, axis=0), jnp.concatenate(pens, axis=0)], axis=1).astype(BF16)

    def tile_rows(step):
        return pl.ds(pl.multiple_of((i - 2 * step) * t, t), 2 * t)

    def logits(step, first):
        return _nt_dot(q2, kn_ref[tile_rows(step), :]) + bias_ref[0 if first else 1]

    o = _flash_attention((i + 2) // 2, logits, lambda step: vb_ref[tile_rows(step), :], m_ref, l_ref, acc_ref)
    o = jnp.where(lo, o[0:t], o[t:2 * t])
    o_ref[...] = (o * _silu(g_ref[0])).astype(BF16)


def _heads_from_token_minor(xt, batch, seq):
    return xt.reshape(batch, N_HEADS, HEAD_DIM, seq).transpose(0, 3, 1, 2)


def _moba_prompt(z3, batch, seq, t5_table, q_gain, k_gain):
    n = batch * seq
    t = ATT_TILE
    nq = seq // t
    npair = SEG // PAIR
    gain2 = lambda g: jnp.tile(g.astype(F32), 2).reshape(1, PAIR)
    return pl.pallas_call(
        _moba_prompt_kernel,
        grid=(batch, npair, nq),
        in_specs=[pl.BlockSpec(memory_space=pltpu.SMEM),
                  pl.BlockSpec((1, t, PAIR), lambda b, p, i: (0, b * nq + i, p)),
                  pl.BlockSpec((1, seq, PAIR), lambda b, p, i: (1, b, p)),
                  pl.BlockSpec((1, seq, PAIR), lambda b, p, i: (2, b, p)),
                  pl.BlockSpec((1, t, PAIR), lambda b, p, i: (3, b * nq + i, p)),
                  pl.BlockSpec((1, PAIR), lambda b, p, i: (0, 0)),
                  pl.BlockSpec((1, PAIR), lambda b, p, i: (0, 0))],
        out_specs=[pl.BlockSpec((t, PAIR), lambda b, p, i: (b * nq + i, p)),
                   pl.BlockSpec((1, PAIR, seq), lambda b, p, i: (b, p, 0)),
                   pl.BlockSpec((1, PAIR, seq), lambda b, p, i: (b, p, 0))],
        out_shape=[jax.ShapeDtypeStruct((n, SEG), BF16),
                   jax.ShapeDtypeStruct((batch, SEG, seq), F32),
                   jax.ShapeDtypeStruct((batch, SEG, seq), F32)],
        scratch_shapes=[pltpu.VMEM((seq + t, 2 * PAIR), BF16),
                        pltpu.VMEM((seq + t, PAIR), BF16),
                        pltpu.VMEM((seq // t, PAIR), F32),
                        pltpu.VMEM((2, 2 * t, 2 * t), F32),
                        pltpu.VMEM((2 * t, PAIR), F32),
                        pltpu.VMEM((2 * t, PAIR), F32),
                        pltpu.VMEM((2 * t, PAIR), F32)],
        compiler_params=_cparams(3),
        name="moba_prompt",
    )(t5_table.astype(F32), z3, z3, z3, z3, gain2(q_gain), gain2(k_gain))


def _rope_pair(x, cos, sin_signed):
    up = pltpu.roll(x, PAIR - HEAD_DIM // 2, 1)
    dn = pltpu.roll(x, HEAD_DIM // 2, 1)
    lane = lax.broadcasted_iota(jnp.int32, (1, PAIR), 1)
    first_half = (lane % HEAD_DIM) < (HEAD_DIM // 2)
    return x * cos + jnp.where(first_half, up, dn) * sin_signed


def _retention_kernel(lg_ref, q_ref, k_ref, v_ref, g_ref, cos_ref, sin_ref, gain_ref, s0_ref,
                      o_ref, sout_ref, s_ref, *, groups, mm_dtype):
    p = pl.program_id(1)
    c = pl.program_id(2)
    n = q_ref.shape[0]
    glen = n // groups
    lo = _lo_mask()

    @pl.when(c == 0)
    def _():
        s_ref[...] = s0_ref[...]

    cos = cos_ref[...]
    sin = sin_ref[...]
    qr = _rope_pair(q_ref[...], cos, sin)
    kr = _rope_pair(k_ref[...], cos, sin) * (HEAD_DIM ** -0.5)
    vb = v_ref[...].astype(mm_dtype)
    lg0 = lg_ref[2 * p]
    lg1 = lg_ref[2 * p + 1]
    lg_lane = jnp.where(lo, lg0, lg1)
    row = lax.broadcasted_iota(jnp.int32, (n, 1), 0)
    pos = (row % glen).astype(F32)
    grp = row // glen
    q_dec = jnp.exp(lg_lane * (pos + 1.0))
    k_dec = jnp.exp(lg_lane * (float(glen - 1) - pos))
    qd = (qr * q_dec).astype(mm_dtype)
    o = jnp.zeros((n, PAIR), F32)
    for g in range(groups):
        og = jnp.dot(qd, s_ref[g].astype(mm_dtype), preferred_element_type=F32)
        o = o + (og if groups == 1 else jnp.where(grp == g, og, 0.0))
    ri = lax.broadcasted_iota(jnp.int32, (n, n), 0)
    ci = lax.broadcasted_iota(jnp.int32, (n, n), 1)
    causal = (ri >= ci) & ((ri // glen) == (ci // glen))
    dpos = jnp.where(causal, ri - ci, 0).astype(F32)
    krb = kr.astype(mm_dtype)
    for hh in range(2):
        hm = lo if hh == 0 else jnp.logical_not(lo)
        lgh = lg0 if hh == 0 else lg1
        intra = jnp.where(causal, jnp.exp(lgh * dpos), 0.0)
        a = _nt_dot(jnp.where(hm, qr, 0.0).astype(mm_dtype), krb) * intra
        oh = jnp.dot(a.astype(mm_dtype), vb, preferred_element_type=F32)
        o = o + jnp.where(hm, oh, 0.0)
    srow = lax.broadcasted_iota(jnp.int32, (PAIR, 1), 0) < HEAD_DIM
    c_dec = jnp.exp(jnp.where(srow, lg0, lg1) * float(glen))
    same_head = srow == lo
    kd = kr * k_dec
    for g in range(groups):
        kg = kd if groups == 1 else jnp.where(grp == g, kd, 0.0)
        kv = lax.dot_general(kg.astype(mm_dtype), vb, (((0,), (0,)), ((), ())),
                             preferred_element_type=F32)
        new_state = jnp.where(same_head, s_ref[g] * c_dec + kv, 0.0)
        s_ref[g] = new_state
        sout_ref[g] = new_state

    mu = _pair_sum(o, lo) * (1.0 / HEAD_DIM)
    oc = o - mu
    var = _pair_sum(oc * oc, lo) * (1.0 / HEAD_DIM)
    on = oc * lax.rsqrt(var + EPS) * gain_ref[...]
    o_ref[...] = (on * _silu(g_ref[...])).astype(BF16)


def _pair_blockdiag(s):
    b = s.shape[0]
    s = s.astype(F32).reshape(b, N_HEADS // 2, 2, HEAD_DIM, HEAD_DIM)
    z = jnp.zeros_like(s[:, :, 0])
    top = jnp.concatenate([s[:, :, 0], z], axis=-1)
    bot = jnp.concatenate([z, s[:, :, 1]], axis=-1)
    return jnp.concatenate([top, bot], axis=-2)


def _pair_unblockdiag(sp):
    b = sp.shape[0]
    s0 = sp[:, :, :HEAD_DIM, :HEAD_DIM]
    s1 = sp[:, :, HEAD_DIM:, HEAD_DIM:]
    return jnp.stack([s0, s1], axis=2).reshape(b, N_HEADS, HEAD_DIM, HEAD_DIM)


def _rope_tables(pos):
    half = HEAD_DIM // 2
    inv = ROPE_BASE ** (-jnp.arange(half, dtype=F32) / half)
    ang = pos.astype(F32)[:, None] * inv[None, :]
    cos, sin = jnp.cos(ang), jnp.sin(ang)
    cos_t = jnp.tile(cos, (1, PAIR // half))
    sin_t = jnp.tile(jnp.concatenate([-sin, sin], axis=-1), (1, 2))
    return cos_t, sin_t


def _log_gamma():
    return jnp.log1p(-jnp.exp2(-5.0 - jnp.arange(N_HEADS, dtype=F32)))


def _retention(z3, n_seq, seq, rows, groups, pos0, ret_gain, state_pairs, mm_dtype, name):
    n = n_seq * seq
    nc = (seq * groups) // rows
    nb = n_seq // groups
    npair = SEG // PAIR
    cos_t, sin_t = _rope_tables(pos0 + jnp.arange(seq, dtype=jnp.int32))
    if groups > 1:
        cos_t, sin_t = jnp.tile(cos_t, (groups, 1)), jnp.tile(sin_t, (groups, 1))
    z4 = z3.reshape(z3.shape[0], n // rows, rows, SEG)
    tok = lambda s: pl.BlockSpec((None, None, rows, PAIR), lambda b, p, c, s=s: (s, b * nc + c, 0, p))
    o, sp = pl.pallas_call(
        functools.partial(_retention_kernel, groups=groups, mm_dtype=mm_dtype),
        grid=(nb, npair, nc),
        in_specs=[pl.BlockSpec(memory_space=pltpu.SMEM),
                  tok(4), tok(5), tok(6), tok(7),
                  pl.BlockSpec((rows, PAIR), lambda b, p, c: (c, 0)),
                  pl.BlockSpec((rows, PAIR), lambda b, p, c: (c, 0)),
                  pl.BlockSpec((1, PAIR), lambda b, p, c: (0, p)),
                  pl.BlockSpec((groups, None, PAIR, PAIR), lambda b, p, c: (b, p, 0, 0))],
        out_specs=[pl.BlockSpec((None, rows, PAIR), lambda b, p, c: (b * nc + c, 0, p)),
                   pl.BlockSpec((groups, None, PAIR, PAIR), lambda b, p, c: (b, p, 0, 0))],
        out_shape=[jax.ShapeDtypeStruct((n // rows, rows, SEG), BF16),
                   jax.ShapeDtypeStruct((n_seq, npair, PAIR, PAIR), F32)],
        scratch_shapes=[pltpu.VMEM((groups, PAIR, PAIR), F32)],
        compiler_params=_cparams(3),
        name=name,
    )(_log_gamma(), z4, z4, z4, z4, cos_t, sin_t, ret_gain.astype(F32).reshape(1, SEG), state_pairs)
    return o.reshape(n, SEG), sp


def _even_weights(w_in, w_out):
    d = w_in.shape[0]
    w3 = w_in.reshape(d, 8, SEG).transpose(1, 0, 2).astype(BF16)
    w2 = w_out.reshape(2, SEG, d).astype(BF16)
    return w3, w2


def _even_layer_prompt(x, norm_g, w_in, q_gain, k_gain, t5_table, ret_gain, w_out):
    b, l, d = x.shape
    x2d = x.reshape(b * l, d)
    w3, w2 = _even_weights(w_in, w_out)
    z3 = _proj_in(x2d, norm_g, w3)
    oa, kt, vt = _moba_prompt(z3, b, l, t5_table, q_gain, k_gain)
    zero_state = jnp.zeros((b, SEG // PAIR, PAIR, PAIR), F32)
    orr, sp = _retention(z3, b, l, min(l, 256), 1, 0, ret_gain, zero_state, BF16, "retention_prompt")
    y = _proj_out(oa, orr, x2d, w2)
    return (y.reshape(b, l, d), _heads_from_token_minor(kt, b, l), _heads_from_token_minor(vt, b, l),
            _pair_unblockdiag(sp))


def _log_sigmoid(x):
    y = -x
    return -(jnp.maximum(y, 0.0) + jnp.log1p(jnp.exp(-jnp.abs(y))))


def _logf_kernel(fz_ref, b_ref, lf_ref, fc_ref):
    seq = fz_ref.shape[1]
    t = ATT_TILE
    tri = (lax.broadcasted_iota(jnp.int32, (t, t), 0) >= lax.broadcasted_iota(jnp.int32, (t, t), 1)).astype(F32)
    carry = jnp.zeros((1, PAIR), F32)
    for c in range(seq // t):
        lf = _log_sigmoid(fz_ref[0, c * t:(c + 1) * t, :] + b_ref[...])
        cs = jnp.dot(tri, lf, precision=HIGHEST, preferred_element_type=F32) + carry
        lf_ref[0, c * t:(c + 1) * t, :] = lf[:, :N_HEADS]
        fc_ref[0, c * t:(c + 1) * t, :] = cs[:, :N_HEADS]
        carry = cs[t - 1:t, :]


def _logf_prompt(z3, batch, seq, b_forget):
    bpad = jnp.zeros((1, PAIR), F32).at[0, :N_HEADS].set(b_forget.astype(F32))
    return pl.pallas_call(
        _logf_kernel,
        grid=(batch,),
        in_specs=[pl.BlockSpec((1, seq, PAIR), lambda b: (7, b, 0)),
                  pl.BlockSpec((1, PAIR), lambda b: (0, 0))],
        out_specs=[pl.BlockSpec((1, seq, N_HEADS), lambda b: (b, 0, 0)),
                   pl.BlockSpec((1, seq, N_HEADS), lambda b: (b, 0, 0))],
        out_shape=[jax.ShapeDtypeStruct((batch, seq, N_HEADS), F32),
                   jax.ShapeDtypeStruct((batch, seq, N_HEADS), F32)],
        compiler_params=_cparams(1),
        name="logf_prompt",
    )(z3, bpad)


def _lane_split3(x):
    hi = x.astype(BF16).astype(F32)
    rest = x - hi
    mid = rest.astype(BF16).astype(F32)
    r = lax.broadcasted_iota(jnp.int32, x.shape, 1) % 3
    return jnp.where(r == 0, hi, jnp.where(r == 1, mid, rest - mid))


def _fox_prompt_kernel(q_ref, k_ref, v_ref, g_ref, fq_ref, fk_ref, qg_ref, kg_ref,
                       o_ref, kt_ref, vt_ref,
                       kn_ref, vb_ref, mask_ref, m_ref, l_ref, acc_ref):
    p = pl.program_id(1)
    i = pl.program_id(2)
    t = ATT_TILE
    nblk = k_ref.shape[1] // t
    lo = _lo_mask()
    lane = lax.broadcasted_iota(jnp.int32, (t, PAIR), 1)
    prow = lax.broadcasted_iota(jnp.int32, (N_HEADS, PAIR), 0)
    plane = lax.broadcasted_iota(jnp.int32, (N_HEADS, PAIR), 1)

    @pl.when(i == 0)
    def _prep():
        kn_ref[0:t, 0:PAIR] = jnp.zeros((t, PAIR), BF16)
        kn_ref[0:t, PAIR:2 * PAIR] = jnp.where((lane == 0) | (lane == 3), NEG, 0.0).astype(BF16)
        vb_ref[0:t, :] = jnp.zeros((t, PAIR), BF16)
        place = jnp.where(((prow == 2 * p) & (plane < 3)) | ((prow == 2 * p + 1) & (plane >= 3) & (plane < 6)),
                          -1.0, 0.0)

        for c in range(nblk):
            r0 = c * t
            kn = _pair_rmsnorm(k_ref[0, r0:r0 + t, :], kg_ref[...], lo)
            v = v_ref[0, r0:r0 + t, :]
            kt_ref[0, :, r0:r0 + t] = kn.T
            vt_ref[0, :, r0:r0 + t] = v.T
            kn_ref[r0 + t:r0 + 2 * t, 0:PAIR] = kn.astype(BF16)
            neg_fk = jnp.dot(fk_ref[0, r0:r0 + t, :], place, precision=HIGHEST, preferred_element_type=F32)
            up = jnp.where((lane >= 6) & (lane < 9), 1.0, _lane_split3(neg_fk))
            kn_ref[r0 + t:r0 + 2 * t, PAIR:2 * PAIR] = up.astype(BF16)
            vb_ref[r0 + t:r0 + 2 * t, :] = v.astype(BF16)
        row = lax.broadcasted_iota(jnp.int32, (2 * t, 2 * t), 0) % t
        col = lax.broadcasted_iota(jnp.int32, (2 * t, 2 * t), 1) - t
        mask_ref[...] = jnp.where(row >= col, 0.0, NEG)

    qn = _pair_rmsnorm(q_ref[0], qg_ref[...], lo)
    qs, ups = [], []
    for hh in range(2):
        hm = lo if hh == 0 else jnp.logical_not(lo)
        qs.append(jnp.where(hm, qn, 0.0) * (HEAD_DIM ** -0.5))
        place_q = jnp.where((prow == 2 * p + hh) & (plane >= 6) & (plane < 9), 1.0, 0.0)
        fq = jnp.dot(fq_ref[0], place_q, precision=HIGHEST, preferred_element_type=F32)
        ups.append(jnp.where((lane >= 3 * hh) & (lane < 3 * hh + 3), 1.0, _lane_split3(fq)))
    q2 = jnp.concatenate([jnp.concatenate(qs, axis=0), jnp.concatenate(ups, axis=0)], axis=1).astype(BF16)

    def tile_rows(step):
        return pl.ds(pl.multiple_of((i - 2 * step) * t, t), 2 * t)

    def logits(step, first):
        s = _nt_dot(q2, kn_ref[tile_rows(step), :])
        return s + mask_ref[...] if first else s

    o = _flash_attention((i + 2) // 2, logits, lambda step: vb_ref[tile_rows(step), :], m_ref, l_ref, acc_ref)
    o = jnp.where(lo, o[0:t], o[t:2 * t])
    o_ref[...] = (o * _silu(g_ref[0])).astype(BF16)


def _fox_prompt(z3, batch, seq, fcum, q_gain, k_gain):
    n = batch * seq
    t = ATT_TILE
    nq = seq // t
    npair = SEG // PAIR
    gain2 = lambda g: jnp.tile(g.astype(F32), 2).reshape(1, PAIR)
    return pl.pallas_call(
        _fox_prompt_kernel,
        grid=(batch, npair, nq),
        in_specs=[pl.BlockSpec((1, t, PAIR), lambda b, p, i: (0, b * nq + i, p)),
                  pl.BlockSpec((1, seq, PAIR), lambda b, p, i: (1, b, p)),
                  pl.BlockSpec((1, seq, PAIR), lambda b, p, i: (2, b, p)),
                  pl.BlockSpec((1, t, PAIR), lambda b, p, i: (3, b * nq + i, p)),
                  pl.BlockSpec((1, t, N_HEADS), lambda b, p, i: (b, i, 0)),
                  pl.BlockSpec((1, seq, N_HEADS), lambda b, p, i: (b, 0, 0)),
                  pl.BlockSpec((1, PAIR), lambda b, p, i: (0, 0)),
                  pl.BlockSpec((1, PAIR), lambda b, p, i: (0, 0))],
        out_specs=[pl.BlockSpec((t, PAIR), lambda b, p, i: (b * nq + i, p)),
                   pl.BlockSpec((1, PAIR, seq), lambda b, p, i: (b, p, 0)),
                   pl.BlockSpec((1, PAIR, seq), lambda b, p, i: (b, p, 0))],
        out_shape=[jax.ShapeDtypeStruct((n, SEG), BF16),
                   jax.ShapeDtypeStruct((batch, SEG, seq), F32),
                   jax.ShapeDtypeStruct((batch, SEG, seq), F32)],
        scratch_shapes=[pltpu.VMEM((seq + t, 2 * PAIR), BF16),
                        pltpu.VMEM((seq + t, PAIR), BF16),
                        pltpu.VMEM((2 * t, 2 * t), F32),
                        pltpu.VMEM((2 * t, PAIR), F32),
                        pltpu.VMEM((2 * t, PAIR), F32),
                        pltpu.VMEM((2 * t, PAIR), F32)],
        compiler_params=_cparams(3),
        name="fox_prompt",
    )(z3, z3, z3, z3, fcum, fcum, gain2(q_gain), gain2(k_gain))


CONV_HALO = 32
CONV_ROWS = 32


def _conv_prompt_kernel(ua_ref, ub_ref, gd_ref, uah_ref, ubh_ref, st_ref, w_ref, cb_ref, cg_ref,
                        d_ref, cs_ref, buf_ref, y_ref):
    ti = pl.program_id(1)
    nt = pl.num_programs(1)
    t = ua_ref.shape[1]
    pad = CONV_HALO - (CONV_WIDTH - 1)
    buf_ref[CONV_HALO:CONV_HALO + t, :] = ua_ref[0] * jax.nn.sigmoid(ub_ref[0])

    @pl.when(ti == 0)
    def _():
        buf_ref[0:CONV_HALO, :] = st_ref[0]

    @pl.when(ti > 0)
    def _():
        buf_ref[0:CONV_HALO, :] = uah_ref[0] * jax.nn.sigmoid(ubh_ref[0])

    for lg in range(SEG // PAIR):
        ls = slice(lg * PAIR, (lg + 1) * PAIR)
        for rc in range(t // CONV_ROWS):
            r0 = rc * CONV_ROWS
            acc = jnp.zeros((CONV_ROWS, PAIR), F32) + cb_ref[:, ls]
            for k in range(CONV_WIDTH):
                acc = acc + w_ref[k:k + 1, ls] * buf_ref[r0 + pad + k:r0 + pad + k + CONV_ROWS, ls]
            y_ref[r0:r0 + CONV_ROWS, ls] = acc

    y = y_ref[...]
    mu = jnp.mean(y, axis=-1, keepdims=True)
    yc = y - mu
    var = jnp.mean(yc * yc, axis=-1, keepdims=True)
    yn = yc * lax.rsqrt(var + EPS) * cg_ref[...]
    d_ref[...] = (_silu(yn) * _silu(gd_ref[0])).astype(BF16)

    @pl.when(ti == nt - 1)
    def _():
        cs_ref[0] = buf_ref[t + pad:t + CONV_HALO, :]


def _conv_prompt(z3, batch, seq, state, conv_w, conv_b, conv_gain):
    n = batch * seq
    t = min(seq, 256)
    nt = seq // t
    hb = t // CONV_HALO
    st = jnp.pad(state.astype(F32), ((0, 0), (CONV_HALO - (CONV_WIDTH - 1), 0), (0, 0)))
    tok = lambda s: pl.BlockSpec((1, t, SEG), lambda b, i, s=s: (s, b * nt + i, 0))
    halo = lambda s: pl.BlockSpec((1, CONV_HALO, SEG),
                                  lambda b, i, s=s: (s, jnp.maximum((b * nt + i) * hb - 1, 0), 0))
    row = pl.BlockSpec((1, SEG), lambda b, i: (0, 0))
    return pl.pallas_call(
        _conv_prompt_kernel,
        grid=(batch, nt),
        in_specs=[tok(4), tok(5), tok(6), halo(4), halo(5),
                  pl.BlockSpec((1, CONV_HALO, SEG), lambda b, i: (b, 0, 0)),
                  pl.BlockSpec((CONV_WIDTH, SEG), lambda b, i: (0, 0)), row, row],
        out_specs=[pl.BlockSpec((t, SEG), lambda b, i: (b * nt + i, 0)),
                   pl.BlockSpec((1, CONV_WIDTH - 1, SEG), lambda b, i: (b, 0, 0))],
        out_shape=[jax.ShapeDtypeStruct((n, SEG), BF16),
                   jax.ShapeDtypeStruct((batch, CONV_WIDTH - 1, SEG), F32)],
        scratch_shapes=[pltpu.VMEM((t + CONV_HALO, SEG), F32),
                        pltpu.VMEM((t, SEG), F32)],
        compiler_params=_cparams(2),
        name="conv_prompt",
    )(z3, z3, z3, z3, z3, st, conv_w.astype(F32), conv_b.astype(F32).reshape(1, SEG),
      conv_gain.astype(F32).reshape(1, SEG))


def _odd_weights(w_in, w_out):
    d = w_in.shape[0]
    w = SEG
    fz = w_in[:, 4 * w:4 * w + N_HEADS]
    rest = jnp.concatenate([w_in[:, :4 * w], w_in[:, 4 * w + N_HEADS:]], axis=1)
    fz_pad = jnp.pad(fz, ((0, 0), (0, w - N_HEADS)))
    w3 = jnp.concatenate([rest, fz_pad], axis=1).reshape(d, 8, w).transpose(1, 0, 2).astype(BF16)
    w2 = w_out.reshape(2, w, d).astype(BF16)
    return w3, w2


def _odd_layer_prompt(x, norm_g, w_in, b_forget, q_gain, k_gain, conv_w, conv_b, conv_gain, w_out):
    b, l, d = x.shape
    x2d = x.reshape(b * l, d)
    w3, w2 = _odd_weights(w_in, w_out)
    z3 = _proj_in(x2d, norm_g, w3)
    logf, fcum = _logf_prompt(z3, b, l, b_forget)
    oc, kt, vt = _fox_prompt(z3, b, l, fcum, q_gain, k_gain)
    dd, cs = _conv_prompt(z3, b, l, jnp.zeros((b, CONV_WIDTH - 1, SEG), F32), conv_w, conv_b, conv_gain)
    y = _proj_out(oc, dd, x2d, w2)
    return (y.reshape(b, l, d), _heads_from_token_minor(kt, b, l), _heads_from_token_minor(vt, b, l), logf, cs)


def _heads_rmsnorm(x, gain):
    lo = _lo_mask()
    parts = [_pair_rmsnorm(x[:, g * PAIR:(g + 1) * PAIR], gain[:, g * PAIR:(g + 1) * PAIR], lo)
             for g in range(SEG // PAIR)]
    return jnp.concatenate(parts, axis=-1)


def _query_rows(qn, n_tok):
    sub = lax.broadcasted_iota(jnp.int32, (N_HEADS, SEG), 0)
    head_of_lane = lax.broadcasted_iota(jnp.int32, (N_HEADS, SEG), 1) // HEAD_DIM
    own = sub == head_of_lane
    return jnp.concatenate([jnp.where(own, jnp.broadcast_to(qn[q:q + 1, :], (N_HEADS, SEG)), 0.0)
                            for q in range(n_tok)], axis=0)


def _rows_to_tokens(rows_out, n_tok):
    sub = lax.broadcasted_iota(jnp.int32, (N_HEADS, SEG), 0)
    head_of_lane = lax.broadcasted_iota(jnp.int32, (N_HEADS, SEG), 1) // HEAD_DIM
    own = sub == head_of_lane
    tok = lax.broadcasted_iota(jnp.int32, (n_tok, SEG), 0)
    out = jnp.zeros((n_tok, SEG), F32)
    for q in range(n_tok):
        o_q = jnp.sum(jnp.where(own, rows_out[q * N_HEADS:(q + 1) * N_HEADS, :], 0.0), axis=0, keepdims=True)
        out = jnp.where(tok == q, o_q, out)
    return out


def _t5_bias_rows(rel, tab):
    bias = jnp.broadcast_to(tab[:, T5_BUCKETS - 1:T5_BUCKETS], rel.shape)
    for b in range(T5_BUCKETS - 2, -1, -1):
        bias = jnp.where(rel <= T5_UPPER[b], tab[:, b:b + 1], bias)
    return bias


def _finish_rows(s_past, s_new, v_pages, v_new, n_tok):
    m = jnp.max(s_past, axis=-1, keepdims=True)
    for col in s_new:
        m = jnp.maximum(m, col)
    p_past = jnp.exp(s_past - m)
    l = jnp.sum(p_past, axis=-1, keepdims=True)
    pb = p_past.astype(BF16)
    acc = jnp.zeros((s_past.shape[0], SEG), F32)
    for pg, v_ref in enumerate(v_pages):
        acc = acc + _nt_dot(pb[:, pg * PAGE_SIZE:(pg + 1) * PAGE_SIZE], v_ref[0].astype(BF16))
    for kj, col in enumerate(s_new):
        p_new = jnp.exp(col - m)
        l = l + p_new
        acc = acc + p_new * v_new[kj:kj + 1, :]
    return _rows_to_tokens(acc / l, n_tok)


def _moba_sample_kernel(pt_ref, tab_ref, q_ref, k_ref, v_ref, g_ref, qg_ref, kg_ref, *rest, n_pages, n_tok):
    k_pages = rest[:n_pages]
    v_pages = rest[n_pages:2 * n_pages]
    o_ref, ka_ref = rest[2 * n_pages:]
    pages_per_block = MOBA_BLOCK // PAGE_SIZE
    nb_past = n_pages // pages_per_block
    past_len = n_pages * PAGE_SIZE
    n_rows = n_tok * N_HEADS

    qn = _heads_rmsnorm(q_ref[0, 0], qg_ref[...])
    kn = _heads_rmsnorm(k_ref[0, 0], kg_ref[...])
    ka_ref[0] = kn
    v_new = v_ref[0, 0]
    qrows = _query_rows(qn, n_tok)
    qb = (qrows * (HEAD_DIM ** -0.5)).astype(BF16)
    tab = tab_ref[...]
    row_q = lax.broadcasted_iota(jnp.int32, (n_rows, 1), 0) // N_HEADS

    raw = [jnp.dot(qb, k_pages[pg][0].astype(BF16), preferred_element_type=F32) for pg in range(n_pages)]
    blk_lane = lax.broadcasted_iota(jnp.int32, (SEG, nb_past), 1)
    k_mean = jnp.zeros((SEG, nb_past), F32)
    for n in range(nb_past):
        tot = k_pages[n * pages_per_block][0]
        for pg in range(n * pages_per_block + 1, (n + 1) * pages_per_block):
            tot = tot + k_pages[pg][0]
        k_mean = jnp.where(blk_lane == n, jnp.sum(tot, axis=-1, keepdims=True) * (1.0 / MOBA_BLOCK), k_mean)
    gate = jnp.dot(qrows, k_mean, precision=HIGHEST, preferred_element_type=F32)
    selpen = jnp.where(_top_blocks(gate, nb_past, nb_past, nb_past), 0.0, NEG)

    far = tab[:, T5_BUCKETS - 1:T5_BUCKETS]
    lane = lax.broadcasted_iota(jnp.int32, (n_rows, PAGE_SIZE), 1)
    tiles = []
    for pg in range(n_pages):
        n = pg // pages_per_block
        min_rel = past_len - (pg + 1) * PAGE_SIZE + 1
        if min_rel > T5_UPPER[-1]:
            bias = far
        else:
            bias = _t5_bias_rows(past_len + row_q - (pg * PAGE_SIZE + lane), tab)
        tiles.append(raw[pg] + bias + selpen[:, n:n + 1])
    s_past = jnp.concatenate(tiles, axis=-1)

    s_new = []
    for kj in range(n_tok):
        dot = jnp.sum(qrows * kn[kj:kj + 1, :], axis=-1, keepdims=True) * (HEAD_DIM ** -0.5)
        rel = row_q - kj
        s_new.append(jnp.where(rel >= 0, dot + _t5_bias_rows(rel, tab), NEG))

    o = _finish_rows(s_past, s_new, v_pages, v_new, n_tok)
    o_ref[0] = (o * _silu(g_ref[0, 0])).astype(BF16)


def _page_specs(n_pages, rows):
    return [pl.BlockSpec((1, rows, PAGE_SIZE), lambda b, pt, pg=pg: (pt[b, pg], 0, 0)) for pg in range(n_pages)]


def _pages_token_minor(cache):
    n_pool = cache.shape[0]
    return cache.transpose(0, 2, 3, 1).reshape(n_pool, SEG, PAGE_SIZE)


def _moba_sample(z3, n_seq, n_tok, cache_k, cache_v, page_table, t5_table, q_gain, k_gain):
    n_pages = page_table.shape[1]
    n_pool = cache_k.shape[0]
    assert MOBA_BLOCK % PAGE_SIZE == 0 and (n_pages * PAGE_SIZE) % MOBA_BLOCK == 0
    assert n_tok <= MOBA_BLOCK and n_pages // (MOBA_BLOCK // PAGE_SIZE) >= MOBA_TOPK
    z4 = z3.reshape(z3.shape[0], n_seq, n_tok, SEG)
    tok = lambda s: pl.BlockSpec((1, 1, n_tok, SEG), lambda b, pt, s=s: (s, b, 0, 0))
    row = pl.BlockSpec((1, SEG), lambda b, pt: (0, 0))
    gain8 = lambda g: jnp.tile(g.astype(F32), N_HEADS).reshape(1, SEG)
    tab_rows = jnp.tile(t5_table.astype(F32).T, (n_tok, 1))
    grid_spec = pltpu.PrefetchScalarGridSpec(
        num_scalar_prefetch=1,
        grid=(n_seq,),
        in_specs=[pl.BlockSpec((n_tok * N_HEADS, T5_BUCKETS), lambda b, pt: (0, 0)),
                  tok(0), tok(1), tok(2), tok(3), row, row]
                 + _page_specs(n_pages, SEG) + _page_specs(n_pages, SEG),
        out_specs=[pl.BlockSpec((1, n_tok, SEG), lambda b, pt: (b, 0, 0)),
                   pl.BlockSpec((1, n_tok, SEG), lambda b, pt: (b, 0, 0))])
    ck = _pages_token_minor(cache_k)
    cv = _pages_token_minor(cache_v)
    return pl.pallas_call(
        functools.partial(_moba_sample_kernel, n_pages=n_pages, n_tok=n_tok),
        grid_spec=grid_spec,
        out_shape=[jax.ShapeDtypeStruct((n_seq, n_tok, SEG), BF16),
                   jax.ShapeDtypeStruct((n_seq, n_tok, SEG), F32)],
        compiler_params=_cparams(1),
        name="moba_sample",
    )(page_table, tab_rows, z4, z4, z4, z4, gain8(q_gain), gain8(k_gain),
      *([ck] * n_pages), *([cv] * n_pages))


def _fox_sample_kernel(pt_ref, q_ref, k_ref, v_ref, g_ref, fz_ref, fzr_ref, bf_ref, bfr_ref, qg_ref, kg_ref,
                       *rest, n_pages, n_tok):
    k_pages = rest[:n_pages]
    v_pages = rest[n_pages:2 * n_pages]
    f_pages = rest[2 * n_pages:3 * n_pages]
    o_ref, kc_ref, lf_ref = rest[3 * n_pages:]
    n_rows = n_tok * N_HEADS

    qn = _heads_rmsnorm(q_ref[0, 0], qg_ref[...])
    kn = _heads_rmsnorm(k_ref[0, 0], kg_ref[...])
    kc_ref[0] = kn
    v_new = v_ref[0, 0]
    lf_ref[0] = _log_sigmoid(fz_ref[0, 0][:, :N_HEADS] + bf_ref[...])
    qrows = _query_rows(qn, n_tok) * (HEAD_DIM ** -0.5)
    qb = qrows.astype(BF16)
    row_q = lax.broadcasted_iota(jnp.int32, (n_rows, 1), 0) // N_HEADS

    upper = (lax.broadcasted_iota(jnp.int32, (PAGE_SIZE, PAGE_SIZE), 0)
             <= lax.broadcasted_iota(jnp.int32, (PAGE_SIZE, PAGE_SIZE), 1)).astype(F32)
    carry = jnp.zeros((n_rows, 1), F32)
    fk_tiles = []
    for pg in range(n_pages):
        rows_lf = jnp.concatenate([f_pages[pg][0]] * n_tok, axis=0)
        cs = jnp.dot(rows_lf, upper, precision=HIGHEST, preferred_element_type=F32) + carry
        fk_tiles.append(cs)
        carry = cs[:, PAGE_SIZE - 1:PAGE_SIZE]
    lfr = _log_sigmoid(fzr_ref[0] + bfr_ref[...])
    cum_new = []
    run = carry
    for tk in range(n_tok):
        run = run + lfr[:, tk:tk + 1]
        cum_new.append(run)
    fq = cum_new[n_tok - 1]
    for tk in range(n_tok - 2, -1, -1):
        fq = jnp.where(row_q == tk, cum_new[tk], fq)

    tiles = [jnp.dot(qb, k_pages[pg][0].astype(BF16), preferred_element_type=F32) + (fq - fk_tiles[pg])
             for pg in range(n_pages)]
    s_past = jnp.concatenate(tiles, axis=-1)
    s_new = []
    for kj in range(n_tok):
        dot = jnp.sum(qrows * kn[kj:kj + 1, :], axis=-1, keepdims=True)
        s_new.append(jnp.where(row_q >= kj, dot + (fq - cum_new[kj]), NEG))

    o = _finish_rows(s_past, s_new, v_pages, v_new, n_tok)
    o_ref[0] = (o * _silu(g_ref[0, 0])).astype(BF16)


def _fox_sample(z3, n_seq, n_tok, cache_k, cache_v, cache_logf, page_table, b_forget, q_gain, k_gain):
    n_pages = page_table.shape[1]
    n_pool = cache_k.shape[0]
    z4 = z3.reshape(z3.shape[0], n_seq, n_tok, SEG)
    tok = lambda s: pl.BlockSpec((1, 1, n_tok, SEG), lambda b, pt, s=s: (s, b, 0, 0))
    row = pl.BlockSpec((1, SEG), lambda b, pt: (0, 0))
    gain8 = lambda g: jnp.tile(g.astype(F32), N_HEADS).reshape(1, SEG)
    fz = z4[7, :, :, :N_HEADS]
    fz_rows = jnp.tile(fz.transpose(0, 2, 1), (1, n_tok, 1))
    bf = b_forget.astype(F32)
    grid_spec = pltpu.PrefetchScalarGridSpec(
        num_scalar_prefetch=1,
        grid=(n_seq,),
        in_specs=[tok(0), tok(1), tok(2), tok(3),
                  pl.BlockSpec((1, 1, n_tok, PAIR), lambda b, pt: (7, b, 0, 0)),
                  pl.BlockSpec((1, n_tok * N_HEADS, n_tok), lambda b, pt: (b, 0, 0)),
                  pl.BlockSpec((1, N_HEADS), lambda b, pt: (0, 0)),
                  pl.BlockSpec((n_tok * N_HEADS, 1), lambda b, pt: (0, 0)),
                  row, row]
                 + _page_specs(n_pages, SEG) + _page_specs(n_pages, SEG) + _page_specs(n_pages, N_HEADS),
        out_specs=[pl.BlockSpec((1, n_tok, SEG), lambda b, pt: (b, 0, 0)),
                   pl.BlockSpec((1, n_tok, SEG), lambda b, pt: (b, 0, 0)),
                   pl.BlockSpec((1, n_tok, N_HEADS), lambda b, pt: (b, 0, 0))])
    ck = _pages_token_minor(cache_k)
    cv = _pages_token_minor(cache_v)
    cf = cache_logf.astype(F32).transpose(0, 2, 1)
    return pl.pallas_call(
        functools.partial(_fox_sample_kernel, n_pages=n_pages, n_tok=n_tok),
        grid_spec=grid_spec,
        out_shape=[jax.ShapeDtypeStruct((n_seq, n_tok, SEG), BF16),
                   jax.ShapeDtypeStruct((n_seq, n_tok, SEG), F32),
                   jax.ShapeDtypeStruct((n_seq, n_tok, N_HEADS), F32)],
        compiler_params=_cparams(1),
        name="fox_sample",
    )(page_table, z4, z4, z4, z4, z4, fz_rows, bf.reshape(1, N_HEADS),
      jnp.tile(bf, n_tok).reshape(n_tok * N_HEADS, 1), gain8(q_gain), gain8(k_gain),
      *([ck] * n_pages), *([cv] * n_pages), *([cf] * n_pages))


FLAT = PAGE_SIZE * N_HEADS


def _flat_rmsnorm(x, gain):
    return x * lax.rsqrt(jnp.mean(x * x, axis=-1, keepdims=True) + EPS) * gain


def _flat_softmax_pv(tiles, v_pages, s_new, v_new):
    m = jnp.max(s_new, axis=-1, keepdims=True)
    for s in tiles:
        m = jnp.maximum(m, jnp.max(s, axis=-1, keepdims=True))
    p_new = jnp.exp(s_new - m)
    l = jnp.sum(p_new, axis=-1, keepdims=True)
    acc = jnp.dot(p_new, v_new, precision=HIGHEST, preferred_element_type=F32)
    for s, v_ref in zip(tiles, v_pages):
        pr = jnp.exp(s - m)
        l = l + jnp.sum(pr, axis=-1, keepdims=True)
        acc = acc + jnp.dot(pr.astype(BF16), v_ref[0].astype(BF16), preferred_element_type=F32)
    return acc / l


def _moba_flat_kernel(pt_ref, tab_ref, q_ref, k_ref, v_ref, g_ref, qg_ref, kg_ref, *rest, n_pages, n_tok):
    k_pages = rest[:n_pages]
    v_pages = rest[n_pages:2 * n_pages]
    o_ref, ka_ref = rest[2 * n_pages:]
    pages_per_block = MOBA_BLOCK // PAGE_SIZE
    nb_past = n_pages // pages_per_block
    past_len = n_pages * PAGE_SIZE
    n_rows = n_tok * N_HEADS
    scale = HEAD_DIM ** -0.5

    qn = _flat_rmsnorm(q_ref[0], qg_ref[...])
    kn = _flat_rmsnorm(k_ref[0], kg_ref[...])
    ka_ref[0] = kn
    tab = tab_ref[...]
    row = lax.broadcasted_iota(jnp.int32, (n_rows, 1), 0)
    row_q, row_h = row // N_HEADS, row % N_HEADS

    blk_lane = lax.broadcasted_iota(jnp.int32, (n_rows, nb_past), 1)
    gate = jnp.zeros((n_rows, nb_past), F32)
    for n in range(nb_past):
        tot = jnp.zeros((N_HEADS, HEAD_DIM), F32)
        for pg in range(n * pages_per_block, (n + 1) * pages_per_block):
            tot = tot + jnp.sum(k_pages[pg][0].reshape(PAGE_SIZE, N_HEADS, HEAD_DIM), axis=0)
        k_mean = jnp.concatenate([tot * (1.0 / MOBA_BLOCK)] * n_tok, axis=0)
        gate = jnp.where(blk_lane == n, jnp.sum(qn * k_mean, axis=-1, keepdims=True), gate)
    selpen = jnp.where(_top_blocks(gate, nb_past, nb_past, nb_past), 0.0, NEG)

    qb = (qn * scale).astype(BF16)
    col = lax.broadcasted_iota(jnp.int32, (n_rows, FLAT), 1)
    own_head = (col % N_HEADS) == row_h
    far = tab[:, T5_BUCKETS - 1:T5_BUCKETS]
    tiles = []
    for pg in range(n_pages):
        n = pg // pages_per_block
        s = _nt_dot(qb, k_pages[pg][0].astype(BF16))
        if past_len - (pg + 1) * PAGE_SIZE + 1 > T5_UPPER[-1]:
            bias = far
        else:
            bias = _t5_bias_rows(past_len + row_q - (pg * PAGE_SIZE + col // N_HEADS), tab)
        tiles.append(jnp.where(own_head, s + (bias + selpen[:, n:n + 1]), NEG))

    ncol = lax.broadcasted_iota(jnp.int32, (n_rows, n_rows), 1)
    rel = row_q - ncol // N_HEADS
    s_new = _nt_dot(qn * scale, kn, precision=HIGHEST) + _t5_bias_rows(rel, tab)
    s_new = jnp.where(((ncol % N_HEADS) == row_h) & (rel >= 0), s_new, NEG)

    o = _flat_softmax_pv(tiles, v_pages, s_new, v_ref[0])
    o_ref[0] = (o * _silu(g_ref[0])).astype(BF16)


def _flat_page_specs(n_pages, rows, width):
    return [pl.BlockSpec((1, rows, width), lambda b, pt, pg=pg: (pt[b, pg], 0, 0)) for pg in range(n_pages)]


def _flat_tokens(z3, seg, n_seq, n_tok):
    return z3[seg].reshape(n_seq, n_tok * N_HEADS, HEAD_DIM)


def _moba_flat(z3, n_seq, n_tok, cache_k, cache_v, page_table, t5_table, q_gain, k_gain):
    n_pages = page_table.shape[1]
    n_pool = cache_k.shape[0]
    assert MOBA_BLOCK % PAGE_SIZE == 0 and (n_pages * PAGE_SIZE) % MOBA_BLOCK == 0
    assert n_tok <= MOBA_BLOCK and n_pages // (MOBA_BLOCK // PAGE_SIZE) >= MOBA_TOPK
    n_rows = n_tok * N_HEADS
    tok = pl.BlockSpec((1, n_rows, HEAD_DIM), lambda b, pt: (b, 0, 0))
    gain = pl.BlockSpec((1, HEAD_DIM), lambda b, pt: (0, 0))
    tab_rows = jnp.tile(t5_table.astype(F32).T, (n_tok, 1))
    grid_spec = pltpu.PrefetchScalarGridSpec(
        num_scalar_prefetch=1,
        grid=(n_seq,),
        in_specs=[pl.BlockSpec((n_rows, T5_BUCKETS), lambda b, pt: (0, 0)), tok, tok, tok, tok, gain, gain]
                 + _flat_page_specs(n_pages, FLAT, HEAD_DIM) + _flat_page_specs(n_pages, FLAT, HEAD_DIM),
        out_specs=[tok, tok])
    ck = cache_k.reshape(n_pool, FLAT, HEAD_DIM)
    cv = cache_v.reshape(n_pool, FLAT, HEAD_DIM)
    flat = lambda seg: _flat_tokens(z3, seg, n_seq, n_tok)
    return pl.pallas_call(
        functools.partial(_moba_flat_kernel, n_pages=n_pages, n_tok=n_tok),
        grid_spec=grid_spec,
        out_shape=[jax.ShapeDtypeStruct((n_seq, n_rows, HEAD_DIM), BF16),
                   jax.ShapeDtypeStruct((n_seq, n_rows, HEAD_DIM), F32)],
        compiler_params=_cparams(1),
        name="moba_sample",
    )(page_table, tab_rows, flat(0), flat(1), flat(2), flat(3),
      q_gain.astype(F32).reshape(1, HEAD_DIM), k_gain.astype(F32).reshape(1, HEAD_DIM),
      *([ck] * n_pages), *([cv] * n_pages))


LOGF_CHUNK = PAIR // N_HEADS


def _fox_flat_kernel(pt_ref, q_ref, k_ref, v_ref, g_ref, fz_ref, fzr_ref, bf_ref, bfr_ref, qg_ref, kg_ref,
                     *rest, n_pages, n_tok):
    k_pages = rest[:n_pages]
    v_pages = rest[n_pages:2 * n_pages]
    f_pages = rest[2 * n_pages:3 * n_pages]
    o_ref, kc_ref, lf_ref = rest[3 * n_pages:]
    n_rows = n_tok * N_HEADS
    chunks = PAGE_SIZE // LOGF_CHUNK
    scale = HEAD_DIM ** -0.5

    qn = _flat_rmsnorm(q_ref[0], qg_ref[...])
    kn = _flat_rmsnorm(k_ref[0], kg_ref[...])
    kc_ref[0] = kn
    lf_ref[0] = _log_sigmoid(fz_ref[0] + bf_ref[...])
    row = lax.broadcasted_iota(jnp.int32, (n_rows, 1), 0)
    row_q, row_h = row // N_HEADS, row % N_HEADS

    lf = jnp.concatenate([f_pages[pg][0] for pg in range(n_pages)], axis=0)
    nr = lf.shape[0]
    li = lax.broadcasted_iota(jnp.int32, (PAIR, PAIR), 0)
    lj = lax.broadcasted_iota(jnp.int32, (PAIR, PAIR), 1)
    scan = ((li % N_HEADS == lj % N_HEADS) & (li <= lj)).astype(F32)
    last = (li == PAIR - N_HEADS + lj % N_HEADS).astype(F32)
    ri = lax.broadcasted_iota(jnp.int32, (nr, nr), 0)
    rj = lax.broadcasted_iota(jnp.int32, (nr, nr), 1)
    before = (rj < ri).astype(F32)
    in_row = jnp.dot(lf, scan, precision=HIGHEST, preferred_element_type=F32)
    totals = jnp.dot(in_row, last, precision=HIGHEST, preferred_element_type=F32)
    fk = in_row + jnp.dot(before, totals, precision=HIGHEST, preferred_element_type=F32)

    lane = lax.broadcasted_iota(jnp.int32, (n_rows, PAIR), 1)
    past_total = jnp.sum(jnp.where(lane == PAIR - N_HEADS + row_h, fk[nr - 1:nr, :], 0.0),
                         axis=-1, keepdims=True)
    lfr = _log_sigmoid(fzr_ref[0] + bfr_ref[...])
    cum_new = []
    run = past_total
    for tk in range(n_tok):
        run = run + lfr[:, tk:tk + 1]
        cum_new.append(run)
    fq = cum_new[n_tok - 1]
    for tk in range(n_tok - 2, -1, -1):
        fq = jnp.where(row_q == tk, cum_new[tk], fq)

    qb = (qn * scale).astype(BF16)
    col = lax.broadcasted_iota(jnp.int32, (n_rows, FLAT), 1)
    own_head = (col % N_HEADS) == row_h
    tiles = []
    for pg in range(n_pages):
        s = _nt_dot(qb, k_pages[pg][0].astype(BF16))
        fk_row = jnp.concatenate([fk[pg * chunks + c:pg * chunks + c + 1, :] for c in range(chunks)], axis=1)
        tiles.append(jnp.where(own_head, s + (fq - fk_row), NEG))

    ncol = lax.broadcasted_iota(jnp.int32, (n_rows, n_rows), 1)
    fk_new = cum_new[n_tok - 1]
    for tk in range(n_tok - 2, -1, -1):
        fk_new = jnp.where(ncol // N_HEADS == tk, cum_new[tk], fk_new)
    s_new = _nt_dot(qn * scale, kn, precision=HIGHEST) + (fq - fk_new)
    s_new = jnp.where(((ncol % N_HEADS) == row_h) & (ncol // N_HEADS <= row_q), s_new, NEG)

    o = _flat_softmax_pv(tiles, v_pages, s_new, v_ref[0])
    o_ref[0] = (o * _silu(g_ref[0])).astype(BF16)


def _fox_flat(z3, n_seq, n_tok, cache_k, cache_v, cache_logf, page_table, b_forget, q_gain, k_gain):
    n_pages = page_table.shape[1]
    n_pool = cache_k.shape[0]
    n_rows = n_tok * N_HEADS
    tok = pl.BlockSpec((1, n_rows, HEAD_DIM), lambda b, pt: (b, 0, 0))
    gain = pl.BlockSpec((1, HEAD_DIM), lambda b, pt: (0, 0))
    fz = z3[7].reshape(n_seq, n_tok, SEG)[:, :, :N_HEADS]
    fz_rows = jnp.tile(fz.transpose(0, 2, 1), (1, n_tok, 1))
    bf = b_forget.astype(F32)
    grid_spec = pltpu.PrefetchScalarGridSpec(
        num_scalar_prefetch=1,
        grid=(n_seq,),
        in_specs=[tok, tok, tok, tok,
                  pl.BlockSpec((1, n_tok, N_HEADS), lambda b, pt: (b, 0, 0)),
                  pl.BlockSpec((1, n_rows, n_tok), lambda b, pt: (b, 0, 0)),
                  pl.BlockSpec((1, N_HEADS), lambda b, pt: (0, 0)),
                  pl.BlockSpec((n_rows, 1), lambda b, pt: (0, 0)),
                  gain, gain]
                 + _flat_page_specs(n_pages, FLAT, HEAD_DIM) + _flat_page_specs(n_pages, FLAT, HEAD_DIM)
                 + _flat_page_specs(n_pages, PAGE_SIZE // LOGF_CHUNK, PAIR),
        out_specs=[tok, tok, pl.BlockSpec((1, n_tok, N_HEADS), lambda b, pt: (b, 0, 0))])
    ck = cache_k.reshape(n_pool, FLAT, HEAD_DIM)
    cv = cache_v.reshape(n_pool, FLAT, HEAD_DIM)
    cf = cache_logf.astype(F32).reshape(n_pool, PAGE_SIZE // LOGF_CHUNK, PAIR)
    flat = lambda seg: _flat_tokens(z3, seg, n_seq, n_tok)
    return pl.pallas_call(
        functools.partial(_fox_flat_kernel, n_pages=n_pages, n_tok=n_tok),
        grid_spec=grid_spec,
        out_shape=[jax.ShapeDtypeStruct((n_seq, n_rows, HEAD_DIM), BF16),
                   jax.ShapeDtypeStruct((n_seq, n_rows, HEAD_DIM), F32),
                   jax.ShapeDtypeStruct((n_seq, n_tok, N_HEADS), F32)],
        compiler_params=_cparams(1),
        name="fox_sample",
    )(page_table, flat(0), flat(1), flat(2), flat(3), fz, fz_rows, bf.reshape(1, N_HEADS),
      jnp.tile(bf, n_tok).reshape(n_rows, 1),
      q_gain.astype(F32).reshape(1, HEAD_DIM), k_gain.astype(F32).reshape(1, HEAD_DIM),
      *([ck] * n_pages), *([cv] * n_pages), *([cf] * n_pages))


CONV_SAMPLE_BATCH = 8


def _conv_sample_kernel(ua_ref, ub_ref, gd_ref, st_ref, w_ref, cb_ref, cg_ref, d_ref, cs_ref, buf_ref):
    n_tok = ua_ref.shape[2]
    hist = CONV_WIDTH - 1
    for e in range(ua_ref.shape[1]):
        buf_ref[0:hist, :] = st_ref[e]
        buf_ref[hist:hist + n_tok, :] = ua_ref[0, e] * jax.nn.sigmoid(ub_ref[0, e])
        y = jnp.zeros((n_tok, SEG), F32) + cb_ref[...]
        for k in range(CONV_WIDTH):
            y = y + w_ref[k:k + 1, :] * buf_ref[k:k + n_tok, :]
        mu = jnp.mean(y, axis=-1, keepdims=True)
        yc = y - mu
        var = jnp.mean(yc * yc, axis=-1, keepdims=True)
        yn = yc * lax.rsqrt(var + EPS) * cg_ref[...]
        d_ref[e] = (_silu(yn) * _silu(gd_ref[0, e])).astype(BF16)
        cs_ref[e] = buf_ref[n_tok:n_tok + hist, :]


def _conv_sample(z3, n_seq, n_tok, state, conv_w, conv_b, conv_gain):
    bt = math.gcd(n_seq, CONV_SAMPLE_BATCH)
    hist = CONV_WIDTH - 1
    z4 = z3.reshape(z3.shape[0], n_seq, n_tok, SEG)
    tok = lambda s: pl.BlockSpec((1, bt, n_tok, SEG), lambda b, s=s: (s, b, 0, 0))
    row = pl.BlockSpec((1, SEG), lambda b: (0, 0))
    return pl.pallas_call(
        _conv_sample_kernel,
        grid=(n_seq // bt,),
        in_specs=[tok(4), tok(5), tok(6),
                  pl.BlockSpec((bt, hist, SEG), lambda b: (b, 0, 0)),
                  pl.BlockSpec((CONV_WIDTH, SEG), lambda b: (0, 0)), row, row],
        out_specs=[pl.BlockSpec((bt, n_tok, SEG), lambda b: (b, 0, 0)),
                   pl.BlockSpec((bt, hist, SEG), lambda b: (b, 0, 0))],
        out_shape=[jax.ShapeDtypeStruct((n_seq, n_tok, SEG), BF16),
                   jax.ShapeDtypeStruct((n_seq, hist, SEG), F32)],
        scratch_shapes=[pltpu.VMEM((hist + n_tok + 6, SEG), F32)],
        compiler_params=_cparams(1),
        name="conv_sample",
    )(z4, z4, z4, state.astype(F32), conv_w.astype(F32), conv_b.astype(F32).reshape(1, SEG),
      conv_gain.astype(F32).reshape(1, SEG))


RET_SAMPLE_GROUPS = 8


def _even_layer_sample(x, cache_k, cache_v, ret_state, page_table, norm_g, w_in, q_gain, k_gain, t5_table,
                       ret_gain, w_out):
    b, l, d = x.shape
    x2d = x.reshape(b * l, d)
    w3, w2 = _even_weights(w_in, w_out)
    z3 = _proj_in(x2d, norm_g, w3)
    oa, ka = _moba_sample(z3, b, l, cache_k, cache_v, page_table, t5_table, q_gain, k_gain)
    past_len = page_table.shape[1] * PAGE_SIZE
    groups = math.gcd(b, RET_SAMPLE_GROUPS)
    orr, sp = _retention(z3, b, l, l * groups, groups, past_len, ret_gain, _pair_blockdiag(ret_state), F32,
                         "retention_sample")
    y = _proj_out(oa.reshape(b * l, SEG), orr, x2d, w2)
    return (y.reshape(b, l, d), ka.reshape(b, l, N_HEADS, HEAD_DIM),
            z3[2].reshape(b, l, N_HEADS, HEAD_DIM), _pair_unblockdiag(sp))


def _odd_layer_sample(x, cache_k, cache_v, cache_logf, conv_state, page_table, norm_g, w_in, b_forget, q_gain,
                      k_gain, conv_w, conv_b, conv_gain, w_out):
    b, l, d = x.shape
    x2d = x.reshape(b * l, d)
    w3, w2 = _odd_weights(w_in, w_out)
    z3 = _proj_in(x2d, norm_g, w3)
    oc, kc, logf = _fox_sample(z3, b, l, cache_k, cache_v, cache_logf, page_table, b_forget, q_gain, k_gain)
    dd, cs = _conv_sample(z3, b, l, conv_state, conv_w, conv_b, conv_gain)
    y = _proj_out(oc.reshape(b * l, SEG), dd.reshape(b * l, SEG), x2d, w2)
    return (y.reshape(b, l, d), kc.reshape(b, l, N_HEADS, HEAD_DIM),
            z3[2].reshape(b, l, N_HEADS, HEAD_DIM), logf, cs)


def kernel(x_prompt, x_sample, cache_moba_k, cache_moba_v, cache_fox_k, cache_fox_v, cache_fox_logf, state_ret, state_conv, page_table, norm_g_even, w_in_even, moba_q_gain, moba_k_gain, t5_table, ret_gain, w_out_even, norm_g_odd, w_in_odd, b_forget, fox_q_gain, fox_k_gain, conv_w, conv_b, conv_gain, w_out_odd):
    depth = norm_g_even.shape[0] + norm_g_odd.shape[0]
    xp, xs = x_prompt, x_sample
    outs = {name: [] for name in ("mk_p", "mv_p", "mk_s", "mv_s", "rs_p", "rs_s",
                                  "fk_p", "fv_p", "fl_p", "fk_s", "fv_s", "fl_s", "cs_p", "cs_s")}
    for layer in range(depth):
        i = layer // 2
        if layer % 2 == 0:
            w = (norm_g_even[i], w_in_even[i], moba_q_gain[i], moba_k_gain[i], t5_table, ret_gain[i], w_out_even[i])
            xp, k1, v1, s1 = _even_layer_prompt(xp, *w)
            xs, k2, v2, s2 = _even_layer_sample(xs, cache_moba_k[i], cache_moba_v[i], state_ret[i], page_table, *w)
            for name, val in zip(("mk_p", "mv_p", "rs_p", "mk_s", "mv_s", "rs_s"), (k1, v1, s1, k2, v2, s2)):
                outs[name].append(val)
        else:
            w = (norm_g_odd[i], w_in_odd[i], b_forget[i], fox_q_gain[i], fox_k_gain[i],
                 conv_w[i], conv_b[i], conv_gain[i], w_out_odd[i])
            xp, k1, v1, f1, c1 = _odd_layer_prompt(xp, *w)
            xs, k2, v2, f2, c2 = _odd_layer_sample(xs, cache_fox_k[i], cache_fox_v[i], cache_fox_logf[i],
                                                   state_conv[i], page_table, *w)
            for name, val in zip(("fk_p", "fv_p", "fl_p", "cs_p", "fk_s", "fv_s", "fl_s", "cs_s"),
                                 (k1, v1, f1, c1, k2, v2, f2, c2)):
                outs[name].append(val)
    st = lambda name: jnp.stack(outs[name])
    return (xp, xs, st("mk_p"), st("mv_p"), st("mk_s"), st("mv_s"), st("rs_p"), st("rs_s"),
            st("fk_p"), st("fv_p"), st("fl_p"), st("fk_s"), st("fv_s"), st("fl_s"), st("cs_p"), st("cs_s"))
```

```python
import functools
import math

import numpy as np
import jax
import jax.numpy as jnp
from jax import lax
from jax.experimental import pallas as pl
from jax.experimental.pallas import tpu as pltpu

F32 = jnp.float32
BF16 = jnp.bfloat16
HIGHEST = lax.Precision.HIGHEST

HEAD_DIM = 64
PAIR = 2 * HEAD_DIM
SEG = 512
N_HEADS = SEG // HEAD_DIM
EPS = 1e-6
MOBA_BLOCK = 256
MOBA_TOPK = 3
T5_BUCKETS = 32
T5_MAX_DIST = 128
ROPE_BASE = 10000.0
CONV_WIDTH = 31
PAGE_SIZE = 128
NEG = -1e30
ATT_TILE = 256
V7X_VMEM_LIMIT = 48 * 1024 * 1024


def _t5_bucket_upper_bounds():
    max_exact = T5_BUCKETS // 2
    rel = np.arange(0, 4 * T5_MAX_DIST)
    relf = np.maximum(rel, 1).astype(np.float64)
    large = max_exact + np.trunc(np.log(relf / max_exact) / math.log(T5_MAX_DIST / max_exact)
                                 * (T5_BUCKETS - max_exact)).astype(np.int64)
    bucket = np.where(rel < max_exact, rel, np.minimum(large, T5_BUCKETS - 1))
    return tuple(int(rel[bucket <= b].max()) for b in range(T5_BUCKETS - 1))


T5_UPPER = _t5_bucket_upper_bounds()


def _cparams(n_axes):
    return pltpu.CompilerParams(dimension_semantics=("arbitrary",) * n_axes,
                                vmem_limit_bytes=V7X_VMEM_LIMIT)


def _silu(x):
    return x * jax.nn.sigmoid(x)


def _lo_mask():
    return lax.broadcasted_iota(jnp.int32, (1, PAIR), 1) < HEAD_DIM


def _pair_sum(x, lo):
    s0 = jnp.sum(jnp.where(lo, x, 0.0), axis=-1, keepdims=True)
    s1 = jnp.sum(jnp.where(lo, 0.0, x), axis=-1, keepdims=True)
    return jnp.where(lo, s0, s1)


def _pair_rmsnorm(x, gain, lo):
    ms = _pair_sum(x * x, lo) * (1.0 / HEAD_DIM)
    return x * lax.rsqrt(ms + EPS) * gain


def _t5_bias(rel, tab_ref, h):
    bias = jnp.full(rel.shape, tab_ref[T5_BUCKETS - 1, h], F32)
    for b in range(T5_BUCKETS - 2, -1, -1):
        bias = jnp.where(rel <= T5_UPPER[b], tab_ref[b, h], bias)
    return bias


def _top_blocks(gate, n_valid, own, width):
    lane = lax.broadcasted_iota(jnp.int32, gate.shape, 1)
    g = jnp.where(lane < n_valid, gate, -jnp.inf)
    sel = lane == own
    for _ in range(MOBA_TOPK):
        m = jnp.max(g, axis=-1, keepdims=True)
        idx = jnp.min(jnp.where(g == m, lane, width), axis=-1, keepdims=True)
        pick = (lane == idx) & (m > -jnp.inf)
        sel = sel | pick
        g = jnp.where(pick, -jnp.inf, g)
    return sel


def _nt_dot(a, b, precision=None):
    return lax.dot_general(a, b, (((1,), (1,)), ((), ())), precision=precision,
                           preferred_element_type=F32)


def _proj_in_kernel(x_ref, g_ref, w_ref, o_ref, xn_ref):
    j = pl.program_id(1)

    @pl.when(j == 0)
    def _():
        x = x_ref[...]
        ms = jnp.mean(x * x, axis=-1, keepdims=True)
        xn_ref[...] = (x * lax.rsqrt(ms + EPS) * g_ref[...]).astype(BF16)

    o_ref[0] = jnp.dot(xn_ref[...], w_ref[j], preferred_element_type=F32)


def _proj_in(x2d, gain, w3):
    n, d = x2d.shape
    nseg = w3.shape[0]
    tm = min(n, 1024)
    return pl.pallas_call(
        _proj_in_kernel,
        grid=(n // tm, nseg),
        in_specs=[pl.BlockSpec((tm, d), lambda i, j: (i, 0)),
                  pl.BlockSpec((1, d), lambda i, j: (0, 0)),
                  pl.BlockSpec((nseg, d, SEG), lambda i, j: (0, 0, 0))],
        out_specs=pl.BlockSpec((1, tm, SEG), lambda i, j: (j, i, 0)),
        out_shape=jax.ShapeDtypeStruct((nseg, n, SEG), F32),
        scratch_shapes=[pltpu.VMEM((tm, d), BF16)],
        compiler_params=_cparams(2),
        name="proj_in",
    )(x2d, gain.reshape(1, d), w3)


def _proj_out_kernel(a_ref, b_ref, x_ref, w_ref, o_ref):
    acc = jnp.dot(a_ref[...], w_ref[0], preferred_element_type=F32)
    acc = acc + jnp.dot(b_ref[...], w_ref[1], preferred_element_type=F32)
    o_ref[...] = x_ref[...] + acc


def _proj_out(a, b, x2d, w2):
    n, d = x2d.shape
    tm = min(n, 512)
    return pl.pallas_call(
        _proj_out_kernel,
        grid=(n // tm,),
        in_specs=[pl.BlockSpec((tm, SEG), lambda i: (i, 0)),
                  pl.BlockSpec((tm, SEG), lambda i: (i, 0)),
                  pl.BlockSpec((tm, d), lambda i: (i, 0)),
                  pl.BlockSpec((2, SEG, d), lambda i: (0, 0, 0))],
        out_specs=pl.BlockSpec((tm, d), lambda i: (i, 0)),
        out_shape=jax.ShapeDtypeStruct((n, d), F32),
        compiler_params=_cparams(1),
        name="proj_out",
    )(a, b, x2d, w2)


def _top_blocks_t(gate_t, n_valid, own, width):
    blk = lax.broadcasted_iota(jnp.int32, gate_t.shape, 0)
    g = jnp.where(blk < n_valid, gate_t, -jnp.inf)
    sel = blk == own
    for _ in range(MOBA_TOPK):
        m = jnp.max(g, axis=0, keepdims=True)
        idx = jnp.min(jnp.where(g == m, blk, width), axis=0, keepdims=True)
        pick = (blk == idx) & (m > -jnp.inf)
        sel = sel | pick
        g = jnp.where(pick, -jnp.inf, g)
    return sel


def _tn_dot(a, b):
    return lax.dot_general(a, b, (((0,), (0,)), ((), ())), preferred_element_type=F32)


def _online_softmax_step(carry, s, v):
    m, l, acc = carry
    m_new = jnp.maximum(m, jnp.max(s, axis=-1, keepdims=True))
    alpha = jnp.exp(m - m_new)
    pr = jnp.exp(s - m_new)
    l = alpha * l + jnp.sum(pr, axis=-1, keepdims=True)
    acc = alpha * acc + jnp.dot(pr.astype(BF16), v, preferred_element_type=F32)
    return m_new, l, acc


def _softmax_init(rows):
    return (jnp.full((rows, 1), NEG, F32), jnp.zeros((rows, 1), F32), jnp.zeros((rows, PAIR), F32))


def _flash_attention(n_tiles, logits_fn, v_tile_fn, m_ref, l_ref, acc_ref):
    s = logits_fn(0, True)
    reps = s.shape[1] // PAIR
    m = jnp.max(s, axis=-1, keepdims=True)
    pr = jnp.exp(s - m)
    m_ref[...] = jnp.broadcast_to(m, m_ref.shape)
    l_ref[...] = jnp.broadcast_to(jnp.sum(pr, axis=-1, keepdims=True), l_ref.shape)
    acc_ref[...] = jnp.dot(pr.astype(BF16), v_tile_fn(0), preferred_element_type=F32)

    def body(step, carry):
        s = logits_fn(step, False)
        m_prev = m_ref[...]
        m_new = jnp.maximum(m_prev, jnp.max(s, axis=-1, keepdims=True))
        alpha = jnp.exp(m_prev - m_new)
        pr = jnp.exp(s - jnp.concatenate([m_new] * reps, axis=1))
        m_ref[...] = m_new
        l_ref[...] = alpha * l_ref[...] + jnp.sum(pr, axis=-1, keepdims=True)
        acc_ref[...] = alpha * acc_ref[...] + jnp.dot(pr.astype(BF16), v_tile_fn(step),
                                                      preferred_element_type=F32)
        return carry

    lax.fori_loop(1, n_tiles, body, 0)
    return acc_ref[...] / l_ref[...]


ATT_PAD_BLOCKS = 3


def _flash_attention_2x(n_tiles, logits_fn, v_tile_fn, sa_ref, sb_ref, m_ref, l_ref, acc_ref):
    reps = sa_ref.shape[1] // PAIR
    sa_ref[...] = logits_fn(0, True)
    m_ref[...] = jnp.full(m_ref.shape, NEG, F32)
    l_ref[...] = jnp.zeros(l_ref.shape, F32)
    acc_ref[...] = jnp.zeros(acc_ref.shape, F32)

    def update(s_ref, step):
        s = s_ref[...]
        m_prev = m_ref[...]
        m_new = jnp.maximum(m_prev, jnp.max(s, axis=-1, keepdims=True))
        alpha = jnp.exp(m_prev - m_new)
        pr = jnp.exp(s - jnp.concatenate([m_new] * reps, axis=1))
        m_ref[...] = m_new
        l_ref[...] = alpha * l_ref[...] + jnp.sum(pr, axis=-1, keepdims=True)
        acc_ref[...] = alpha * acc_ref[...] + jnp.dot(pr.astype(BF16), v_tile_fn(step),
                                                      preferred_element_type=F32)

    def body(j, carry):
        sb_ref[...] = logits_fn(2 * j + 1, False)
        update(sa_ref, 2 * j)
        sa_ref[...] = logits_fn(jnp.minimum(2 * j + 2, n_tiles), False)
        update(sb_ref, 2 * j + 1)
        return carry

    lax.fori_loop(0, (n_tiles + 1) // 2, body, 0)
    return acc_ref[...] / l_ref[...]


def _moba_prompt_kernel(tab_ref, q_ref, k_ref, v_ref, g_ref, qg_ref, kg_ref,
                        o_ref, kt_ref, vt_ref,
                        kn_ref, vb_ref, km_ref, bias_ref, sa_ref, sb_ref, m_ref, l_ref, acc_ref):
    p = pl.program_id(1)
    i = pl.program_id(2)
    t = ATT_TILE
    nblk = k_ref.shape[1] // t
    lo = _lo_mask()

    @pl.when(i == 0)
    def _prep():
        lane = lax.broadcasted_iota(jnp.int32, (t, PAIR), 1)
        for c in range(ATT_PAD_BLOCKS):
            kn_ref[c * t:(c + 1) * t, 0:PAIR] = jnp.zeros((t, PAIR), BF16)
            kn_ref[c * t:(c + 1) * t, PAIR:2 * PAIR] = (lane == nblk).astype(BF16)
            vb_ref[c * t:(c + 1) * t, :] = jnp.zeros((t, PAIR), BF16)

        for c in range(nblk):
            r0 = c * t
            d0 = r0 + ATT_PAD_BLOCKS * t
            kn = _pair_rmsnorm(k_ref[0, r0:r0 + t, :], kg_ref[...], lo)
            v = v_ref[0, r0:r0 + t, :]
            kt_ref[0, :, r0:r0 + t] = kn.T
            vt_ref[0, :, r0:r0 + t] = v.T
            kn_ref[d0:d0 + t, 0:PAIR] = kn.astype(BF16)
            kn_ref[d0:d0 + t, PAIR:2 * PAIR] = (lane == c).astype(BF16)
            km_ref[c:c + 1, :] = jnp.sum(kn, axis=0, keepdims=True) * (1.0 / t)
            vb_ref[d0:d0 + t, :] = v.astype(BF16)
        rel = (lax.broadcasted_iota(jnp.int32, (t, t), 0) - lax.broadcasted_iota(jnp.int32, (t, t), 1))
        for hh in range(2):
            h = 2 * p + hh
            bias_ref[0, hh * t:(hh + 1) * t, 0:t] = _t5_bias(rel + t, tab_ref, h)
            bias_ref[0, hh * t:(hh + 1) * t, t:2 * t] = jnp.where(rel >= 0, _t5_bias(rel, tab_ref, h), NEG)
            bias_ref[1, hh * t:(hh + 1) * t, :] = jnp.full((t, 2 * t), tab_ref[T5_BUCKETS - 1, h], F32)

    qn = _pair_rmsnorm(q_ref[0], qg_ref[...], lo)
    km = km_ref[...]
    km_heads = jnp.concatenate([jnp.where(lo, km, 0.0), jnp.where(lo, 0.0, km),
                                jnp.zeros((PAIR - 2 * nblk, PAIR), F32)], axis=0)
    gate_all = _nt_dot(qn, km_heads, precision=HIGHEST).T
    no_block = jnp.where(lax.broadcasted_iota(jnp.int32, (PAIR - nblk, t), 0) == 0, NEG, 0.0)
    qs, pens = [], []
    for hh in range(2):
        hm = lo if hh == 0 else jnp.logical_not(lo)
        qh = jnp.where(hm, qn, 0.0)
        gate_t = gate_all[hh * nblk:(hh + 1) * nblk]
        sel = _top_blocks_t(gate_t, i, i, nblk)
        pen_t = jnp.concatenate([jnp.where(sel, 0.0, NEG), no_block], axis=0)
        pens.append(pen_t.T)
        qs.append(qh * (HEAD_DIM ** -0.5))
    q2 = jnp.concatenate([jnp.concatenate(qs, axis=0), jnp.concatenate(pens, axis=0)], axis=1).astype(BF16)

    def tile_rows(step):
        return pl.ds(pl.multiple_of((i - 2 * step - 1 + ATT_PAD_BLOCKS) * t, t), 2 * t)

    def logits(step, first):
        return _nt_dot(q2, kn_ref[tile_rows(step), :]) + bias_ref[0 if first else 1]

    o = _flash_attention_2x((i + 2) // 2, logits, lambda step: vb_ref[tile_rows(step), :],
                            sa_ref, sb_ref, m_ref, l_ref, acc_ref)
    o = jnp.where(lo, o[0:t], o[t:2 * t])
    o_ref[...] = (o * _silu(g_ref[0])).astype(BF16)


def _heads_from_token_minor(xt, batch, seq):
    return xt.reshape(batch, N_HEADS, HEAD_DIM, seq).transpose(0, 3, 1, 2)


def _moba_prompt(z3, batch, seq, t5_table, q_gain, k_gain):
    n = batch * seq
    t = ATT_TILE
    nq = seq // t
    npair = SEG // PAIR
    gain2 = lambda g: jnp.tile(g.astype(F32), 2).reshape(1, PAIR)
    return pl.pallas_call(
        _moba_prompt_kernel,
        grid=(batch, npair, nq),
        in_specs=[pl.BlockSpec(memory_space=pltpu.SMEM),
                  pl.BlockSpec((1, t, PAIR), lambda b, p, i: (0, b * nq + i, p)),
                  pl.BlockSpec((1, seq, PAIR), lambda b, p, i: (1, b, p)),
                  pl.BlockSpec((1, seq, PAIR), lambda b, p, i: (2, b, p)),
                  pl.BlockSpec((1, t, PAIR), lambda b, p, i: (3, b * nq + i, p)),
                  pl.BlockSpec((1, PAIR), lambda b, p, i: (0, 0)),
                  pl.BlockSpec((1, PAIR), lambda b, p, i: (0, 0))],
        out_specs=[pl.BlockSpec((t, PAIR), lambda b, p, i: (b * nq + i, p)),
                   pl.BlockSpec((1, PAIR, seq), lambda b, p, i: (b, p, 0)),
                   pl.BlockSpec((1, PAIR, seq), lambda b, p, i: (b, p, 0))],
        out_shape=[jax.ShapeDtypeStruct((n, SEG), BF16),
                   jax.ShapeDtypeStruct((batch, SEG, seq), F32),
                   jax.ShapeDtypeStruct((batch, SEG, seq), F32)],
        scratch_shapes=[pltpu.VMEM((seq + ATT_PAD_BLOCKS * t, 2 * PAIR), BF16),
                        pltpu.VMEM((seq + ATT_PAD_BLOCKS * t, PAIR), BF16),
                        pltpu.VMEM((seq // t, PAIR), F32),
                        pltpu.VMEM((2, 2 * t, 2 * t), F32),
                        pltpu.VMEM((2 * t, 2 * t), F32),
                        pltpu.VMEM((2 * t, 2 * t), F32),
                        pltpu.VMEM((2 * t, PAIR), F32),
                        pltpu.VMEM((2 * t, PAIR), F32),
                        pltpu.VMEM((2 * t, PAIR), F32)],
        compiler_params=_cparams(3),
        name="moba_prompt",
    )(t5_table.astype(F32), z3, z3, z3, z3, gain2(q_gain), gain2(k_gain))


def _rope_pair(x, cos, sin_signed):
    up = pltpu.roll(x, PAIR - HEAD_DIM // 2, 1)
    dn = pltpu.roll(x, HEAD_DIM // 2, 1)
    lane = lax.broadcasted_iota(jnp.int32, (1, PAIR), 1)
    first_half = (lane % HEAD_DIM) < (HEAD_DIM // 2)
    return x * cos + jnp.where(first_half, up, dn) * sin_signed


def _retention_kernel(lg_ref, q_ref, k_ref, v_ref, g_ref, cos_ref, sin_ref, gain_ref, s0_ref,
                      o_ref, sout_ref, s_ref, *, groups, mm_dtype):
    p = pl.program_id(1)
    c = pl.program_id(2)
    n = q_ref.shape[0]
    glen = n // groups
    lo = _lo_mask()

    @pl.when(c == 0)
    def _():
        s_ref[...] = s0_ref[...]

    cos = cos_ref[...]
    sin = sin_ref[...]
    qr = _rope_pair(q_ref[...], cos, sin)
    kr = _rope_pair(k_ref[...], cos, sin) * (HEAD_DIM ** -0.5)
    vb = v_ref[...].astype(mm_dtype)
    lg0 = lg_ref[2 * p]
    lg1 = lg_ref[2 * p + 1]
    lg_lane = jnp.where(lo, lg0, lg1)
    row = lax.broadcasted_iota(jnp.int32, (n, 1), 0)
    pos = (row % glen).astype(F32)
    grp = row // glen
    q_dec = jnp.exp(lg_lane * (pos + 1.0))
    k_dec = jnp.exp(lg_lane * (float(glen - 1) - pos))
    qd = (qr * q_dec).astype(mm_dtype)
    o = jnp.zeros((n, PAIR), F32)
    for g in range(groups):
        og = jnp.dot(qd, s_ref[g].astype(mm_dtype), preferred_element_type=F32)
        o = o + (og if groups == 1 else jnp.where(grp == g, og, 0.0))
    ri = lax.broadcasted_iota(jnp.int32, (n, n), 0)
    ci = lax.broadcasted_iota(jnp.int32, (n, n), 1)
    causal = (ri >= ci) & ((ri // glen) == (ci // glen))
    dpos = jnp.where(causal, ri - ci, 0).astype(F32)
    krb = kr.astype(mm_dtype)
    for hh in range(2):
        hm = lo if hh == 0 else jnp.logical_not(lo)
        lgh = lg0 if hh == 0 else lg1
        intra = jnp.where(causal, jnp.exp(lgh * dpos), 0.0)
        a = _nt_dot(jnp.where(hm, qr, 0.0).astype(mm_dtype), krb) * intra
        oh = jnp.dot(a.astype(mm_dtype), vb, preferred_element_type=F32)
        o = o + jnp.where(hm, oh, 0.0)
    srow = lax.broadcasted_iota(jnp.int32, (PAIR, 1), 0) < HEAD_DIM
    c_dec = jnp.exp(jnp.where(srow, lg0, lg1) * float(glen))
    same_head = srow == lo
    kd = kr * k_dec
    for g in range(groups):
        kg = kd if groups == 1 else jnp.where(grp == g, kd, 0.0)
        kv = lax.dot_general(kg.astype(mm_dtype), vb, (((0,), (0,)), ((), ())),
                             preferred_element_type=F32)
        new_state = jnp.where(same_head, s_ref[g] * c_dec + kv, 0.0)
        s_ref[g] = new_state
        sout_ref[g] = new_state

    mu = _pair_sum(o, lo) * (1.0 / HEAD_DIM)
    oc = o - mu
    var = _pair_sum(oc * oc, lo) * (1.0 / HEAD_DIM)
    on = oc * lax.rsqrt(var + EPS) * gain_ref[...]
    o_ref[...] = (on * _silu(g_ref[...])).astype(BF16)


def _pair_blockdiag(s):
    b = s.shape[0]
    s = s.astype(F32).reshape(b, N_HEADS // 2, 2, HEAD_DIM, HEAD_DIM)
    z = jnp.zeros_like(s[:, :, 0])
    top = jnp.concatenate([s[:, :, 0], z], axis=-1)
    bot = jnp.concatenate([z, s[:, :, 1]], axis=-1)
    return jnp.concatenate([top, bot], axis=-2)


def _pair_unblockdiag(sp):
    b = sp.shape[0]
    s0 = sp[:, :, :HEAD_DIM, :HEAD_DIM]
    s1 = sp[:, :, HEAD_DIM:, HEAD_DIM:]
    return jnp.stack([s0, s1], axis=2).reshape(b, N_HEADS, HEAD_DIM, HEAD_DIM)


def _rope_tables(pos):
    half = HEAD_DIM // 2
    inv = ROPE_BASE ** (-jnp.arange(half, dtype=F32) / half)
    ang = pos.astype(F32)[:, None] * inv[None, :]
    cos, sin = jnp.cos(ang), jnp.sin(ang)
    cos_t = jnp.tile(cos, (1, PAIR // half))
    sin_t = jnp.tile(jnp.concatenate([-sin, sin], axis=-1), (1, 2))
    return cos_t, sin_t


def _log_gamma():
    return jnp.log1p(-jnp.exp2(-5.0 - jnp.arange(N_HEADS, dtype=F32)))


def _retention(z3, n_seq, seq, rows, groups, pos0, ret_gain, state_pairs, mm_dtype, name):
    n = n_seq * seq
    nc = (seq * groups) // rows
    nb = n_seq // groups
    npair = SEG // PAIR
    cos_t, sin_t = _rope_tables(pos0 + jnp.arange(seq, dtype=jnp.int32))
    if groups > 1:
        cos_t, sin_t = jnp.tile(cos_t, (groups, 1)), jnp.tile(sin_t, (groups, 1))
    z4 = z3.reshape(z3.shape[0], n // rows, rows, SEG)
    tok = lambda s: pl.BlockSpec((None, None, rows, PAIR), lambda b, p, c, s=s: (s, b * nc + c, 0, p))
    o, sp = pl.pallas_call(
        functools.partial(_retention_kernel, groups=groups, mm_dtype=mm_dtype),
        grid=(nb, npair, nc),
        in_specs=[pl.BlockSpec(memory_space=pltpu.SMEM),
                  tok(4), tok(5), tok(6), tok(7),
                  pl.BlockSpec((rows, PAIR), lambda b, p, c: (c, 0)),
                  pl.BlockSpec((rows, PAIR), lambda b, p, c: (c, 0)),
                  pl.BlockSpec((1, PAIR), lambda b, p, c: (0, p)),
                  pl.BlockSpec((groups, None, PAIR, PAIR), lambda b, p, c: (b, p, 0, 0))],
        out_specs=[pl.BlockSpec((None, rows, PAIR), lambda b, p, c: (b * nc + c, 0, p)),
                   pl.BlockSpec((groups, None, PAIR, PAIR), lambda b, p, c: (b, p, 0, 0))],
        out_shape=[jax.ShapeDtypeStruct((n // rows, rows, SEG), BF16),
                   jax.ShapeDtypeStruct((n_seq, npair, PAIR, PAIR), F32)],
        scratch_shapes=[pltpu.VMEM((groups, PAIR, PAIR), F32)],
        compiler_params=_cparams(3),
        name=name,
    )(_log_gamma(), z4, z4, z4, z4, cos_t, sin_t, ret_gain.astype(F32).reshape(1, SEG), state_pairs)
    return o.reshape(n, SEG), sp


def _even_weights(w_in, w_out):
    d = w_in.shape[0]
    w3 = w_in.reshape(d, 8, SEG).transpose(1, 0, 2).astype(BF16)
    w2 = w_out.reshape(2, SEG, d).astype(BF16)
    return w3, w2


def _even_layer_prompt(x, norm_g, w_in, q_gain, k_gain, t5_table, ret_gain, w_out):
    b, l, d = x.shape
    x2d = x.reshape(b * l, d)
    w3, w2 = _even_weights(w_in, w_out)
    z3 = _proj_in(x2d, norm_g, w3)
    oa, kt, vt = _moba_prompt(z3, b, l, t5_table, q_gain, k_gain)
    zero_state = jnp.zeros((b, SEG // PAIR, PAIR, PAIR), F32)
    orr, sp = _retention(z3, b, l, min(l, 256), 1, 0, ret_gain, zero_state, BF16, "retention_prompt")
    y = _proj_out(oa, orr, x2d, w2)
    return (y.reshape(b, l, d), _heads_from_token_minor(kt, b, l), _heads_from_token_minor(vt, b, l),
            _pair_unblockdiag(sp))


def _log_sigmoid(x):
    y = -x
    return -(jnp.maximum(y, 0.0) + jnp.log1p(jnp.exp(-jnp.abs(y))))


def _logf_kernel(fz_ref, b_ref, lf_ref, fc_ref):
    seq = fz_ref.shape[1]
    t = ATT_TILE
    tri = (lax.broadcasted_iota(jnp.int32, (t, t), 0) >= lax.broadcasted_iota(jnp.int32, (t, t), 1)).astype(F32)
    carry = jnp.zeros((1, PAIR), F32)
    for c in range(seq // t):
        lf = _log_sigmoid(fz_ref[0, c * t:(c + 1) * t, :] + b_ref[...])
        cs = jnp.dot(tri, lf, precision=HIGHEST, preferred_element_type=F32) + carry
        lf_ref[0, c * t:(c + 1) * t, :] = lf[:, :N_HEADS]
        fc_ref[0, c * t:(c + 1) * t, :] = cs[:, :N_HEADS]
        carry = cs[t - 1:t, :]


def _logf_prompt(z3, batch, seq, b_forget):
    bpad = jnp.zeros((1, PAIR), F32).at[0, :N_HEADS].set(b_forget.astype(F32))
    return pl.pallas_call(
        _logf_kernel,
        grid=(batch,),
        in_specs=[pl.BlockSpec((1, seq, PAIR), lambda b: (7, b, 0)),
                  pl.BlockSpec((1, PAIR), lambda b: (0, 0))],
        out_specs=[pl.BlockSpec((1, seq, N_HEADS), lambda b: (b, 0, 0)),
                   pl.BlockSpec((1, seq, N_HEADS), lambda b: (b, 0, 0))],
        out_shape=[jax.ShapeDtypeStruct((batch, seq, N_HEADS), F32),
                   jax.ShapeDtypeStruct((batch, seq, N_HEADS), F32)],
        compiler_params=_cparams(1),
        name="logf_prompt",
    )(z3, bpad)


def _lane_split3(x):
    hi = x.astype(BF16).astype(F32)
    rest = x - hi
    mid = rest.astype(BF16).astype(F32)
    r = lax.broadcasted_iota(jnp.int32, x.shape, 1) % 3
    return jnp.where(r == 0, hi, jnp.where(r == 1, mid, rest - mid))


def _fox_prompt_kernel(q_ref, k_ref, v_ref, g_ref, fq_ref, fk_ref, qg_ref, kg_ref,
                       o_ref, kt_ref, vt_ref,
                       kn_ref, vb_ref, mask_ref, sa_ref, sb_ref, m_ref, l_ref, acc_ref):
    p = pl.program_id(1)
    i = pl.program_id(2)
    t = ATT_TILE
    nblk = k_ref.shape[1] // t
    lo = _lo_mask()
    lane = lax.broadcasted_iota(jnp.int32, (t, PAIR), 1)
    prow = lax.broadcasted_iota(jnp.int32, (N_HEADS, PAIR), 0)
    plane = lax.broadcasted_iota(jnp.int32, (N_HEADS, PAIR), 1)

    @pl.when(i == 0)
    def _prep():
        for c in range(ATT_PAD_BLOCKS):
            kn_ref[c * t:(c + 1) * t, 0:PAIR] = jnp.zeros((t, PAIR), BF16)
            kn_ref[c * t:(c + 1) * t, PAIR:2 * PAIR] = jnp.where((lane == 0) | (lane == 3), NEG, 0.0).astype(BF16)
            vb_ref[c * t:(c + 1) * t, :] = jnp.zeros((t, PAIR), BF16)
        place = jnp.where(((prow == 2 * p) & (plane < 3)) | ((prow == 2 * p + 1) & (plane >= 3) & (plane < 6)),
                          -1.0, 0.0)

        for c in range(nblk):
            r0 = c * t
            d0 = r0 + ATT_PAD_BLOCKS * t
            kn = _pair_rmsnorm(k_ref[0, r0:r0 + t, :], kg_ref[...], lo)
            v = v_ref[0, r0:r0 + t, :]
            kt_ref[0, :, r0:r0 + t] = kn.T
            vt_ref[0, :, r0:r0 + t] = v.T
            kn_ref[d0:d0 + t, 0:PAIR] = kn.astype(BF16)
            neg_fk = jnp.dot(fk_ref[0, r0:r0 + t, :], place, precision=HIGHEST, preferred_element_type=F32)
            up = jnp.where((lane >= 6) & (lane < 9), 1.0, _lane_split3(neg_fk))
            kn_ref[d0:d0 + t, PAIR:2 * PAIR] = up.astype(BF16)
            vb_ref[d0:d0 + t, :] = v.astype(BF16)
        row = lax.broadcasted_iota(jnp.int32, (2 * t, 2 * t), 0) % t
        col = lax.broadcasted_iota(jnp.int32, (2 * t, 2 * t), 1) - t
        mask_ref[...] = jnp.where(row >= col, 0.0, NEG)

    qn = _pair_rmsnorm(q_ref[0], qg_ref[...], lo)
    qs, ups = [], []
    for hh in range(2):
        hm = lo if hh == 0 else jnp.logical_not(lo)
        qs.append(jnp.where(hm, qn, 0.0) * (HEAD_DIM ** -0.5))
        place_q = jnp.where((prow == 2 * p + hh) & (plane >= 6) & (plane < 9), 1.0, 0.0)
        fq = jnp.dot(fq_ref[0], place_q, precision=HIGHEST, preferred_element_type=F32)
        ups.append(jnp.where((lane >= 3 * hh) & (lane < 3 * hh + 3), 1.0, _lane_split3(fq)))
    q2 = jnp.concatenate([jnp.concatenate(qs, axis=0), jnp.concatenate(ups, axis=0)], axis=1).astype(BF16)

    def tile_rows(step):
        return pl.ds(pl.multiple_of((i - 2 * step - 1 + ATT_PAD_BLOCKS) * t, t), 2 * t)

    def logits(step, first):
        s = _nt_dot(q2, kn_ref[tile_rows(step), :])
        return s + mask_ref[...] if first else s

    o = _flash_attention_2x((i + 2) // 2, logits, lambda step: vb_ref[tile_rows(step), :],
                            sa_ref, sb_ref, m_ref, l_ref, acc_ref)
    o = jnp.where(lo, o[0:t], o[t:2 * t])
    o_ref[...] = (o * _silu(g_ref[0])).astype(BF16)


def _fox_prompt(z3, batch, seq, fcum, q_gain, k_gain):
    n = batch * seq
    t = ATT_TILE
    nq = seq // t
    npair = SEG // PAIR
    gain2 = lambda g: jnp.tile(g.astype(F32), 2).reshape(1, PAIR)
    return pl.pallas_call(
        _fox_prompt_kernel,
        grid=(batch, npair, nq),
        in_specs=[pl.BlockSpec((1, t, PAIR), lambda b, p, i: (0, b * nq + i, p)),
                  pl.BlockSpec((1, seq, PAIR), lambda b, p, i: (1, b, p)),
                  pl.BlockSpec((1, seq, PAIR), lambda b, p, i: (2, b, p)),
                  pl.BlockSpec((1, t, PAIR), lambda b, p, i: (3, b * nq + i, p)),
                  pl.BlockSpec((1, t, N_HEADS), lambda b, p, i: (b, i, 0)),
                  pl.BlockSpec((1, seq, N_HEADS), lambda b, p, i: (b, 0, 0)),
                  pl.BlockSpec((1, PAIR), lambda b, p, i: (0, 0)),
                  pl.BlockSpec((1, PAIR), lambda b, p, i: (0, 0))],
        out_specs=[pl.BlockSpec((t, PAIR), lambda b, p, i: (b * nq + i, p)),
                   pl.BlockSpec((1, PAIR, seq), lambda b, p, i: (b, p, 0)),
                   pl.BlockSpec((1, PAIR, seq), lambda b, p, i: (b, p, 0))],
        out_shape=[jax.ShapeDtypeStruct((n, SEG), BF16),
                   jax.ShapeDtypeStruct((batch, SEG, seq), F32),
                   jax.ShapeDtypeStruct((batch, SEG, seq), F32)],
        scratch_shapes=[pltpu.VMEM((seq + ATT_PAD_BLOCKS * t, 2 * PAIR), BF16),
                        pltpu.VMEM((seq + ATT_PAD_BLOCKS * t, PAIR), BF16),
                        pltpu.VMEM((2 * t, 2 * t), F32),
                        pltpu.VMEM((2 * t, 2 * t), F32),
                        pltpu.VMEM((2 * t, 2 * t), F32),
                        pltpu.VMEM((2 * t, PAIR), F32),
                        pltpu.VMEM((2 * t, PAIR), F32),
                        pltpu.VMEM((2 * t, PAIR), F32)],
        compiler_params=_cparams(3),
        name="fox_prompt",
    )(z3, z3, z3, z3, fcum, fcum, gain2(q_gain), gain2(k_gain))


CONV_HALO = 32
CONV_ROWS = 32


def _conv_prompt_kernel(ua_ref, ub_ref, gd_ref, uah_ref, ubh_ref, st_ref, w_ref, cb_ref, cg_ref,
                        d_ref, cs_ref, buf_ref, y_ref):
    ti = pl.program_id(1)
    nt = pl.num_programs(1)
    t = ua_ref.shape[1]
    pad = CONV_HALO - (CONV_WIDTH - 1)
    buf_ref[CONV_HALO:CONV_HALO + t, :] = ua_ref[0] * jax.nn.sigmoid(ub_ref[0])

    @pl.when(ti == 0)
    def _():
        buf_ref[0:CONV_HALO, :] = st_ref[0]

    @pl.when(ti > 0)
    def _():
        buf_ref[0:CONV_HALO, :] = uah_ref[0] * jax.nn.sigmoid(ubh_ref[0])

    for lg in range(SEG // PAIR):
        ls = slice(lg * PAIR, (lg + 1) * PAIR)
        for rc in range(t // CONV_ROWS):
            r0 = rc * CONV_ROWS
            acc = jnp.zeros((CONV_ROWS, PAIR), F32) + cb_ref[:, ls]
            for k in range(CONV_WIDTH):
                acc = acc + w_ref[k:k + 1, ls] * buf_ref[r0 + pad + k:r0 + pad + k + CONV_ROWS, ls]
            y_ref[r0:r0 + CONV_ROWS, ls] = acc

    y = y_ref[...]
    mu = jnp.mean(y, axis=-1, keepdims=True)
    yc = y - mu
    var = jnp.mean(yc * yc, axis=-1, keepdims=True)
    yn = yc * lax.rsqrt(var + EPS) * cg_ref[...]
    d_ref[...] = (_silu(yn) * _silu(gd_ref[0])).astype(BF16)

    @pl.when(ti == nt - 1)
    def _():
        cs_ref[0] = buf_ref[t + pad:t + CONV_HALO, :]


def _conv_prompt(z3, batch, seq, state, conv_w, conv_b, conv_gain):
    n = batch * seq
    t = min(seq, 256)
    nt = seq // t
    hb = t // CONV_HALO
    st = jnp.pad(state.astype(F32), ((0, 0), (CONV_HALO - (CONV_WIDTH - 1), 0), (0, 0)))
    tok = lambda s: pl.BlockSpec((1, t, SEG), lambda b, i, s=s: (s, b * nt + i, 0))
    halo = lambda s: pl.BlockSpec((1, CONV_HALO, SEG),
                                  lambda b, i, s=s: (s, jnp.maximum((b * nt + i) * hb - 1, 0), 0))
    row = pl.BlockSpec((1, SEG), lambda b, i: (0, 0))
    return pl.pallas_call(
        _conv_prompt_kernel,
        grid=(batch, nt),
        in_specs=[tok(4), tok(5), tok(6), halo(4), halo(5),
                  pl.BlockSpec((1, CONV_HALO, SEG), lambda b, i: (b, 0, 0)),
                  pl.BlockSpec((CONV_WIDTH, SEG), lambda b, i: (0, 0)), row, row],
        out_specs=[pl.BlockSpec((t, SEG), lambda b, i: (b * nt + i, 0)),
                   pl.BlockSpec((1, CONV_WIDTH - 1, SEG), lambda b, i: (b, 0, 0))],
        out_shape=[jax.ShapeDtypeStruct((n, SEG), BF16),
                   jax.ShapeDtypeStruct((batch, CONV_WIDTH - 1, SEG), F32)],
        scratch_shapes=[pltpu.VMEM((t + CONV_HALO, SEG), F32),
                        pltpu.VMEM((t, SEG), F32)],
        compiler_params=_cparams(2),
        name="conv_prompt",
    )(z3, z3, z3, z3, z3, st, conv_w.astype(F32), conv_b.astype(F32).reshape(1, SEG),
      conv_gain.astype(F32).reshape(1, SEG))


def _odd_weights(w_in, w_out):
    d = w_in.shape[0]
    w = SEG
    fz = w_in[:, 4 * w:4 * w + N_HEADS]
    rest = jnp.concatenate([w_in[:, :4 * w], w_in[:, 4 * w + N_HEADS:]], axis=1)
    fz_pad = jnp.pad(fz, ((0, 0), (0, w - N_HEADS)))
    w3 = jnp.concatenate([rest, fz_pad], axis=1).reshape(d, 8, w).transpose(1, 0, 2).astype(BF16)
    w2 = w_out.reshape(2, w, d).astype(BF16)
    return w3, w2


def _odd_layer_prompt(x, norm_g, w_in, b_forget, q_gain, k_gain, conv_w, conv_b, conv_gain, w_out):
    b, l, d = x.shape
    x2d = x.reshape(b * l, d)
    w3, w2 = _odd_weights(w_in, w_out)
    z3 = _proj_in(x2d, norm_g, w3)
    logf, fcum = _logf_prompt(z3, b, l, b_forget)
    oc, kt, vt = _fox_prompt(z3, b, l, fcum, q_gain, k_gain)
    dd, cs = _conv_prompt(z3, b, l, jnp.zeros((b, CONV_WIDTH - 1, SEG), F32), conv_w, conv_b, conv_gain)
    y = _proj_out(oc, dd, x2d, w2)
    return (y.reshape(b, l, d), _heads_from_token_minor(kt, b, l), _heads_from_token_minor(vt, b, l), logf, cs)


def _heads_rmsnorm(x, gain):
    lo = _lo_mask()
    parts = [_pair_rmsnorm(x[:, g * PAIR:(g + 1) * PAIR], gain[:, g * PAIR:(g + 1) * PAIR], lo)
             for g in range(SEG // PAIR)]
    return jnp.concatenate(parts, axis=-1)


def _query_rows(qn, n_tok):
    sub = lax.broadcasted_iota(jnp.int32, (N_HEADS, SEG), 0)
    head_of_lane = lax.broadcasted_iota(jnp.int32, (N_HEADS, SEG), 1) // HEAD_DIM
    own = sub == head_of_lane
    return jnp.concatenate([jnp.where(own, jnp.broadcast_to(qn[q:q + 1, :], (N_HEADS, SEG)), 0.0)
                            for q in range(n_tok)], axis=0)


def _rows_to_tokens(rows_out, n_tok):
    sub = lax.broadcasted_iota(jnp.int32, (N_HEADS, SEG), 0)
    head_of_lane = lax.broadcasted_iota(jnp.int32, (N_HEADS, SEG), 1) // HEAD_DIM
    own = sub == head_of_lane
    tok = lax.broadcasted_iota(jnp.int32, (n_tok, SEG), 0)
    out = jnp.zeros((n_tok, SEG), F32)
    for q in range(n_tok):
        o_q = jnp.sum(jnp.where(own, rows_out[q * N_HEADS:(q + 1) * N_HEADS, :], 0.0), axis=0, keepdims=True)
        out = jnp.where(tok == q, o_q, out)
    return out


def _t5_bias_rows(rel, tab):
    bias = jnp.broadcast_to(tab[:, T5_BUCKETS - 1:T5_BUCKETS], rel.shape)
    for b in range(T5_BUCKETS - 2, -1, -1):
        bias = jnp.where(rel <= T5_UPPER[b], tab[:, b:b + 1], bias)
    return bias


def _finish_rows(s_past, s_new, v_pages, v_new, n_tok):
    m = jnp.max(s_past, axis=-1, keepdims=True)
    for col in s_new:
        m = jnp.maximum(m, col)
    p_past = jnp.exp(s_past - m)
    l = jnp.sum(p_past, axis=-1, keepdims=True)
    pb = p_past.astype(BF16)
    acc = jnp.zeros((s_past.shape[0], SEG), F32)
    for pg, v_ref in enumerate(v_pages):
        acc = acc + _nt_dot(pb[:, pg * PAGE_SIZE:(pg + 1) * PAGE_SIZE], v_ref[0].astype(BF16))
    for kj, col in enumerate(s_new):
        p_new = jnp.exp(col - m)
        l = l + p_new
        acc = acc + p_new * v_new[kj:kj + 1, :]
    return _rows_to_tokens(acc / l, n_tok)


SAMPLE_SEQS_PER_STEP = 1


def _moba_sample_kernel(pt_ref, tab_ref, q_ref, k_ref, v_ref, g_ref, qg_ref, kg_ref, *rest, n_pages, n_tok, seqs):
    o_ref, ka_ref = rest[2 * seqs * n_pages:]
    for r in range(seqs):
        _moba_sample_one(r, tab_ref, q_ref, k_ref, v_ref, g_ref, qg_ref, kg_ref,
                         rest[r * n_pages:(r + 1) * n_pages],
                         rest[(seqs + r) * n_pages:(seqs + r + 1) * n_pages], o_ref, ka_ref, n_pages, n_tok)


def _moba_sample_one(r, tab_ref, q_ref, k_ref, v_ref, g_ref, qg_ref, kg_ref, k_pages, v_pages, o_ref, ka_ref,
                     n_pages, n_tok):
    pages_per_block = MOBA_BLOCK // PAGE_SIZE
    nb_past = n_pages // pages_per_block
    past_len = n_pages * PAGE_SIZE
    n_rows = n_tok * N_HEADS

    qn = _heads_rmsnorm(q_ref[0, r], qg_ref[...])
    kn = _heads_rmsnorm(k_ref[0, r], kg_ref[...])
    ka_ref[r] = kn
    v_new = v_ref[0, r]
    qrows = _query_rows(qn, n_tok)
    qb = (qrows * (HEAD_DIM ** -0.5)).astype(BF16)
    tab = tab_ref[...]
    row_q = lax.broadcasted_iota(jnp.int32, (n_rows, 1), 0) // N_HEADS

    raw = [jnp.dot(qb, k_pages[pg][0].astype(BF16), preferred_element_type=F32) for pg in range(n_pages)]
    blk_lane = lax.broadcasted_iota(jnp.int32, (SEG, nb_past), 1)
    k_mean = jnp.zeros((SEG, nb_past), F32)
    for n in range(nb_past):
        tot = k_pages[n * pages_per_block][0]
        for pg in range(n * pages_per_block + 1, (n + 1) * pages_per_block):
            tot = tot + k_pages[pg][0]
        k_mean = jnp.where(blk_lane == n, jnp.sum(tot, axis=-1, keepdims=True) * (1.0 / MOBA_BLOCK), k_mean)
    gate = jnp.dot(qrows, k_mean, precision=HIGHEST, preferred_element_type=F32)
    selpen = jnp.where(_top_blocks(gate, nb_past, nb_past, nb_past), 0.0, NEG)

    far = tab[:, T5_BUCKETS - 1:T5_BUCKETS]
    lane = lax.broadcasted_iota(jnp.int32, (n_rows, PAGE_SIZE), 1)
    tiles = []
    for pg in range(n_pages):
        n = pg // pages_per_block
        min_rel = past_len - (pg + 1) * PAGE_SIZE + 1
        if min_rel > T5_UPPER[-1]:
            bias = far
        else:
            bias = _t5_bias_rows(past_len + row_q - (pg * PAGE_SIZE + lane), tab)
        tiles.append(raw[pg] + bias + selpen[:, n:n + 1])
    s_past = jnp.concatenate(tiles, axis=-1)

    s_new = []
    for kj in range(n_tok):
        dot = jnp.sum(qrows * kn[kj:kj + 1, :], axis=-1, keepdims=True) * (HEAD_DIM ** -0.5)
        rel = row_q - kj
        s_new.append(jnp.where(rel >= 0, dot + _t5_bias_rows(rel, tab), NEG))

    o = _finish_rows(s_past, s_new, v_pages, v_new, n_tok)
    o_ref[r] = (o * _silu(g_ref[0, r])).astype(BF16)


def _page_specs(n_pages, rows, seqs):
    return [pl.BlockSpec((1, rows, PAGE_SIZE), lambda b, pt, r=r, pg=pg: (pt[b * seqs + r, pg], 0, 0))
            for r in range(seqs) for pg in range(n_pages)]


def _pages_token_minor(cache):
    n_pool = cache.shape[0]
    return cache.transpose(0, 2, 3, 1).reshape(n_pool, SEG, PAGE_SIZE)


def _moba_sample(z3, n_seq, n_tok, cache_k, cache_v, page_table, t5_table, q_gain, k_gain):
    n_pages = page_table.shape[1]
    n_pool = cache_k.shape[0]
    assert MOBA_BLOCK % PAGE_SIZE == 0 and (n_pages * PAGE_SIZE) % MOBA_BLOCK == 0
    assert n_tok <= MOBA_BLOCK and n_pages // (MOBA_BLOCK // PAGE_SIZE) >= MOBA_TOPK
    seqs = math.gcd(n_seq, SAMPLE_SEQS_PER_STEP)
    z4 = z3.reshape(z3.shape[0], n_seq, n_tok, SEG)
    tok = lambda s: pl.BlockSpec((1, seqs, n_tok, SEG), lambda b, pt, s=s: (s, b, 0, 0))
    row = pl.BlockSpec((1, SEG), lambda b, pt: (0, 0))
    gain8 = lambda g: jnp.tile(g.astype(F32), N_HEADS).reshape(1, SEG)
    tab_rows = jnp.tile(t5_table.astype(F32).T, (n_tok, 1))
    grid_spec = pltpu.PrefetchScalarGridSpec(
        num_scalar_prefetch=1,
        grid=(n_seq // seqs,),
        in_specs=[pl.BlockSpec((n_tok * N_HEADS, T5_BUCKETS), lambda b, pt: (0, 0)),
                  tok(0), tok(1), tok(2), tok(3), row, row]
                 + _page_specs(n_pages, SEG, seqs) + _page_specs(n_pages, SEG, seqs),
        out_specs=[pl.BlockSpec((seqs, n_tok, SEG), lambda b, pt: (b, 0, 0)),
                   pl.BlockSpec((seqs, n_tok, SEG), lambda b, pt: (b, 0, 0))])
    ck = _pages_token_minor(cache_k)
    cv = _pages_token_minor(cache_v)
    return pl.pallas_call(
        functools.partial(_moba_sample_kernel, n_pages=n_pages, n_tok=n_tok, seqs=seqs),
        grid_spec=grid_spec,
        out_shape=[jax.ShapeDtypeStruct((n_seq, n_tok, SEG), BF16),
                   jax.ShapeDtypeStruct((n_seq, n_tok, SEG), F32)],
        compiler_params=_cparams(1),
        name="moba_sample",
    )(page_table, tab_rows, z4, z4, z4, z4, gain8(q_gain), gain8(k_gain),
      *([ck] * (seqs * n_pages)), *([cv] * (seqs * n_pages)))


def _fox_sample_kernel(pt_ref, q_ref, k_ref, v_ref, g_ref, fz_ref, fzr_ref, bf_ref, bfr_ref, qg_ref, kg_ref,
                       *rest, n_pages, n_tok, seqs):
    o_ref, kc_ref, lf_ref = rest[3 * seqs * n_pages:]
    for r in range(seqs):
        pages = [rest[(part * seqs + r) * n_pages:(part * seqs + r + 1) * n_pages] for part in range(3)]
        _fox_sample_one(r, q_ref, k_ref, v_ref, g_ref, fz_ref, fzr_ref, bf_ref, bfr_ref, qg_ref, kg_ref,
                        pages[0], pages[1], pages[2], o_ref, kc_ref, lf_ref, n_pages, n_tok)


def _fox_sample_one(r, q_ref, k_ref, v_ref, g_ref, fz_ref, fzr_ref, bf_ref, bfr_ref, qg_ref, kg_ref,
                    k_pages, v_pages, f_pages, o_ref, kc_ref, lf_ref, n_pages, n_tok):
    n_rows = n_tok * N_HEADS

    qn = _heads_rmsnorm(q_ref[0, r], qg_ref[...])
    kn = _heads_rmsnorm(k_ref[0, r], kg_ref[...])
    kc_ref[r] = kn
    v_new = v_ref[0, r]
    lf_ref[r] = _log_sigmoid(fz_ref[0, r][:, :N_HEADS] + bf_ref[...])
    qrows = _query_rows(qn, n_tok) * (HEAD_DIM ** -0.5)
    qb = qrows.astype(BF16)
    row_q = lax.broadcasted_iota(jnp.int32, (n_rows, 1), 0) // N_HEADS

    upper = (lax.broadcasted_iota(jnp.int32, (PAGE_SIZE, PAGE_SIZE), 0)
             <= lax.broadcasted_iota(jnp.int32, (PAGE_SIZE, PAGE_SIZE), 1)).astype(F32)
    carry = jnp.zeros((n_rows, 1), F32)
    fk_tiles = []
    for pg in range(n_pages):
        rows_lf = jnp.concatenate([f_pages[pg][0]] * n_tok, axis=0)
        cs = jnp.dot(rows_lf, upper, precision=HIGHEST, preferred_element_type=F32) + carry
        fk_tiles.append(cs)
        carry = cs[:, PAGE_SIZE - 1:PAGE_SIZE]
    lfr = _log_sigmoid(fzr_ref[r] + bfr_ref[...])
    cum_new = []
    run = carry
    for tk in range(n_tok):
        run = run + lfr[:, tk:tk + 1]
        cum_new.append(run)
    fq = cum_new[n_tok - 1]
    for tk in range(n_tok - 2, -1, -1):
        fq = jnp.where(row_q == tk, cum_new[tk], fq)

    tiles = [jnp.dot(qb, k_pages[pg][0].astype(BF16), preferred_element_type=F32) + (fq - fk_tiles[pg])
             for pg in range(n_pages)]
    s_past = jnp.concatenate(tiles, axis=-1)
    s_new = []
    for kj in range(n_tok):
        dot = jnp.sum(qrows * kn[kj:kj + 1, :], axis=-1, keepdims=True)
        s_new.append(jnp.where(row_q >= kj, dot + (fq - cum_new[kj]), NEG))

    o = _finish_rows(s_past, s_new, v_pages, v_new, n_tok)
    o_ref[r] = (o * _silu(g_ref[0, r])).astype(BF16)


def _fox_sample(z3, n_seq, n_tok, cache_k, cache_v, cache_logf, page_table, b_forget, q_gain, k_gain):
    n_pages = page_table.shape[1]
    n_pool = cache_k.shape[0]
    seqs = math.gcd(n_seq, SAMPLE_SEQS_PER_STEP)
    z4 = z3.reshape(z3.shape[0], n_seq, n_tok, SEG)
    tok = lambda s: pl.BlockSpec((1, seqs, n_tok, SEG), lambda b, pt, s=s: (s, b, 0, 0))
    row = pl.BlockSpec((1, SEG), lambda b, pt: (0, 0))
    gain8 = lambda g: jnp.tile(g.astype(F32), N_HEADS).reshape(1, SEG)
    fz = z4[7, :, :, :N_HEADS]
    fz_rows = jnp.tile(fz.transpose(0, 2, 1), (1, n_tok, 1))
    bf = b_forget.astype(F32)
    grid_spec = pltpu.PrefetchScalarGridSpec(
        num_scalar_prefetch=1,
        grid=(n_seq // seqs,),
        in_specs=[tok(0), tok(1), tok(2), tok(3),
                  pl.BlockSpec((1, seqs, n_tok, PAIR), lambda b, pt: (7, b, 0, 0)),
                  pl.BlockSpec((seqs, n_tok * N_HEADS, n_tok), lambda b, pt: (b, 0, 0)),
                  pl.BlockSpec((1, N_HEADS), lambda b, pt: (0, 0)),
                  pl.BlockSpec((n_tok * N_HEADS, 1), lambda b, pt: (0, 0)),
                  row, row]
                 + _page_specs(n_pages, SEG, seqs) + _page_specs(n_pages, SEG, seqs)
                 + _page_specs(n_pages, N_HEADS, seqs),
        out_specs=[pl.BlockSpec((seqs, n_tok, SEG), lambda b, pt: (b, 0, 0)),
                   pl.BlockSpec((seqs, n_tok, SEG), lambda b, pt: (b, 0, 0)),
                   pl.BlockSpec((seqs, n_tok, N_HEADS), lambda b, pt: (b, 0, 0))])
    ck = _pages_token_minor(cache_k)
    cv = _pages_token_minor(cache_v)
    cf = cache_logf.astype(F32).transpose(0, 2, 1)
    return pl.pallas_call(
        functools.partial(_fox_sample_kernel, n_pages=n_pages, n_tok=n_tok, seqs=seqs),
        grid_spec=grid_spec,
        out_shape=[jax.ShapeDtypeStruct((n_seq, n_tok, SEG), BF16),
                   jax.ShapeDtypeStruct((n_seq, n_tok, SEG), F32),
                   jax.ShapeDtypeStruct((n_seq, n_tok, N_HEADS), F32)],
        compiler_params=_cparams(1),
        name="fox_sample",
    )(page_table, z4, z4, z4, z4, z4, fz_rows, bf.reshape(1, N_HEADS),
      jnp.tile(bf, n_tok).reshape(n_tok * N_HEADS, 1), gain8(q_gain), gain8(k_gain),
      *([ck] * (seqs * n_pages)), *([cv] * (seqs * n_pages)), *([cf] * (seqs * n_pages)))


FLAT = PAGE_SIZE * N_HEADS


def _flat_rmsnorm(x, gain):
    return x * lax.rsqrt(jnp.mean(x * x, axis=-1, keepdims=True) + EPS) * gain


def _flat_softmax_pv(tiles, v_pages, s_new, v_new):
    m = jnp.max(s_new, axis=-1, keepdims=True)
    for s in tiles:
        m = jnp.maximum(m, jnp.max(s, axis=-1, keepdims=True))
    p_new = jnp.exp(s_new - m)
    l = jnp.sum(p_new, axis=-1, keepdims=True)
    acc = jnp.dot(p_new, v_new, precision=HIGHEST, preferred_element_type=F32)
    for s, v_ref in zip(tiles, v_pages):
        pr = jnp.exp(s - m)
        l = l + jnp.sum(pr, axis=-1, keepdims=True)
        acc = acc + jnp.dot(pr.astype(BF16), v_ref[0].astype(BF16), preferred_element_type=F32)
    return acc / l


def _moba_flat_kernel(pt_ref, tab_ref, q_ref, k_ref, v_ref, g_ref, qg_ref, kg_ref, *rest, n_pages, n_tok):
    k_pages = rest[:n_pages]
    v_pages = rest[n_pages:2 * n_pages]
    o_ref, ka_ref = rest[2 * n_pages:]
    pages_per_block = MOBA_BLOCK // PAGE_SIZE
    nb_past = n_pages // pages_per_block
    past_len = n_pages * PAGE_SIZE
    n_rows = n_tok * N_HEADS
    scale = HEAD_DIM ** -0.5

    qn = _flat_rmsnorm(q_ref[0], qg_ref[...])
    kn = _flat_rmsnorm(k_ref[0], kg_ref[...])
    ka_ref[0] = kn
    tab = tab_ref[...]
    row = lax.broadcasted_iota(jnp.int32, (n_rows, 1), 0)
    row_q, row_h = row // N_HEADS, row % N_HEADS

    blk_lane = lax.broadcasted_iota(jnp.int32, (n_rows, nb_past), 1)
    gate = jnp.zeros((n_rows, nb_past), F32)
    for n in range(nb_past):
        tot = jnp.zeros((N_HEADS, HEAD_DIM), F32)
        for pg in range(n * pages_per_block, (n + 1) * pages_per_block):
            tot = tot + jnp.sum(k_pages[pg][0].reshape(PAGE_SIZE, N_HEADS, HEAD_DIM), axis=0)
        k_mean = jnp.concatenate([tot * (1.0 / MOBA_BLOCK)] * n_tok, axis=0)
        gate = jnp.where(blk_lane == n, jnp.sum(qn * k_mean, axis=-1, keepdims=True), gate)
    selpen = jnp.where(_top_blocks(gate, nb_past, nb_past, nb_past), 0.0, NEG)

    qb = (qn * scale).astype(BF16)
    col = lax.broadcasted_iota(jnp.int32, (n_rows, FLAT), 1)
    own_head = (col % N_HEADS) == row_h
    far = tab[:, T5_BUCKETS - 1:T5_BUCKETS]
    tiles = []
    for pg in range(n_pages):
        n = pg // pages_per_block
        s = _nt_dot(qb, k_pages[pg][0].astype(BF16))
        if past_len - (pg + 1) * PAGE_SIZE + 1 > T5_UPPER[-1]:
            bias = far
        else:
            bias = _t5_bias_rows(past_len + row_q - (pg * PAGE_SIZE + col // N_HEADS), tab)
        tiles.append(jnp.where(own_head, s + (bias + selpen[:, n:n + 1]), NEG))

    ncol = lax.broadcasted_iota(jnp.int32, (n_rows, n_rows), 1)
    rel = row_q - ncol // N_HEADS
    s_new = _nt_dot(qn * scale, kn, precision=HIGHEST) + _t5_bias_rows(rel, tab)
    s_new = jnp.where(((ncol % N_HEADS) == row_h) & (rel >= 0), s_new, NEG)

    o = _flat_softmax_pv(tiles, v_pages, s_new, v_ref[0])
    o_ref[0] = (o * _silu(g_ref[0])).astype(BF16)


def _flat_page_specs(n_pages, rows, width):
    return [pl.BlockSpec((1, rows, width), lambda b, pt, pg=pg: (pt[b, pg], 0, 0)) for pg in range(n_pages)]


def _flat_tokens(z3, seg, n_seq, n_tok):
    return z3[seg].reshape(n_seq, n_tok * N_HEADS, HEAD_DIM)


def _moba_flat(z3, n_seq, n_tok, cache_k, cache_v, page_table, t5_table, q_gain, k_gain):
    n_pages = page_table.shape[1]
    n_pool = cache_k.shape[0]
    assert MOBA_BLOCK % PAGE_SIZE == 0 and (n_pages * PAGE_SIZE) % MOBA_BLOCK == 0
    assert n_tok <= MOBA_BLOCK and n_pages // (MOBA_BLOCK // PAGE_SIZE) >= MOBA_TOPK
    n_rows = n_tok * N_HEADS
    tok = pl.BlockSpec((1, n_rows, HEAD_DIM), lambda b, pt: (b, 0, 0))
    gain = pl.BlockSpec((1, HEAD_DIM), lambda b, pt: (0, 0))
    tab_rows = jnp.tile(t5_table.astype(F32).T, (n_tok, 1))
    grid_spec = pltpu.PrefetchScalarGridSpec(
        num_scalar_prefetch=1,
        grid=(n_seq,),
        in_specs=[pl.BlockSpec((n_rows, T5_BUCKETS), lambda b, pt: (0, 0)), tok, tok, tok, tok, gain, gain]
                 + _flat_page_specs(n_pages, FLAT, HEAD_DIM) + _flat_page_specs(n_pages, FLAT, HEAD_DIM),
        out_specs=[tok, tok])
    ck = cache_k.reshape(n_pool, FLAT, HEAD_DIM)
    cv = cache_v.reshape(n_pool, FLAT, HEAD_DIM)
    flat = lambda seg: _flat_tokens(z3, seg, n_seq, n_tok)
    return pl.pallas_call(
        functools.partial(_moba_flat_kernel, n_pages=n_pages, n_tok=n_tok),
        grid_spec=grid_spec,
        out_shape=[jax.ShapeDtypeStruct((n_seq, n_rows, HEAD_DIM), BF16),
                   jax.ShapeDtypeStruct((n_seq, n_rows, HEAD_DIM), F32)],
        compiler_params=_cparams(1),
        name="moba_sample",
    )(page_table, tab_rows, flat(0), flat(1), flat(2), flat(3),
      q_gain.astype(F32).reshape(1, HEAD_DIM), k_gain.astype(F32).reshape(1, HEAD_DIM),
      *([ck] * n_pages), *([cv] * n_pages))


LOGF_CHUNK = PAIR // N_HEADS


def _fox_flat_kernel(pt_ref, q_ref, k_ref, v_ref, g_ref, fz_ref, fzr_ref, bf_ref, bfr_ref, qg_ref, kg_ref,
                     *rest, n_pages, n_tok):
    k_pages = rest[:n_pages]
    v_pages = rest[n_pages:2 * n_pages]
    f_pages = rest[2 * n_pages:3 * n_pages]
    o_ref, kc_ref, lf_ref = rest[3 * n_pages:]
    n_rows = n_tok * N_HEADS
    chunks = PAGE_SIZE // LOGF_CHUNK
    scale = HEAD_DIM ** -0.5

    qn = _flat_rmsnorm(q_ref[0], qg_ref[...])
    kn = _flat_rmsnorm(k_ref[0], kg_ref[...])
    kc_ref[0] = kn
    lf_ref[0] = _log_sigmoid(fz_ref[0] + bf_ref[...])
    row = lax.broadcasted_iota(jnp.int32, (n_rows, 1), 0)
    row_q, row_h = row // N_HEADS, row % N_HEADS

    lf = jnp.concatenate([f_pages[pg][0] for pg in range(n_pages)], axis=0)
    nr = lf.shape[0]
    li = lax.broadcasted_iota(jnp.int32, (PAIR, PAIR), 0)
    lj = lax.broadcasted_iota(jnp.int32, (PAIR, PAIR), 1)
    scan = ((li % N_HEADS == lj % N_HEADS) & (li <= lj)).astype(F32)
    last = (li == PAIR - N_HEADS + lj % N_HEADS).astype(F32)
    ri = lax.broadcasted_iota(jnp.int32, (nr, nr), 0)
    rj = lax.broadcasted_iota(jnp.int32, (nr, nr), 1)
    before = (rj < ri).astype(F32)
    in_row = jnp.dot(lf, scan, precision=HIGHEST, preferred_element_type=F32)
    totals = jnp.dot(in_row, last, precision=HIGHEST, preferred_element_type=F32)
    fk = in_row + jnp.dot(before, totals, precision=HIGHEST, preferred_element_type=F32)

    lane = lax.broadcasted_iota(jnp.int32, (n_rows, PAIR), 1)
    past_total = jnp.sum(jnp.where(lane == PAIR - N_HEADS + row_h, fk[nr - 1:nr, :], 0.0),
                         axis=-1, keepdims=True)
    lfr = _log_sigmoid(fzr_ref[0] + bfr_ref[...])
    cum_new = []
    run = past_total
    for tk in range(n_tok):
        run = run + lfr[:, tk:tk + 1]
        cum_new.append(run)
    fq = cum_new[n_tok - 1]
    for tk in range(n_tok - 2, -1, -1):
        fq = jnp.where(row_q == tk, cum_new[tk], fq)

    qb = (qn * scale).astype(BF16)
    col = lax.broadcasted_iota(jnp.int32, (n_rows, FLAT), 1)
    own_head = (col % N_HEADS) == row_h
    tiles = []
    for pg in range(n_pages):
        s = _nt_dot(qb, k_pages[pg][0].astype(BF16))
        fk_row = jnp.concatenate([fk[pg * chunks + c:pg * chunks + c + 1, :] for c in range(chunks)], axis=1)
        tiles.append(jnp.where(own_head, s + (fq - fk_row), NEG))

    ncol = lax.broadcasted_iota(jnp.int32, (n_rows, n_rows), 1)
    fk_new = cum_new[n_tok - 1]
    for tk in range(n_tok - 2, -1, -1):
        fk_new = jnp.where(ncol // N_HEADS == tk, cum_new[tk], fk_new)
    s_new = _nt_dot(qn * scale, kn, precision=HIGHEST) + (fq - fk_new)
    s_new = jnp.where(((ncol % N_HEADS) == row_h) & (ncol // N_HEADS <= row_q), s_new, NEG)

    o = _flat_softmax_pv(tiles, v_pages, s_new, v_ref[0])
    o_ref[0] = (o * _silu(g_ref[0])).astype(BF16)


def _fox_flat(z3, n_seq, n_tok, cache_k, cache_v, cache_logf, page_table, b_forget, q_gain, k_gain):
    n_pages = page_table.shape[1]
    n_pool = cache_k.shape[0]
    n_rows = n_tok * N_HEADS
    tok = pl.BlockSpec((1, n_rows, HEAD_DIM), lambda b, pt: (b, 0, 0))
    gain = pl.BlockSpec((1, HEAD_DIM), lambda b, pt: (0, 0))
    fz = z3[7].reshape(n_seq, n_tok, SEG)[:, :, :N_HEADS]
    fz_rows = jnp.tile(fz.transpose(0, 2, 1), (1, n_tok, 1))
    bf = b_forget.astype(F32)
    grid_spec = pltpu.PrefetchScalarGridSpec(
        num_scalar_prefetch=1,
        grid=(n_seq,),
        in_specs=[tok, tok, tok, tok,
                  pl.BlockSpec((1, n_tok, N_HEADS), lambda b, pt: (b, 0, 0)),
                  pl.BlockSpec((1, n_rows, n_tok), lambda b, pt: (b, 0, 0)),
                  pl.BlockSpec((1, N_HEADS), lambda b, pt: (0, 0)),
                  pl.BlockSpec((n_rows, 1), lambda b, pt: (0, 0)),
                  gain, gain]
                 + _flat_page_specs(n_pages, FLAT, HEAD_DIM) + _flat_page_specs(n_pages, FLAT, HEAD_DIM)
                 + _flat_page_specs(n_pages, PAGE_SIZE // LOGF_CHUNK, PAIR),
        out_specs=[tok, tok, pl.BlockSpec((1, n_tok, N_HEADS), lambda b, pt: (b, 0, 0))])
    ck = cache_k.reshape(n_pool, FLAT, HEAD_DIM)
    cv = cache_v.reshape(n_pool, FLAT, HEAD_DIM)
    cf = cache_logf.astype(F32).reshape(n_pool, PAGE_SIZE // LOGF_CHUNK, PAIR)
    flat = lambda seg: _flat_tokens(z3, seg, n_seq, n_tok)
    return pl.pallas_call(
        functools.partial(_fox_flat_kernel, n_pages=n_pages, n_tok=n_tok),
        grid_spec=grid_spec,
        out_shape=[jax.ShapeDtypeStruct((n_seq, n_rows, HEAD_DIM), BF16),
                   jax.ShapeDtypeStruct((n_seq, n_rows, HEAD_DIM), F32),
                   jax.ShapeDtypeStruct((n_seq, n_tok, N_HEADS), F32)],
        compiler_params=_cparams(1),
        name="fox_sample",
    )(page_table, flat(0), flat(1), flat(2), flat(3), fz, fz_rows, bf.reshape(1, N_HEADS),
      jnp.tile(bf, n_tok).reshape(n_rows, 1),
      q_gain.astype(F32).reshape(1, HEAD_DIM), k_gain.astype(F32).reshape(1, HEAD_DIM),
      *([ck] * n_pages), *([cv] * n_pages), *([cf] * n_pages))


CONV_SAMPLE_BATCH = 8


def _conv_sample_kernel(ua_ref, ub_ref, gd_ref, st_ref, w_ref, cb_ref, cg_ref, d_ref, cs_ref, buf_ref):
    n_tok = ua_ref.shape[2]
    hist = CONV_WIDTH - 1
    for e in range(ua_ref.shape[1]):
        buf_ref[0:hist, :] = st_ref[e]
        buf_ref[hist:hist + n_tok, :] = ua_ref[0, e] * jax.nn.sigmoid(ub_ref[0, e])
        y = jnp.zeros((n_tok, SEG), F32) + cb_ref[...]
        for k in range(CONV_WIDTH):
            y = y + w_ref[k:k + 1, :] * buf_ref[k:k + n_tok, :]
        mu = jnp.mean(y, axis=-1, keepdims=True)
        yc = y - mu
        var = jnp.mean(yc * yc, axis=-1, keepdims=True)
        yn = yc * lax.rsqrt(var + EPS) * cg_ref[...]
        d_ref[e] = (_silu(yn) * _silu(gd_ref[0, e])).astype(BF16)
        cs_ref[e] = buf_ref[n_tok:n_tok + hist, :]


def _conv_sample(z3, n_seq, n_tok, state, conv_w, conv_b, conv_gain):
    bt = math.gcd(n_seq, CONV_SAMPLE_BATCH)
    hist = CONV_WIDTH - 1
    z4 = z3.reshape(z3.shape[0], n_seq, n_tok, SEG)
    tok = lambda s: pl.BlockSpec((1, bt, n_tok, SEG), lambda b, s=s: (s, b, 0, 0))
    row = pl.BlockSpec((1, SEG), lambda b: (0, 0))
    return pl.pallas_call(
        _conv_sample_kernel,
        grid=(n_seq // bt,),
        in_specs=[tok(4), tok(5), tok(6),
                  pl.BlockSpec((bt, hist, SEG), lambda b: (b, 0, 0)),
                  pl.BlockSpec((CONV_WIDTH, SEG), lambda b: (0, 0)), row, row],
        out_specs=[pl.BlockSpec((bt, n_tok, SEG), lambda b: (b, 0, 0)),
                   pl.BlockSpec((bt, hist, SEG), lambda b: (b, 0, 0))],
        out_shape=[jax.ShapeDtypeStruct((n_seq, n_tok, SEG), BF16),
                   jax.ShapeDtypeStruct((n_seq, hist, SEG), F32)],
        scratch_shapes=[pltpu.VMEM((hist + n_tok + 6, SEG), F32)],
        compiler_params=_cparams(1),
        name="conv_sample",
    )(z4, z4, z4, state.astype(F32), conv_w.astype(F32), conv_b.astype(F32).reshape(1, SEG),
      conv_gain.astype(F32).reshape(1, SEG))


RET_SAMPLE_GROUPS = 8


def _even_layer_sample(x, cache_k, cache_v, ret_state, page_table, norm_g, w_in, q_gain, k_gain, t5_table,
                       ret_gain, w_out):
    b, l, d = x.shape
    x2d = x.reshape(b * l, d)
    w3, w2 = _even_weights(w_in, w_out)
    z3 = _proj_in(x2d, norm_g, w3)
    oa, ka = _moba_sample(z3, b, l, cache_k, cache_v, page_table, t5_table, q_gain, k_gain)
    past_len = page_table.shape[1] * PAGE_SIZE
    groups = math.gcd(b, RET_SAMPLE_GROUPS)
    orr, sp = _retention(z3, b, l, l * groups, groups, past_len, ret_gain, _pair_blockdiag(ret_state), F32,
                         "retention_sample")
    y = _proj_out(oa.reshape(b * l, SEG), orr, x2d, w2)
    return (y.reshape(b, l, d), ka.reshape(b, l, N_HEADS, HEAD_DIM),
            z3[2].reshape(b, l, N_HEADS, HEAD_DIM), _pair_unblockdiag(sp))


def _odd_layer_sample(x, cache_k, cache_v, cache_logf, conv_state, page_table, norm_g, w_in, b_forget, q_gain,
                      k_gain, conv_w, conv_b, conv_gain, w_out):
    b, l, d = x.shape
    x2d = x.reshape(b * l, d)
    w3, w2 = _odd_weights(w_in, w_out)
    z3 = _proj_in(x2d, norm_g, w3)
    oc, kc, logf = _fox_sample(z3, b, l, cache_k, cache_v, cache_logf, page_table, b_forget, q_gain, k_gain)
    dd, cs = _conv_sample(z3, b, l, conv_state, conv_w, conv_b, conv_gain)
    y = _proj_out(oc.reshape(b * l, SEG), dd.reshape(b * l, SEG), x2d, w2)
    return (y.reshape(b, l, d), kc.reshape(b, l, N_HEADS, HEAD_DIM),
            z3[2].reshape(b, l, N_HEADS, HEAD_DIM), logf, cs)


def kernel(x_prompt, x_sample, cache_moba_k, cache_moba_v, cache_fox_k, cache_fox_v, cache_fox_logf, state_ret, state_conv, page_table, norm_g_even, w_in_even, moba_q_gain, moba_k_gain, t5_table, ret_gain, w_out_even, norm_g_odd, w_in_odd, b_forget, fox_q_gain, fox_k_gain, conv_w, conv_b, conv_gain, w_out_odd):
    depth = norm_g_even.shape[0] + norm_g_odd.shape[0]
    xp, xs = x_prompt, x_sample
    outs = {name: [] for name in ("mk_p", "mv_p", "mk_s", "mv_s", "rs_p", "rs_s",
                                  "fk_p", "fv_p", "fl_p", "fk_s", "fv_s", "fl_s", "cs_p", "cs_s")}
    for layer in range(depth):
        i = layer // 2
        if layer % 2 == 0:
            w = (norm_g_even[i], w_in_even[i], moba_q_gain[i], moba_k_gain[i], t5_table, ret_gain[i], w_out_even[i])
            xp, k1, v1, s1 = _even_layer_prompt(xp, *w)
            xs, k2, v2, s2 = _even_layer_sample(xs, cache_moba_k[i], cache_moba_v[i], state_ret[i], page_table, *w)
            for name, val in zip(("mk_p", "mv_p", "rs_p", "mk_s", "mv_s", "rs_s"), (k1, v1, s1, k2, v2, s2)):
                outs[name].append(val)
        else:
            w = (norm_g_odd[i], w_in_odd[i], b_forget[i], fox_q_gain[i], fox_k_gain[i],
                 conv_w[i], conv_b[i], conv_gain[i], w_out_odd[i])
            xp, k1, v1, f1, c1 = _odd_layer_prompt(xp, *w)
            xs, k2, v2, f2, c2 = _odd_layer_sample(xs, cache_fox_k[i], cache_fox_v[i], cache_fox_logf[i],
                                                   state_conv[i], page_table, *w)
            for name, val in zip(("fk_p", "fv_p", "fl_p", "cs_p", "fk_s", "fv_s", "fl_s", "cs_s"),
                                 (k1, v1, f1, c1, k2, v2, f2, c2)):
                outs[name].append(val)
    st = lambda name: jnp.stack(outs[name])
    return (xp, xs, st("mk_p"), st("mv_p"), st("mk_s"), st("mv_s"), st("rs_p"), st("rs_s"),
            st("fk_p"), st("fv_p"), st("fl_p"), st("fk_s"), st("fv_s"), st("fl_s"), st("cs_p"), st("cs_s"))
```

```python
import functools
import math

import numpy as np
import jax
import jax.numpy as jnp
from jax import lax
from jax.experimental import pallas as pl
from jax.experimental.pallas import tpu as pltpu

F32 = jnp.float32
BF16 = jnp.bfloat16
HIGHEST = lax.Precision.HIGHEST

HEAD_DIM = 64
PAIR = 2 * HEAD_DIM
SEG = 512
N_HEADS = SEG // HEAD_DIM
EPS = 1e-6
MOBA_BLOCK = 256
MOBA_TOPK = 3
T5_BUCKETS = 32
T5_MAX_DIST = 128
ROPE_BASE = 10000.0
CONV_WIDTH = 31
PAGE_SIZE = 128
NEG = -1e30
ATT_TILE = 256
V7X_VMEM_LIMIT = 48 * 1024 * 1024


def _t5_bucket_upper_bounds():
    max_exact = T5_BUCKETS // 2
    rel = np.arange(0, 4 * T5_MAX_DIST)
    relf = np.maximum(rel, 1).astype(np.float64)
    large = max_exact + np.trunc(np.log(relf / max_exact) / math.log(T5_MAX_DIST / max_exact)
                                 * (T5_BUCKETS - max_exact)).astype(np.int64)
    bucket = np.where(rel < max_exact, rel, np.minimum(large, T5_BUCKETS - 1))
    return tuple(int(rel[bucket <= b].max()) for b in range(T5_BUCKETS - 1))


T5_UPPER = _t5_bucket_upper_bounds()


def _cparams(n_axes):
    return pltpu.CompilerParams(dimension_semantics=("arbitrary",) * n_axes,
                                vmem_limit_bytes=V7X_VMEM_LIMIT)


def _silu(x):
    return x * jax.nn.sigmoid(x)


def _lo_mask():
    return lax.broadcasted_iota(jnp.int32, (1, PAIR), 1) < HEAD_DIM


def _pair_sum(x, lo):
    s0 = jnp.sum(jnp.where(lo, x, 0.0), axis=-1, keepdims=True)
    s1 = jnp.sum(jnp.where(lo, 0.0, x), axis=-1, keepdims=True)
    return jnp.where(lo, s0, s1)


def _pair_rmsnorm(x, gain, lo):
    ms = _pair_sum(x * x, lo) * (1.0 / HEAD_DIM)
    return x * lax.rsqrt(ms + EPS) * gain


def _t5_bias(rel, tab_ref, h):
    bias = jnp.full(rel.shape, tab_ref[T5_BUCKETS - 1, h], F32)
    for b in range(T5_BUCKETS - 2, -1, -1):
        bias = jnp.where(rel <= T5_UPPER[b], tab_ref[b, h], bias)
    return bias


def _top_blocks(gate, n_valid, own, width):
    lane = lax.broadcasted_iota(jnp.int32, gate.shape, 1)
    g = jnp.where(lane < n_valid, gate, -jnp.inf)
    sel = lane == own
    for _ in range(MOBA_TOPK):
        m = jnp.max(g, axis=-1, keepdims=True)
        idx = jnp.min(jnp.where(g == m, lane, width), axis=-1, keepdims=True)
        pick = (lane == idx) & (m > -jnp.inf)
        sel = sel | pick
        g = jnp.where(pick, -jnp.inf, g)
    return sel


def _nt_dot(a, b, precision=None):
    return lax.dot_general(a, b, (((1,), (1,)), ((), ())), precision=precision,
                           preferred_element_type=F32)


def _proj_in_kernel(x_ref, g_ref, w_ref, o_ref, xn_ref):
    j = pl.program_id(1)

    @pl.when(j == 0)
    def _():
        x = x_ref[...]
        ms = jnp.mean(x * x, axis=-1, keepdims=True)
        xn_ref[...] = (x * lax.rsqrt(ms + EPS) * g_ref[...]).astype(BF16)

    o_ref[0] = jnp.dot(xn_ref[...], w_ref[j], preferred_element_type=F32)


def _proj_in(x2d, gain, w3):
    n, d = x2d.shape
    nseg = w3.shape[0]
    tm = min(n, 1024)
    return pl.pallas_call(
        _proj_in_kernel,
        grid=(n // tm, nseg),
        in_specs=[pl.BlockSpec((tm, d), lambda i, j: (i, 0)),
                  pl.BlockSpec((1, d), lambda i, j: (0, 0)),
                  pl.BlockSpec((nseg, d, SEG), lambda i, j: (0, 0, 0))],
        out_specs=pl.BlockSpec((1, tm, SEG), lambda i, j: (j, i, 0)),
        out_shape=jax.ShapeDtypeStruct((nseg, n, SEG), F32),
        scratch_shapes=[pltpu.VMEM((tm, d), BF16)],
        compiler_params=_cparams(2),
        name="proj_in",
    )(x2d, gain.reshape(1, d), w3)


def _proj_out_kernel(a_ref, b_ref, x_ref, w_ref, o_ref):
    acc = jnp.dot(a_ref[...], w_ref[0], preferred_element_type=F32)
    acc = acc + jnp.dot(b_ref[...], w_ref[1], preferred_element_type=F32)
    o_ref[...] = x_ref[...] + acc


def _proj_out(a, b, x2d, w2):
    n, d = x2d.shape
    tm = min(n, 512)
    return pl.pallas_call(
        _proj_out_kernel,
        grid=(n // tm,),
        in_specs=[pl.BlockSpec((tm, SEG), lambda i: (i, 0)),
                  pl.BlockSpec((tm, SEG), lambda i: (i, 0)),
                  pl.BlockSpec((tm, d), lambda i: (i, 0)),
                  pl.BlockSpec((2, SEG, d), lambda i: (0, 0, 0))],
        out_specs=pl.BlockSpec((tm, d), lambda i: (i, 0)),
        out_shape=jax.ShapeDtypeStruct((n, d), F32),
        compiler_params=_cparams(1),
        name="proj_out",
    )(a, b, x2d, w2)


def _top_blocks_t(gate_t, n_valid, own, width):
    blk = lax.broadcasted_iota(jnp.int32, gate_t.shape, 0)
    g = jnp.where(blk < n_valid, gate_t, -jnp.inf)
    sel = blk == own
    for _ in range(MOBA_TOPK):
        m = jnp.max(g, axis=0, keepdims=True)
        idx = jnp.min(jnp.where(g == m, blk, width), axis=0, keepdims=True)
        pick = (blk == idx) & (m > -jnp.inf)
        sel = sel | pick
        g = jnp.where(pick, -jnp.inf, g)
    return sel


def _tn_dot(a, b):
    return lax.dot_general(a, b, (((0,), (0,)), ((), ())), preferred_element_type=F32)


def _online_softmax_step(carry, s, v):
    m, l, acc = carry
    m_new = jnp.maximum(m, jnp.max(s, axis=-1, keepdims=True))
    alpha = jnp.exp(m - m_new)
    pr = jnp.exp(s - m_new)
    l = alpha * l + jnp.sum(pr, axis=-1, keepdims=True)
    acc = alpha * acc + jnp.dot(pr.astype(BF16), v, preferred_element_type=F32)
    return m_new, l, acc


def _softmax_init(rows):
    return (jnp.full((rows, 1), NEG, F32), jnp.zeros((rows, 1), F32), jnp.zeros((rows, PAIR), F32))


def _flash_attention(n_tiles, logits_fn, v_tile_fn, m_ref, l_ref, acc_ref):
    s = logits_fn(0, True)
    reps = s.shape[1] // PAIR
    m = jnp.max(s, axis=-1, keepdims=True)
    pr = jnp.exp(s - m)
    m_ref[...] = jnp.broadcast_to(m, m_ref.shape)
    l_ref[...] = jnp.broadcast_to(jnp.sum(pr, axis=-1, keepdims=True), l_ref.shape)
    acc_ref[...] = jnp.dot(pr.astype(BF16), v_tile_fn(0), preferred_element_type=F32)

    def body(step, carry):
        s = logits_fn(step, False)
        m_prev = m_ref[...]
        m_new = jnp.maximum(m_prev, jnp.max(s, axis=-1, keepdims=True))
        alpha = jnp.exp(m_prev - m_new)
        pr = jnp.exp(s - jnp.concatenate([m_new] * reps, axis=1))
        m_ref[...] = m_new
        l_ref[...] = alpha * l_ref[...] + jnp.sum(pr, axis=-1, keepdims=True)
        acc_ref[...] = alpha * acc_ref[...] + jnp.dot(pr.astype(BF16), v_tile_fn(step),
                                                      preferred_element_type=F32)
        return carry

    lax.fori_loop(1, n_tiles, body, 0)
    return acc_ref[...] / l_ref[...]


ATT_PAD_BLOCKS = 3


def _flash_attention_2x(n_tiles, logits_fn, v_tile_fn, sa_ref, sb_ref, m_ref, l_ref, acc_ref):
    reps = sa_ref.shape[1] // PAIR
    sa_ref[...] = logits_fn(0, True)
    m_ref[...] = jnp.full(m_ref.shape, NEG, F32)
    l_ref[...] = jnp.zeros(l_ref.shape, F32)
    acc_ref[...] = jnp.zeros(acc_ref.shape, F32)

    def update(s_ref, step):
        s = s_ref[...]
        m_prev = m_ref[...]
        m_new = jnp.maximum(m_prev, jnp.max(s, axis=-1, keepdims=True))
        alpha = jnp.exp(m_prev - m_new)
        pr = jnp.exp(s - jnp.concatenate([m_new] * reps, axis=1))
        m_ref[...] = m_new
        l_ref[...] = alpha * l_ref[...] + jnp.sum(pr, axis=-1, keepdims=True)
        acc_ref[...] = alpha * acc_ref[...] + jnp.dot(pr.astype(BF16), v_tile_fn(step),
                                                      preferred_element_type=F32)

    def body(j, carry):
        sb_ref[...] = logits_fn(2 * j + 1, False)
        update(sa_ref, 2 * j)
        sa_ref[...] = logits_fn(jnp.minimum(2 * j + 2, n_tiles), False)
        update(sb_ref, 2 * j + 1)
        return carry

    lax.fori_loop(0, (n_tiles + 1) // 2, body, 0)
    return acc_ref[...] / l_ref[...]


def _moba_prompt_kernel(tab_ref, q_ref, k_ref, v_ref, g_ref, qg_ref, kg_ref,
                        o_ref, kt_ref, vt_ref,
                        kn_ref, vb_ref, km_ref, bias_ref, sa_ref, sb_ref, m_ref, l_ref, acc_ref):
    p = pl.program_id(1)
    i = pl.program_id(2)
    t = ATT_TILE
    nblk = k_ref.shape[1] // t
    lo = _lo_mask()

    @pl.when(i == 0)
    def _prep():
        lane = lax.broadcasted_iota(jnp.int32, (t, PAIR), 1)
        for c in range(ATT_PAD_BLOCKS):
            kn_ref[c * t:(c + 1) * t, 0:PAIR] = jnp.zeros((t, PAIR), BF16)
            kn_ref[c * t:(c + 1) * t, PAIR:2 * PAIR] = (lane == nblk).astype(BF16)
            vb_ref[c * t:(c + 1) * t, :] = jnp.zeros((t, PAIR), BF16)

        for c in range(nblk):
            r0 = c * t
            d0 = r0 + ATT_PAD_BLOCKS * t
            kn = _pair_rmsnorm(k_ref[0, r0:r0 + t, :], kg_ref[...], lo)
            v = v_ref[0, r0:r0 + t, :]
            kt_ref[0, :, r0:r0 + t] = kn.T
            vt_ref[0, :, r0:r0 + t] = v.T
            kn_ref[d0:d0 + t, 0:PAIR] = kn.astype(BF16)
            kn_ref[d0:d0 + t, PAIR:2 * PAIR] = (lane == c).astype(BF16)
            km_ref[c:c + 1, :] = jnp.sum(kn, axis=0, keepdims=True) * (1.0 / t)
            vb_ref[d0:d0 + t, :] = v.astype(BF16)
        rel = (lax.broadcasted_iota(jnp.int32, (t, t), 0) - lax.broadcasted_iota(jnp.int32, (t, t), 1))
        for hh in range(2):
            h = 2 * p + hh
            bias_ref[0, hh * t:(hh + 1) * t, 0:t] = _t5_bias(rel + t, tab_ref, h)
            bias_ref[0, hh * t:(hh + 1) * t, t:2 * t] = jnp.where(rel >= 0, _t5_bias(rel, tab_ref, h), NEG)
            bias_ref[1, hh * t:(hh + 1) * t, :] = jnp.full((t, 2 * t), tab_ref[T5_BUCKETS - 1, h], F32)

    qn = _pair_rmsnorm(q_ref[0], qg_ref[...], lo)
    km = km_ref[...]
    km_heads = jnp.concatenate([jnp.where(lo, km, 0.0), jnp.where(lo, 0.0, km),
                                jnp.zeros((PAIR - 2 * nblk, PAIR), F32)], axis=0)
    gate_all = _nt_dot(qn, km_heads, precision=HIGHEST).T
    no_block = jnp.where(lax.broadcasted_iota(jnp.int32, (PAIR - nblk, t), 0) == 0, NEG, 0.0)
    qs, pens = [], []
    for hh in range(2):
        hm = lo if hh == 0 else jnp.logical_not(lo)
        qh = jnp.where(hm, qn, 0.0)
        gate_t = gate_all[hh * nblk:(hh + 1) * nblk]
        sel = _top_blocks_t(gate_t, i, i, nblk)
        pen_t = jnp.concatenate([jnp.where(sel, 0.0, NEG), no_block], axis=0)
        pens.append(pen_t.T)
        qs.append(qh * (HEAD_DIM ** -0.5))
    q2 = jnp.concatenate([jnp.concatenate(qs, axis=0), jnp.concatenate(pens, axis=0)], axis=1).astype(BF16)

    def tile_rows(step):
        return pl.ds(pl.multiple_of((i - 2 * step - 1 + ATT_PAD_BLOCKS) * t, t), 2 * t)

    def logits(step, first):
        return _nt_dot(q2, kn_ref[tile_rows(step), :]) + bias_ref[0 if first else 1]

    o = _flash_attention((i + 2) // 2, logits, lambda step: vb_ref[tile_rows(step), :], m_ref, l_ref, acc_ref)
    o = jnp.where(lo, o[0:t], o[t:2 * t])
    o_ref[...] = (o * _silu(g_ref[0])).astype(BF16)


def _heads_from_token_minor(xt, batch, seq):
    return xt.reshape(batch, N_HEADS, HEAD_DIM, seq).transpose(0, 3, 1, 2)


def _moba_prompt(z3, batch, seq, t5_table, q_gain, k_gain):
    n = batch * seq
    t = ATT_TILE
    nq = seq // t
    npair = SEG // PAIR
    gain2 = lambda g: jnp.tile(g.astype(F32), 2).reshape(1, PAIR)
    return pl.pallas_call(
        _moba_prompt_kernel,
        grid=(batch, npair, nq),
        in_specs=[pl.BlockSpec(memory_space=pltpu.SMEM),
                  pl.BlockSpec((1, t, PAIR), lambda b, p, i: (0, b * nq + i, p)),
                  pl.BlockSpec((1, seq, PAIR), lambda b, p, i: (1, b, p)),
                  pl.BlockSpec((1, seq, PAIR), lambda b, p, i: (2, b, p)),
                  pl.BlockSpec((1, t, PAIR), lambda b, p, i: (3, b * nq + i, p)),
                  pl.BlockSpec((1, PAIR), lambda b, p, i: (0, 0)),
                  pl.BlockSpec((1, PAIR), lambda b, p, i: (0, 0))],
        out_specs=[pl.BlockSpec((t, PAIR), lambda b, p, i: (b * nq + i, p)),
                   pl.BlockSpec((1, PAIR, seq), lambda b, p, i: (b, p, 0)),
                   pl.BlockSpec((1, PAIR, seq), lambda b, p, i: (b, p, 0))],
        out_shape=[jax.ShapeDtypeStruct((n, SEG), BF16),
                   jax.ShapeDtypeStruct((batch, SEG, seq), F32),
                   jax.ShapeDtypeStruct((batch, SEG, seq), F32)],
        scratch_shapes=[pltpu.VMEM((seq + ATT_PAD_BLOCKS * t, 2 * PAIR), BF16),
                        pltpu.VMEM((seq + ATT_PAD_BLOCKS * t, PAIR), BF16),
                        pltpu.VMEM((seq // t, PAIR), F32),
                        pltpu.VMEM((2, 2 * t, 2 * t), F32),
                        pltpu.VMEM((2 * t, 2 * t), F32),
                        pltpu.VMEM((2 * t, 2 * t), F32),
                        pltpu.VMEM((2 * t, PAIR), F32),
                        pltpu.VMEM((2 * t, PAIR), F32),
                        pltpu.VMEM((2 * t, PAIR), F32)],
        compiler_params=_cparams(3),
        name="moba_prompt",
    )(t5_table.astype(F32), z3, z3, z3, z3, gain2(q_gain), gain2(k_gain))


def _rope_pair(x, cos, sin_signed):
    up = pltpu.roll(x, PAIR - HEAD_DIM // 2, 1)
    dn = pltpu.roll(x, HEAD_DIM // 2, 1)
    lane = lax.broadcasted_iota(jnp.int32, (1, PAIR), 1)
    first_half = (lane % HEAD_DIM) < (HEAD_DIM // 2)
    return x * cos + jnp.where(first_half, up, dn) * sin_signed


def _retention_kernel(lg_ref, q_ref, k_ref, v_ref, g_ref, cos_ref, sin_ref, gain_ref, s0_ref,
                      o_ref, sout_ref, s_ref, *, groups, mm_dtype):
    p = pl.program_id(1)
    c = pl.program_id(2)
    n = q_ref.shape[0]
    glen = n // groups
    lo = _lo_mask()

    @pl.when(c == 0)
    def _():
        s_ref[...] = s0_ref[...]

    cos = cos_ref[...]
    sin = sin_ref[...]
    qr = _rope_pair(q_ref[...], cos, sin)
    kr = _rope_pair(k_ref[...], cos, sin) * (HEAD_DIM ** -0.5)
    vb = v_ref[...].astype(mm_dtype)
    lg0 = lg_ref[2 * p]
    lg1 = lg_ref[2 * p + 1]
    lg_lane = jnp.where(lo, lg0, lg1)
    row = lax.broadcasted_iota(jnp.int32, (n, 1), 0)
    pos = (row % glen).astype(F32)
    grp = row // glen
    q_dec = jnp.exp(lg_lane * (pos + 1.0))
    k_dec = jnp.exp(lg_lane * (float(glen - 1) - pos))
    qd = (qr * q_dec).astype(mm_dtype)
    o = jnp.zeros((n, PAIR), F32)
    for g in range(groups):
        og = jnp.dot(qd, s_ref[g].astype(mm_dtype), preferred_element_type=F32)
        o = o + (og if groups == 1 else jnp.where(grp == g, og, 0.0))
    ri = lax.broadcasted_iota(jnp.int32, (n, n), 0)
    ci = lax.broadcasted_iota(jnp.int32, (n, n), 1)
    causal = (ri >= ci) & ((ri // glen) == (ci // glen))
    dpos = jnp.where(causal, ri - ci, 0).astype(F32)
    krb = kr.astype(mm_dtype)
    for hh in range(2):
        hm = lo if hh == 0 else jnp.logical_not(lo)
        lgh = lg0 if hh == 0 else lg1
        intra = jnp.where(causal, jnp.exp(lgh * dpos), 0.0)
        a = _nt_dot(jnp.where(hm, qr, 0.0).astype(mm_dtype), krb) * intra
        oh = jnp.dot(a.astype(mm_dtype), vb, preferred_element_type=F32)
        o = o + jnp.where(hm, oh, 0.0)
    srow = lax.broadcasted_iota(jnp.int32, (PAIR, 1), 0) < HEAD_DIM
    c_dec = jnp.exp(jnp.where(srow, lg0, lg1) * float(glen))
    same_head = srow == lo
    kd = kr * k_dec
    for g in range(groups):
        kg = kd if groups == 1 else jnp.where(grp == g, kd, 0.0)
        kv = lax.dot_general(kg.astype(mm_dtype), vb, (((0,), (0,)), ((), ())),
                             preferred_element_type=F32)
        new_state = jnp.where(same_head, s_ref[g] * c_dec + kv, 0.0)
        s_ref[g] = new_state
        sout_ref[g] = new_state

    mu = _pair_sum(o, lo) * (1.0 / HEAD_DIM)
    oc = o - mu
    var = _pair_sum(oc * oc, lo) * (1.0 / HEAD_DIM)
    on = oc * lax.rsqrt(var + EPS) * gain_ref[...]
    o_ref[...] = (on * _silu(g_ref[...])).astype(BF16)


def _pair_blockdiag(s):
    b = s.shape[0]
    s = s.astype(F32).reshape(b, N_HEADS // 2, 2, HEAD_DIM, HEAD_DIM)
    z = jnp.zeros_like(s[:, :, 0])
    top = jnp.concatenate([s[:, :, 0], z], axis=-1)
    bot = jnp.concatenate([z, s[:, :, 1]], axis=-1)
    return jnp.concatenate([top, bot], axis=-2)


def _pair_unblockdiag(sp):
    b = sp.shape[0]
    s0 = sp[:, :, :HEAD_DIM, :HEAD_DIM]
    s1 = sp[:, :, HEAD_DIM:, HEAD_DIM:]
    return jnp.stack([s0, s1], axis=2).reshape(b, N_HEADS, HEAD_DIM, HEAD_DIM)


def _rope_tables(pos):
    half = HEAD_DIM // 2
    inv = ROPE_BASE ** (-jnp.arange(half, dtype=F32) / half)
    ang = pos.astype(F32)[:, None] * inv[None, :]
    cos, sin = jnp.cos(ang), jnp.sin(ang)
    cos_t = jnp.tile(cos, (1, PAIR // half))
    sin_t = jnp.tile(jnp.concatenate([-sin, sin], axis=-1), (1, 2))
    return cos_t, sin_t


def _log_gamma():
    return jnp.log1p(-jnp.exp2(-5.0 - jnp.arange(N_HEADS, dtype=F32)))


def _retention(z3, n_seq, seq, rows, groups, pos0, ret_gain, state_pairs, mm_dtype, name):
    n = n_seq * seq
    nc = (seq * groups) // rows
    nb = n_seq // groups
    npair = SEG // PAIR
    cos_t, sin_t = _rope_tables(pos0 + jnp.arange(seq, dtype=jnp.int32))
    if groups > 1:
        cos_t, sin_t = jnp.tile(cos_t, (groups, 1)), jnp.tile(sin_t, (groups, 1))
    z4 = z3.reshape(z3.shape[0], n // rows, rows, SEG)
    tok = lambda s: pl.BlockSpec((None, None, rows, PAIR), lambda b, p, c, s=s: (s, b * nc + c, 0, p))
    o, sp = pl.pallas_call(
        functools.partial(_retention_kernel, groups=groups, mm_dtype=mm_dtype),
        grid=(nb, npair, nc),
        in_specs=[pl.BlockSpec(memory_space=pltpu.SMEM),
                  tok(4), tok(5), tok(6), tok(7),
                  pl.BlockSpec((rows, PAIR), lambda b, p, c: (c, 0)),
                  pl.BlockSpec((rows, PAIR), lambda b, p, c: (c, 0)),
                  pl.BlockSpec((1, PAIR), lambda b, p, c: (0, p)),
                  pl.BlockSpec((groups, None, PAIR, PAIR), lambda b, p, c: (b, p, 0, 0))],
        out_specs=[pl.BlockSpec((None, rows, PAIR), lambda b, p, c: (b * nc + c, 0, p)),
                   pl.BlockSpec((groups, None, PAIR, PAIR), lambda b, p, c: (b, p, 0, 0))],
        out_shape=[jax.ShapeDtypeStruct((n // rows, rows, SEG), BF16),
                   jax.ShapeDtypeStruct((n_seq, npair, PAIR, PAIR), F32)],
        scratch_shapes=[pltpu.VMEM((groups, PAIR, PAIR), F32)],
        compiler_params=_cparams(3),
        name=name,
    )(_log_gamma(), z4, z4, z4, z4, cos_t, sin_t, ret_gain.astype(F32).reshape(1, SEG), state_pairs)
    return o.reshape(n, SEG), sp


def _even_weights(w_in, w_out):
    d = w_in.shape[0]
    w3 = w_in.reshape(d, 8, SEG).transpose(1, 0, 2).astype(BF16)
    w2 = w_out.reshape(2, SEG, d).astype(BF16)
    return w3, w2


def _even_layer_prompt(x, norm_g, w_in, q_gain, k_gain, t5_table, ret_gain, w_out):
    b, l, d = x.shape
    x2d = x.reshape(b * l, d)
    w3, w2 = _even_weights(w_in, w_out)
    z3 = _proj_in(x2d, norm_g, w3)
    oa, kt, vt = _moba_prompt(z3, b, l, t5_table, q_gain, k_gain)
    zero_state = jnp.zeros((b, SEG // PAIR, PAIR, PAIR), F32)
    orr, sp = _retention(z3, b, l, min(l, 256), 1, 0, ret_gain, zero_state, BF16, "retention_prompt")
    y = _proj_out(oa, orr, x2d, w2)
    return (y.reshape(b, l, d), _heads_from_token_minor(kt, b, l), _heads_from_token_minor(vt, b, l),
            _pair_unblockdiag(sp))


def _log_sigmoid(x):
    y = -x
    return -(jnp.maximum(y, 0.0) + jnp.log1p(jnp.exp(-jnp.abs(y))))


def _logf_kernel(fz_ref, b_ref, lf_ref, fc_ref):
    seq = fz_ref.shape[1]
    t = ATT_TILE
    tri = (lax.broadcasted_iota(jnp.int32, (t, t), 0) >= lax.broadcasted_iota(jnp.int32, (t, t), 1)).astype(F32)
    carry = jnp.zeros((1, PAIR), F32)
    for c in range(seq // t):
        lf = _log_sigmoid(fz_ref[0, c * t:(c + 1) * t, :] + b_ref[...])
        cs = jnp.dot(tri, lf, precision=HIGHEST, preferred_element_type=F32) + carry
        lf_ref[0, c * t:(c + 1) * t, :] = lf[:, :N_HEADS]
        fc_ref[0, c * t:(c + 1) * t, :] = cs[:, :N_HEADS]
        carry = cs[t - 1:t, :]


def _logf_prompt(z3, batch, seq, b_forget):
    bpad = jnp.zeros((1, PAIR), F32).at[0, :N_HEADS].set(b_forget.astype(F32))
    return pl.pallas_call(
        _logf_kernel,
        grid=(batch,),
        in_specs=[pl.BlockSpec((1, seq, PAIR), lambda b: (7, b, 0)),
                  pl.BlockSpec((1, PAIR), lambda b: (0, 0))],
        out_specs=[pl.BlockSpec((1, seq, N_HEADS), lambda b: (b, 0, 0)),
                   pl.BlockSpec((1, seq, N_HEADS), lambda b: (b, 0, 0))],
        out_shape=[jax.ShapeDtypeStruct((batch, seq, N_HEADS), F32),
                   jax.ShapeDtypeStruct((batch, seq, N_HEADS), F32)],
        compiler_params=_cparams(1),
        name="logf_prompt",
    )(z3, bpad)


def _lane_split3(x):
    hi = x.astype(BF16).astype(F32)
    rest = x - hi
    mid = rest.astype(BF16).astype(F32)
    r = lax.broadcasted_iota(jnp.int32, x.shape, 1) % 3
    return jnp.where(r == 0, hi, jnp.where(r == 1, mid, rest - mid))


def _fox_prompt_kernel(q_ref, k_ref, v_ref, g_ref, fq_ref, fk_ref, qg_ref, kg_ref,
                       o_ref, kt_ref, vt_ref,
                       kn_ref, vb_ref, mask_ref, sa_ref, sb_ref, m_ref, l_ref, acc_ref):
    p = pl.program_id(1)
    i = pl.program_id(2)
    t = ATT_TILE
    nblk = k_ref.shape[1] // t
    lo = _lo_mask()
    lane = lax.broadcasted_iota(jnp.int32, (t, PAIR), 1)
    prow = lax.broadcasted_iota(jnp.int32, (N_HEADS, PAIR), 0)
    plane = lax.broadcasted_iota(jnp.int32, (N_HEADS, PAIR), 1)

    @pl.when(i == 0)
    def _prep():
        for c in range(ATT_PAD_BLOCKS):
            kn_ref[c * t:(c + 1) * t, 0:PAIR] = jnp.zeros((t, PAIR), BF16)
            kn_ref[c * t:(c + 1) * t, PAIR:2 * PAIR] = jnp.where((lane == 0) | (lane == 3), NEG, 0.0).astype(BF16)
            vb_ref[c * t:(c + 1) * t, :] = jnp.zeros((t, PAIR), BF16)
        place = jnp.where(((prow == 2 * p) & (plane < 3)) | ((prow == 2 * p + 1) & (plane >= 3) & (plane < 6)),
                          -1.0, 0.0)

        for c in range(nblk):
            r0 = c * t
            d0 = r0 + ATT_PAD_BLOCKS * t
            kn = _pair_rmsnorm(k_ref[0, r0:r0 + t, :], kg_ref[...], lo)
            v = v_ref[0, r0:r0 + t, :]
            kt_ref[0, :, r0:r0 + t] = kn.T
            vt_ref[0, :, r0:r0 + t] = v.T
            kn_ref[d0:d0 + t, 0:PAIR] = kn.astype(BF16)
            neg_fk = jnp.dot(fk_ref[0, r0:r0 + t, :], place, precision=HIGHEST, preferred_element_type=F32)
            up = jnp.where((lane >= 6) & (lane < 9), 1.0, _lane_split3(neg_fk))
            kn_ref[d0:d0 + t, PAIR:2 * PAIR] = up.astype(BF16)
            vb_ref[d0:d0 + t, :] = v.astype(BF16)
        row = lax.broadcasted_iota(jnp.int32, (2 * t, 2 * t), 0) % t
        col = lax.broadcasted_iota(jnp.int32, (2 * t, 2 * t), 1) - t
        mask_ref[...] = jnp.where(row >= col, 0.0, NEG)

    qn = _pair_rmsnorm(q_ref[0], qg_ref[...], lo)
    qs, ups = [], []
    for hh in range(2):
        hm = lo if hh == 0 else jnp.logical_not(lo)
        qs.append(jnp.where(hm, qn, 0.0) * (HEAD_DIM ** -0.5))
        place_q = jnp.where((prow == 2 * p + hh) & (plane >= 6) & (plane < 9), 1.0, 0.0)
        fq = jnp.dot(fq_ref[0], place_q, precision=HIGHEST, preferred_element_type=F32)
        ups.append(jnp.where((lane >= 3 * hh) & (lane < 3 * hh + 3), 1.0, _lane_split3(fq)))
    q2 = jnp.concatenate([jnp.concatenate(qs, axis=0), jnp.concatenate(ups, axis=0)], axis=1).astype(BF16)

    def tile_rows(step):
        return pl.ds(pl.multiple_of((i - 2 * step - 1 + ATT_PAD_BLOCKS) * t, t), 2 * t)

    def logits(step, first):
        s = _nt_dot(q2, kn_ref[tile_rows(step), :])
        return s + mask_ref[...] if first else s

    o = _flash_attention_2x((i + 2) // 2, logits, lambda step: vb_ref[tile_rows(step), :],
                            sa_ref, sb_ref, m_ref, l_ref, acc_ref)
    o = jnp.where(lo, o[0:t], o[t:2 * t])
    o_ref[...] = (o * _silu(g_ref[0])).astype(BF16)


def _fox_prompt(z3, batch, seq, fcum, q_gain, k_gain):
    n = batch * seq
    t = ATT_TILE
    nq = seq // t
    npair = SEG // PAIR
    gain2 = lambda g: jnp.tile(g.astype(F32), 2).reshape(1, PAIR)
    return pl.pallas_call(
        _fox_prompt_kernel,
        grid=(batch, npair, nq),
        in_specs=[pl.BlockSpec((1, t, PAIR), lambda b, p, i: (0, b * nq + i, p)),
                  pl.BlockSpec((1, seq, PAIR), lambda b, p, i: (1, b, p)),
                  pl.BlockSpec((1, seq, PAIR), lambda b, p, i: (2, b, p)),
                  pl.BlockSpec((1, t, PAIR), lambda b, p, i: (3, b * nq + i, p)),
                  pl.BlockSpec((1, t, N_HEADS), lambda b, p, i: (b, i, 0)),
                  pl.BlockSpec((1, seq, N_HEADS), lambda b, p, i: (b, 0, 0)),
                  pl.BlockSpec((1, PAIR), lambda b, p, i: (0, 0)),
                  pl.BlockSpec((1, PAIR), lambda b, p, i: (0, 0))],
        out_specs=[pl.BlockSpec((t, PAIR), lambda b, p, i: (b * nq + i, p)),
                   pl.BlockSpec((1, PAIR, seq), lambda b, p, i: (b, p, 0)),
                   pl.BlockSpec((1, PAIR, seq), lambda b, p, i: (b, p, 0))],
        out_shape=[jax.ShapeDtypeStruct((n, SEG), BF16),
                   jax.ShapeDtypeStruct((batch, SEG, seq), F32),
                   jax.ShapeDtypeStruct((batch, SEG, seq), F32)],
        scratch_shapes=[pltpu.VMEM((seq + ATT_PAD_BLOCKS * t, 2 * PAIR), BF16),
                        pltpu.VMEM((seq + ATT_PAD_BLOCKS * t, PAIR), BF16),
                        pltpu.VMEM((2 * t, 2 * t), F32),
                        pltpu.VMEM((2 * t, 2 * t), F32),
                        pltpu.VMEM((2 * t, 2 * t), F32),
                        pltpu.VMEM((2 * t, PAIR), F32),
                        pltpu.VMEM((2 * t, PAIR), F32),
                        pltpu.VMEM((2 * t, PAIR), F32)],
        compiler_params=_cparams(3),
        name="fox_prompt",
    )(z3, z3, z3, z3, fcum, fcum, gain2(q_gain), gain2(k_gain))


CONV_HALO = 32
CONV_ROWS = 32


def _conv_prompt_kernel(ua_ref, ub_ref, gd_ref, uah_ref, ubh_ref, st_ref, w_ref, cb_ref, cg_ref,
                        d_ref, cs_ref, buf_ref, y_ref):
    ti = pl.program_id(1)
    nt = pl.num_programs(1)
    t = ua_ref.shape[1]
    pad = CONV_HALO - (CONV_WIDTH - 1)
    buf_ref[CONV_HALO:CONV_HALO + t, :] = ua_ref[0] * jax.nn.sigmoid(ub_ref[0])

    @pl.when(ti == 0)
    def _():
        buf_ref[0:CONV_HALO, :] = st_ref[0]

    @pl.when(ti > 0)
    def _():
        buf_ref[0:CONV_HALO, :] = uah_ref[0] * jax.nn.sigmoid(ubh_ref[0])

    for lg in range(SEG // PAIR):
        ls = slice(lg * PAIR, (lg + 1) * PAIR)
        for rc in range(t // CONV_ROWS):
            r0 = rc * CONV_ROWS
            acc = jnp.zeros((CONV_ROWS, PAIR), F32) + cb_ref[:, ls]
            for k in range(CONV_WIDTH):
                acc = acc + w_ref[k:k + 1, ls] * buf_ref[r0 + pad + k:r0 + pad + k + CONV_ROWS, ls]
            y_ref[r0:r0 + CONV_ROWS, ls] = acc

    y = y_ref[...]
    mu = jnp.mean(y, axis=-1, keepdims=True)
    yc = y - mu
    var = jnp.mean(yc * yc, axis=-1, keepdims=True)
    yn = yc * lax.rsqrt(var + EPS) * cg_ref[...]
    d_ref[...] = (_silu(yn) * _silu(gd_ref[0])).astype(BF16)

    @pl.when(ti == nt - 1)
    def _():
        cs_ref[0] = buf_ref[t + pad:t + CONV_HALO, :]


def _conv_prompt(z3, batch, seq, state, conv_w, conv_b, conv_gain):
    n = batch * seq
    t = min(seq, 256)
    nt = seq // t
    hb = t // CONV_HALO
    st = jnp.pad(state.astype(F32), ((0, 0), (CONV_HALO - (CONV_WIDTH - 1), 0), (0, 0)))
    tok = lambda s: pl.BlockSpec((1, t, SEG), lambda b, i, s=s: (s, b * nt + i, 0))
    halo = lambda s: pl.BlockSpec((1, CONV_HALO, SEG),
                                  lambda b, i, s=s: (s, jnp.maximum((b * nt + i) * hb - 1, 0), 0))
    row = pl.BlockSpec((1, SEG), lambda b, i: (0, 0))
    return pl.pallas_call(
        _conv_prompt_kernel,
        grid=(batch, nt),
        in_specs=[tok(4), tok(5), tok(6), halo(4), halo(5),
                  pl.BlockSpec((1, CONV_HALO, SEG), lambda b, i: (b, 0, 0)),
                  pl.BlockSpec((CONV_WIDTH, SEG), lambda b, i: (0, 0)), row, row],
        out_specs=[pl.BlockSpec((t, SEG), lambda b, i: (b * nt + i, 0)),
                   pl.BlockSpec((1, CONV_WIDTH - 1, SEG), lambda b, i: (b, 0, 0))],
        out_shape=[jax.ShapeDtypeStruct((n, SEG), BF16),
                   jax.ShapeDtypeStruct((batch, CONV_WIDTH - 1, SEG), F32)],
        scratch_shapes=[pltpu.VMEM((t + CONV_HALO, SEG), F32),
                        pltpu.VMEM((t, SEG), F32)],
        compiler_params=_cparams(2),
        name="conv_prompt",
    )(z3, z3, z3, z3, z3, st, conv_w.astype(F32), conv_b.astype(F32).reshape(1, SEG),
      conv_gain.astype(F32).reshape(1, SEG))


def _odd_weights(w_in, w_out):
    d = w_in.shape[0]
    w = SEG
    fz = w_in[:, 4 * w:4 * w + N_HEADS]
    rest = jnp.concatenate([w_in[:, :4 * w], w_in[:, 4 * w + N_HEADS:]], axis=1)
    fz_pad = jnp.pad(fz, ((0, 0), (0, w - N_HEADS)))
    w3 = jnp.concatenate([rest, fz_pad], axis=1).reshape(d, 8, w).transpose(1, 0, 2).astype(BF16)
    w2 = w_out.reshape(2, w, d).astype(BF16)
    return w3, w2


def _odd_layer_prompt(x, norm_g, w_in, b_forget, q_gain, k_gain, conv_w, conv_b, conv_gain, w_out):
    b, l, d = x.shape
    x2d = x.reshape(b * l, d)
    w3, w2 = _odd_weights(w_in, w_out)
    z3 = _proj_in(x2d, norm_g, w3)
    logf, fcum = _logf_prompt(z3, b, l, b_forget)
    oc, kt, vt = _fox_prompt(z3, b, l, fcum, q_gain, k_gain)
    dd, cs = _conv_prompt(z3, b, l, jnp.zeros((b, CONV_WIDTH - 1, SEG), F32), conv_w, conv_b, conv_gain)
    y = _proj_out(oc, dd, x2d, w2)
    return (y.reshape(b, l, d), _heads_from_token_minor(kt, b, l), _heads_from_token_minor(vt, b, l), logf, cs)


def _heads_rmsnorm(x, gain):
    lo = _lo_mask()
    parts = [_pair_rmsnorm(x[:, g * PAIR:(g + 1) * PAIR], gain[:, g * PAIR:(g + 1) * PAIR], lo)
             for g in range(SEG // PAIR)]
    return jnp.concatenate(parts, axis=-1)


def _query_rows(qn, n_tok):
    sub = lax.broadcasted_iota(jnp.int32, (N_HEADS, SEG), 0)
    head_of_lane = lax.broadcasted_iota(jnp.int32, (N_HEADS, SEG), 1) // HEAD_DIM
    own = sub == head_of_lane
    return jnp.concatenate([jnp.where(own, jnp.broadcast_to(qn[q:q + 1, :], (N_HEADS, SEG)), 0.0)
                            for q in range(n_tok)], axis=0)


def _rows_to_tokens(rows_out, n_tok):
    sub = lax.broadcasted_iota(jnp.int32, (N_HEADS, SEG), 0)
    head_of_lane = lax.broadcasted_iota(jnp.int32, (N_HEADS, SEG), 1) // HEAD_DIM
    own = sub == head_of_lane
    tok = lax.broadcasted_iota(jnp.int32, (n_tok, SEG), 0)
    out = jnp.zeros((n_tok, SEG), F32)
    for q in range(n_tok):
        o_q = jnp.sum(jnp.where(own, rows_out[q * N_HEADS:(q + 1) * N_HEADS, :], 0.0), axis=0, keepdims=True)
        out = jnp.where(tok == q, o_q, out)
    return out


def _t5_bias_rows(rel, tab):
    bias = jnp.broadcast_to(tab[:, T5_BUCKETS - 1:T5_BUCKETS], rel.shape)
    for b in range(T5_BUCKETS - 2, -1, -1):
        bias = jnp.where(rel <= T5_UPPER[b], tab[:, b:b + 1], bias)
    return bias


def _finish_rows(s_past, s_new, v_pages, v_new, n_tok):
    m = jnp.max(s_past, axis=-1, keepdims=True)
    for col in s_new:
        m = jnp.maximum(m, col)
    p_past = jnp.exp(s_past - m)
    l = jnp.sum(p_past, axis=-1, keepdims=True)
    pb = p_past.astype(BF16)
    acc = jnp.zeros((s_past.shape[0], SEG), F32)
    for pg, v_ref in enumerate(v_pages):
        acc = acc + _nt_dot(pb[:, pg * PAGE_SIZE:(pg + 1) * PAGE_SIZE], v_ref[0].astype(BF16))
    for kj, col in enumerate(s_new):
        p_new = jnp.exp(col - m)
        l = l + p_new
        acc = acc + p_new * v_new[kj:kj + 1, :]
    return _rows_to_tokens(acc / l, n_tok)


SAMPLE_SEQS_PER_STEP = 1


def _moba_sample_kernel(pt_ref, tab_ref, q_ref, k_ref, v_ref, g_ref, qg_ref, kg_ref, *rest, n_pages, n_tok, seqs):
    o_ref, ka_ref = rest[2 * seqs * n_pages:]
    for r in range(seqs):
        _moba_sample_one(r, tab_ref, q_ref, k_ref, v_ref, g_ref, qg_ref, kg_ref,
                         rest[r * n_pages:(r + 1) * n_pages],
                         rest[(seqs + r) * n_pages:(seqs + r + 1) * n_pages], o_ref, ka_ref, n_pages, n_tok)


def _moba_sample_one(r, tab_ref, q_ref, k_ref, v_ref, g_ref, qg_ref, kg_ref, k_pages, v_pages, o_ref, ka_ref,
                     n_pages, n_tok):
    pages_per_block = MOBA_BLOCK // PAGE_SIZE
    nb_past = n_pages // pages_per_block
    past_len = n_pages * PAGE_SIZE
    n_rows = n_tok * N_HEADS

    qn = _heads_rmsnorm(q_ref[0, r], qg_ref[...])
    kn = _heads_rmsnorm(k_ref[0, r], kg_ref[...])
    ka_ref[r] = kn
    v_new = v_ref[0, r]
    qrows = _query_rows(qn, n_tok)
    qb = (qrows * (HEAD_DIM ** -0.5)).astype(BF16)
    tab = tab_ref[...]
    row_q = lax.broadcasted_iota(jnp.int32, (n_rows, 1), 0) // N_HEADS

    raw = [jnp.dot(qb, k_pages[pg][0].astype(BF16), preferred_element_type=F32) for pg in range(n_pages)]
    blk_lane = lax.broadcasted_iota(jnp.int32, (SEG, nb_past), 1)
    k_mean = jnp.zeros((SEG, nb_past), F32)
    for n in range(nb_past):
        tot = k_pages[n * pages_per_block][0]
        for pg in range(n * pages_per_block + 1, (n + 1) * pages_per_block):
            tot = tot + k_pages[pg][0]
        k_mean = jnp.where(blk_lane == n, jnp.sum(tot, axis=-1, keepdims=True) * (1.0 / MOBA_BLOCK), k_mean)
    gate = jnp.dot(qrows, k_mean, precision=HIGHEST, preferred_element_type=F32)
    selpen = jnp.where(_top_blocks(gate, nb_past, nb_past, nb_past), 0.0, NEG)

    far = tab[:, T5_BUCKETS - 1:T5_BUCKETS]
    lane = lax.broadcasted_iota(jnp.int32, (n_rows, PAGE_SIZE), 1)
    tiles = []
    for pg in range(n_pages):
        n = pg // pages_per_block
        min_rel = past_len - (pg + 1) * PAGE_SIZE + 1
        if min_rel > T5_UPPER[-1]:
            bias = far
        else:
            bias = _t5_bias_rows(past_len + row_q - (pg * PAGE_SIZE + lane), tab)
        tiles.append(raw[pg] + bias + selpen[:, n:n + 1])
    s_past = jnp.concatenate(tiles, axis=-1)

    s_new = []
    for kj in range(n_tok):
        dot = jnp.sum(qrows * kn[kj:kj + 1, :], axis=-1, keepdims=True) * (HEAD_DIM ** -0.5)
        rel = row_q - kj
        s_new.append(jnp.where(rel >= 0, dot + _t5_bias_rows(rel, tab), NEG))

    o = _finish_rows(s_past, s_new, v_pages, v_new, n_tok)
    o_ref[r] = (o * _silu(g_ref[0, r])).astype(BF16)


def _page_specs(n_pages, rows, seqs):
    return [pl.BlockSpec((1, rows, PAGE_SIZE), lambda b, pt, r=r, pg=pg: (pt[b * seqs + r, pg], 0, 0))
            for r in range(seqs) for pg in range(n_pages)]


def _pages_token_minor(cache):
    n_pool = cache.shape[0]
    return cache.transpose(0, 2, 3, 1).reshape(n_pool, SEG, PAGE_SIZE)


def _moba_sample(z3, n_seq, n_tok, cache_k, cache_v, page_table, t5_table, q_gain, k_gain):
    n_pages = page_table.shape[1]
    n_pool = cache_k.shape[0]
    assert MOBA_BLOCK % PAGE_SIZE == 0 and (n_pages * PAGE_SIZE) % MOBA_BLOCK == 0
    assert n_tok <= MOBA_BLOCK and n_pages // (MOBA_BLOCK // PAGE_SIZE) >= MOBA_TOPK
    seqs = math.gcd(n_seq, SAMPLE_SEQS_PER_STEP)
    z4 = z3.reshape(z3.shape[0], n_seq, n_tok, SEG)
    tok = lambda s: pl.BlockSpec((1, seqs, n_tok, SEG), lambda b, pt, s=s: (s, b, 0, 0))
    row = pl.BlockSpec((1, SEG), lambda b, pt: (0, 0))
    gain8 = lambda g: jnp.tile(g.astype(F32), N_HEADS).reshape(1, SEG)
    tab_rows = jnp.tile(t5_table.astype(F32).T, (n_tok, 1))
    grid_spec = pltpu.PrefetchScalarGridSpec(
        num_scalar_prefetch=1,
        grid=(n_seq // seqs,),
        in_specs=[pl.BlockSpec((n_tok * N_HEADS, T5_BUCKETS), lambda b, pt: (0, 0)),
                  tok(0), tok(1), tok(2), tok(3), row, row]
                 + _page_specs(n_pages, SEG, seqs) + _page_specs(n_pages, SEG, seqs),
        out_specs=[pl.BlockSpec((seqs, n_tok, SEG), lambda b, pt: (b, 0, 0)),
                   pl.BlockSpec((seqs, n_tok, SEG), lambda b, pt: (b, 0, 0))])
    ck = _pages_token_minor(cache_k)
    cv = _pages_token_minor(cache_v)
    return pl.pallas_call(
        functools.partial(_moba_sample_kernel, n_pages=n_pages, n_tok=n_tok, seqs=seqs),
        grid_spec=grid_spec,
        out_shape=[jax.ShapeDtypeStruct((n_seq, n_tok, SEG), BF16),
                   jax.ShapeDtypeStruct((n_seq, n_tok, SEG), F32)],
        compiler_params=_cparams(1),
        name="moba_sample",
    )(page_table, tab_rows, z4, z4, z4, z4, gain8(q_gain), gain8(k_gain),
      *([ck] * (seqs * n_pages)), *([cv] * (seqs * n_pages)))


def _fox_sample_kernel(pt_ref, q_ref, k_ref, v_ref, g_ref, fz_ref, fzr_ref, bf_ref, bfr_ref, qg_ref, kg_ref,
                       *rest, n_pages, n_tok, seqs):
    o_ref, kc_ref, lf_ref = rest[3 * seqs * n_pages:]
    for r in range(seqs):
        pages = [rest[(part * seqs + r) * n_pages:(part * seqs + r + 1) * n_pages] for part in range(3)]
        _fox_sample_one(r, q_ref, k_ref, v_ref, g_ref, fz_ref, fzr_ref, bf_ref, bfr_ref, qg_ref, kg_ref,
                        pages[0], pages[1], pages[2], o_ref, kc_ref, lf_ref, n_pages, n_tok)


def _fox_sample_one(r, q_ref, k_ref, v_ref, g_ref, fz_ref, fzr_ref, bf_ref, bfr_ref, qg_ref, kg_ref,
                    k_pages, v_pages, f_pages, o_ref, kc_ref, lf_ref, n_pages, n_tok):
    n_rows = n_tok * N_HEADS

    qn = _heads_rmsnorm(q_ref[0, r], qg_ref[...])
    kn = _heads_rmsnorm(k_ref[0, r], kg_ref[...])
    kc_ref[r] = kn
    v_new = v_ref[0, r]
    lf_ref[r] = _log_sigmoid(fz_ref[0, r][:, :N_HEADS] + bf_ref[...])
    qrows = _query_rows(qn, n_tok) * (HEAD_DIM ** -0.5)
    qb = qrows.astype(BF16)
    row_q = lax.broadcasted_iota(jnp.int32, (n_rows, 1), 0) // N_HEADS

    upper = (lax.broadcasted_iota(jnp.int32, (PAGE_SIZE, PAGE_SIZE), 0)
             <= lax.broadcasted_iota(jnp.int32, (PAGE_SIZE, PAGE_SIZE), 1)).astype(F32)
    carry = jnp.zeros((n_rows, 1), F32)
    fk_tiles = []
    for pg in range(n_pages):
        rows_lf = jnp.concatenate([f_pages[pg][0]] * n_tok, axis=0)
        cs = jnp.dot(rows_lf, upper, precision=HIGHEST, preferred_element_type=F32) + carry
        fk_tiles.append(cs)
        carry = cs[:, PAGE_SIZE - 1:PAGE_SIZE]
    lfr = _log_sigmoid(fzr_ref[r] + bfr_ref[...])
    cum_new = []
    run = carry
    for tk in range(n_tok):
        run = run + lfr[:, tk:tk + 1]
        cum_new.append(run)
    fq = cum_new[n_tok - 1]
    for tk in range(n_tok - 2, -1, -1):
        fq = jnp.where(row_q == tk, cum_new[tk], fq)

    tiles = [jnp.dot(qb, k_pages[pg][0].astype(BF16), preferred_element_type=F32) + (fq - fk_tiles[pg])
             for pg in range(n_pages)]
    s_past = jnp.concatenate(tiles, axis=-1)
    s_new = []
    for kj in range(n_tok):
        dot = jnp.sum(qrows * kn[kj:kj + 1, :], axis=-1, keepdims=True)
        s_new.append(jnp.where(row_q >= kj, dot + (fq - cum_new[kj]), NEG))

    o = _finish_rows(s_past, s_new, v_pages, v_new, n_tok)
    o_ref[r] = (o * _silu(g_ref[0, r])).astype(BF16)


def _fox_sample(z3, n_seq, n_tok, cache_k, cache_v, cache_logf, page_table, b_forget, q_gain, k_gain):
    n_pages = page_table.shape[1]
    n_pool = cache_k.shape[0]
    seqs = math.gcd(n_seq, SAMPLE_SEQS_PER_STEP)
    z4 = z3.reshape(z3.shape[0], n_seq, n_tok, SEG)
    tok = lambda s: pl.BlockSpec((1, seqs, n_tok, SEG), lambda b, pt, s=s: (s, b, 0, 0))
    row = pl.BlockSpec((1, SEG), lambda b, pt: (0, 0))
    gain8 = lambda g: jnp.tile(g.astype(F32), N_HEADS).reshape(1, SEG)
    fz = z4[7, :, :, :N_HEADS]
    fz_rows = jnp.tile(fz.transpose(0, 2, 1), (1, n_tok, 1))
    bf = b_forget.astype(F32)
    grid_spec = pltpu.PrefetchScalarGridSpec(
        num_scalar_prefetch=1,
        grid=(n_seq // seqs,),
        in_specs=[tok(0), tok(1), tok(2), tok(3),
                  pl.BlockSpec((1, seqs, n_tok, PAIR), lambda b, pt: (7, b, 0, 0)),
                  pl.BlockSpec((seqs, n_tok * N_HEADS, n_tok), lambda b, pt: (b, 0, 0)),
                  pl.BlockSpec((1, N_HEADS), lambda b, pt: (0, 0)),
                  pl.BlockSpec((n_tok * N_HEADS, 1), lambda b, pt: (0, 0)),
                  row, row]
                 + _page_specs(n_pages, SEG, seqs) + _page_specs(n_pages, SEG, seqs)
                 + _page_specs(n_pages, N_HEADS, seqs),
        out_specs=[pl.BlockSpec((seqs, n_tok, SEG), lambda b, pt: (b, 0, 0)),
                   pl.BlockSpec((seqs, n_tok, SEG), lambda b, pt: (b, 0, 0)),
                   pl.BlockSpec((seqs, n_tok, N_HEADS), lambda b, pt: (b, 0, 0))])
    ck = _pages_token_minor(cache_k)
    cv = _pages_token_minor(cache_v)
    cf = cache_logf.astype(F32).transpose(0, 2, 1)
    return pl.pallas_call(
        functools.partial(_fox_sample_kernel, n_pages=n_pages, n_tok=n_tok, seqs=seqs),
        grid_spec=grid_spec,
        out_shape=[jax.ShapeDtypeStruct((n_seq, n_tok, SEG), BF16),
                   jax.ShapeDtypeStruct((n_seq, n_tok, SEG), F32),
                   jax.ShapeDtypeStruct((n_seq, n_tok, N_HEADS), F32)],
        compiler_params=_cparams(1),
        name="fox_sample",
    )(page_table, z4, z4, z4, z4, z4, fz_rows, bf.reshape(1, N_HEADS),
      jnp.tile(bf, n_tok).reshape(n_tok * N_HEADS, 1), gain8(q_gain), gain8(k_gain),
      *([ck] * (seqs * n_pages)), *([cv] * (seqs * n_pages)), *([cf] * (seqs * n_pages)))


FLAT = PAGE_SIZE * N_HEADS


def _flat_rmsnorm(x, gain):
    return x * lax.rsqrt(jnp.mean(x * x, axis=-1, keepdims=True) + EPS) * gain


def _flat_softmax_pv(tiles, v_pages, s_new, v_new):
    m = jnp.max(s_new, axis=-1, keepdims=True)
    for s in tiles:
        m = jnp.maximum(m, jnp.max(s, axis=-1, keepdims=True))
    p_new = jnp.exp(s_new - m)
    l = jnp.sum(p_new, axis=-1, keepdims=True)
    acc = jnp.dot(p_new, v_new, precision=HIGHEST, preferred_element_type=F32)
    for s, v_ref in zip(tiles, v_pages):
        pr = jnp.exp(s - m)
        l = l + jnp.sum(pr, axis=-1, keepdims=True)
        acc = acc + jnp.dot(pr.astype(BF16), v_ref[0].astype(BF16), preferred_element_type=F32)
    return acc / l


def _moba_flat_kernel(pt_ref, tab_ref, q_ref, k_ref, v_ref, g_ref, qg_ref, kg_ref, *rest, n_pages, n_tok):
    k_pages = rest[:n_pages]
    v_pages = rest[n_pages:2 * n_pages]
    o_ref, ka_ref = rest[2 * n_pages:]
    pages_per_block = MOBA_BLOCK // PAGE_SIZE
    nb_past = n_pages // pages_per_block
    past_len = n_pages * PAGE_SIZE
    n_rows = n_tok * N_HEADS
    scale = HEAD_DIM ** -0.5

    qn = _flat_rmsnorm(q_ref[0], qg_ref[...])
    kn = _flat_rmsnorm(k_ref[0], kg_ref[...])
    ka_ref[0] = kn
    tab = tab_ref[...]
    row = lax.broadcasted_iota(jnp.int32, (n_rows, 1), 0)
    row_q, row_h = row // N_HEADS, row % N_HEADS

    blk_lane = lax.broadcasted_iota(jnp.int32, (n_rows, nb_past), 1)
    gate = jnp.zeros((n_rows, nb_past), F32)
    for n in range(nb_past):
        tot = jnp.zeros((N_HEADS, HEAD_DIM), F32)
        for pg in range(n * pages_per_block, (n + 1) * pages_per_block):
            tot = tot + jnp.sum(k_pages[pg][0].reshape(PAGE_SIZE, N_HEADS, HEAD_DIM), axis=0)
        k_mean = jnp.concatenate([tot * (1.0 / MOBA_BLOCK)] * n_tok, axis=0)
        gate = jnp.where(blk_lane == n, jnp.sum(qn * k_mean, axis=-1, keepdims=True), gate)
    selpen = jnp.where(_top_blocks(gate, nb_past, nb_past, nb_past), 0.0, NEG)

    qb = (qn * scale).astype(BF16)
    col = lax.broadcasted_iota(jnp.int32, (n_rows, FLAT), 1)
    own_head = (col % N_HEADS) == row_h
    far = tab[:, T5_BUCKETS - 1:T5_BUCKETS]
    tiles = []
    for pg in range(n_pages):
        n = pg // pages_per_block
        s = _nt_dot(qb, k_pages[pg][0].astype(BF16))
        if past_len - (pg + 1) * PAGE_SIZE + 1 > T5_UPPER[-1]:
            bias = far
        else:
            bias = _t5_bias_rows(past_len + row_q - (pg * PAGE_SIZE + col // N_HEADS), tab)
        tiles.append(jnp.where(own_head, s + (bias + selpen[:, n:n + 1]), NEG))

    ncol = lax.broadcasted_iota(jnp.int32, (n_rows, n_rows), 1)
    rel = row_q - ncol // N_HEADS
    s_new = _nt_dot(qn * scale, kn, precision=HIGHEST) + _t5_bias_rows(rel, tab)
    s_new = jnp.where(((ncol % N_HEADS) == row_h) & (rel >= 0), s_new, NEG)

    o = _flat_softmax_pv(tiles, v_pages, s_new, v_ref[0])
    o_ref[0] = (o * _silu(g_ref[0])).astype(BF16)


def _flat_page_specs(n_pages, rows, width):
    return [pl.BlockSpec((1, rows, width), lambda b, pt, pg=pg: (pt[b, pg], 0, 0)) for pg in range(n_pages)]


def _flat_tokens(z3, seg, n_seq, n_tok):
    return z3[seg].reshape(n_seq, n_tok * N_HEADS, HEAD_DIM)


def _moba_flat(z3, n_seq, n_tok, cache_k, cache_v, page_table, t5_table, q_gain, k_gain):
    n_pages = page_table.shape[1]
    n_pool = cache_k.shape[0]
    assert MOBA_BLOCK % PAGE_SIZE == 0 and (n_pages * PAGE_SIZE) % MOBA_BLOCK == 0
    assert n_tok <= MOBA_BLOCK and n_pages // (MOBA_BLOCK // PAGE_SIZE) >= MOBA_TOPK
    n_rows = n_tok * N_HEADS
    tok = pl.BlockSpec((1, n_rows, HEAD_DIM), lambda b, pt: (b, 0, 0))
    gain = pl.BlockSpec((1, HEAD_DIM), lambda b, pt: (0, 0))
    tab_rows = jnp.tile(t5_table.astype(F32).T, (n_tok, 1))
    grid_spec = pltpu.PrefetchScalarGridSpec(
        num_scalar_prefetch=1,
        grid=(n_seq,),
        in_specs=[pl.BlockSpec((n_rows, T5_BUCKETS), lambda b, pt: (0, 0)), tok, tok, tok, tok, gain, gain]
                 + _flat_page_specs(n_pages, FLAT, HEAD_DIM) + _flat_page_specs(n_pages, FLAT, HEAD_DIM),
        out_specs=[tok, tok])
    ck = cache_k.reshape(n_pool, FLAT, HEAD_DIM)
    cv = cache_v.reshape(n_pool, FLAT, HEAD_DIM)
    flat = lambda seg: _flat_tokens(z3, seg, n_seq, n_tok)
    return pl.pallas_call(
        functools.partial(_moba_flat_kernel, n_pages=n_pages, n_tok=n_tok),
        grid_spec=grid_spec,
        out_shape=[jax.ShapeDtypeStruct((n_seq, n_rows, HEAD_DIM), BF16),
                   jax.ShapeDtypeStruct((n_seq, n_rows, HEAD_DIM), F32)],
        compiler_params=_cparams(1),
        name="moba_sample",
    )(page_table, tab_rows, flat(0), flat(1), flat(2), flat(3),
      q_gain.astype(F32).reshape(1, HEAD_DIM), k_gain.astype(F32).reshape(1, HEAD_DIM),
      *([ck] * n_pages), *([cv] * n_pages))


LOGF_CHUNK = PAIR // N_HEADS


def _fox_flat_kernel(pt_ref, q_ref, k_ref, v_ref, g_ref, fz_ref, fzr_ref, bf_ref, bfr_ref, qg_ref, kg_ref,
                     *rest, n_pages, n_tok):
    k_pages = rest[:n_pages]
    v_pages = rest[n_pages:2 * n_pages]
    f_pages = rest[2 * n_pages:3 * n_pages]
    o_ref, kc_ref, lf_ref = rest[3 * n_pages:]
    n_rows = n_tok * N_HEADS
    chunks = PAGE_SIZE // LOGF_CHUNK
    scale = HEAD_DIM ** -0.5

    qn = _flat_rmsnorm(q_ref[0], qg_ref[...])
    kn = _flat_rmsnorm(k_ref[0], kg_ref[...])
    kc_ref[0] = kn
    lf_ref[0] = _log_sigmoid(fz_ref[0] + bf_ref[...])
    row = lax.broadcasted_iota(jnp.int32, (n_rows, 1), 0)
    row_q, row_h = row // N_HEADS, row % N_HEADS

    lf = jnp.concatenate([f_pages[pg][0] for pg in range(n_pages)], axis=0)
    nr = lf.shape[0]
    li = lax.broadcasted_iota(jnp.int32, (PAIR, PAIR), 0)
    lj = lax.broadcasted_iota(jnp.int32, (PAIR, PAIR), 1)
    scan = ((li % N_HEADS == lj % N_HEADS) & (li <= lj)).astype(F32)
    last = (li == PAIR - N_HEADS + lj % N_HEADS).astype(F32)
    ri = lax.broadcasted_iota(jnp.int32, (nr, nr), 0)
    rj = lax.broadcasted_iota(jnp.int32, (nr, nr), 1)
    before = (rj < ri).astype(F32)
    in_row = jnp.dot(lf, scan, precision=HIGHEST, preferred_element_type=F32)
    totals = jnp.dot(in_row, last, precision=HIGHEST, preferred_element_type=F32)
    fk = in_row + jnp.dot(before, totals, precision=HIGHEST, preferred_element_type=F32)

    lane = lax.broadcasted_iota(jnp.int32, (n_rows, PAIR), 1)
    past_total = jnp.sum(jnp.where(lane == PAIR - N_HEADS + row_h, fk[nr - 1:nr, :], 0.0),
                         axis=-1, keepdims=True)
    lfr = _log_sigmoid(fzr_ref[0] + bfr_ref[...])
    cum_new = []
    run = past_total
    for tk in range(n_tok):
        run = run + lfr[:, tk:tk + 1]
        cum_new.append(run)
    fq = cum_new[n_tok - 1]
    for tk in range(n_tok - 2, -1, -1):
        fq = jnp.where(row_q == tk, cum_new[tk], fq)

    qb = (qn * scale).astype(BF16)
    col = lax.broadcasted_iota(jnp.int32, (n_rows, FLAT), 1)
    own_head = (col % N_HEADS) == row_h
    tiles = []
    for pg in range(n_pages):
        s = _nt_dot(qb, k_pages[pg][0].astype(BF16))
        fk_row = jnp.concatenate([fk[pg * chunks + c:pg * chunks + c + 1, :] for c in range(chunks)], axis=1)
        tiles.append(jnp.where(own_head, s + (fq - fk_row), NEG))

    ncol = lax.broadcasted_iota(jnp.int32, (n_rows, n_rows), 1)
    fk_new = cum_new[n_tok - 1]
    for tk in range(n_tok - 2, -1, -1):
        fk_new = jnp.where(ncol // N_HEADS == tk, cum_new[tk], fk_new)
    s_new = _nt_dot(qn * scale, kn, precision=HIGHEST) + (fq - fk_new)
    s_new = jnp.where(((ncol % N_HEADS) == row_h) & (ncol // N_HEADS <= row_q), s_new, NEG)

    o = _flat_softmax_pv(tiles, v_pages, s_new, v_ref[0])
    o_ref[0] = (o * _silu(g_ref[0])).astype(BF16)


def _fox_flat(z3, n_seq, n_tok, cache_k, cache_v, cache_logf, page_table, b_forget, q_gain, k_gain):
    n_pages = page_table.shape[1]
    n_pool = cache_k.shape[0]
    n_rows = n_tok * N_HEADS
    tok = pl.BlockSpec((1, n_rows, HEAD_DIM), lambda b, pt: (b, 0, 0))
    gain = pl.BlockSpec((1, HEAD_DIM), lambda b, pt: (0, 0))
    fz = z3[7].reshape(n_seq, n_tok, SEG)[:, :, :N_HEADS]
    fz_rows = jnp.tile(fz.transpose(0, 2, 1), (1, n_tok, 1))
    bf = b_forget.astype(F32)
    grid_spec = pltpu.PrefetchScalarGridSpec(
        num_scalar_prefetch=1,
        grid=(n_seq,),
        in_specs=[tok, tok, tok, tok,
                  pl.BlockSpec((1, n_tok, N_HEADS), lambda b, pt: (b, 0, 0)),
                  pl.BlockSpec((1, n_rows, n_tok), lambda b, pt: (b, 0, 0)),
                  pl.BlockSpec((1, N_HEADS), lambda b, pt: (0, 0)),
                  pl.BlockSpec((n_rows, 1), lambda b, pt: (0, 0)),
                  gain, gain]
                 + _flat_page_specs(n_pages, FLAT, HEAD_DIM) + _flat_page_specs(n_pages, FLAT, HEAD_DIM)
                 + _flat_page_specs(n_pages, PAGE_SIZE // LOGF_CHUNK, PAIR),
        out_specs=[tok, tok, pl.BlockSpec((1, n_tok, N_HEADS), lambda b, pt: (b, 0, 0))])
    ck = cache_k.reshape(n_pool, FLAT, HEAD_DIM)
    cv = cache_v.reshape(n_pool, FLAT, HEAD_DIM)
    cf = cache_logf.astype(F32).reshape(n_pool, PAGE_SIZE // LOGF_CHUNK, PAIR)
    flat = lambda seg: _flat_tokens(z3, seg, n_seq, n_tok)
    return pl.pallas_call(
        functools.partial(_fox_flat_kernel, n_pages=n_pages, n_tok=n_tok),
        grid_spec=grid_spec,
        out_shape=[jax.ShapeDtypeStruct((n_seq, n_rows, HEAD_DIM), BF16),
                   jax.ShapeDtypeStruct((n_seq, n_rows, HEAD_DIM), F32),
                   jax.ShapeDtypeStruct((n_seq, n_tok, N_HEADS), F32)],
        compiler_params=_cparams(1),
        name="fox_sample",
    )(page_table, flat(0), flat(1), flat(2), flat(3), fz, fz_rows, bf.reshape(1, N_HEADS),
      jnp.tile(bf, n_tok).reshape(n_rows, 1),
      q_gain.astype(F32).reshape(1, HEAD_DIM), k_gain.astype(F32).reshape(1, HEAD_DIM),
      *([ck] * n_pages), *([cv] * n_pages), *([cf] * n_pages))


CONV_SAMPLE_BATCH = 8


def _conv_sample_kernel(ua_ref, ub_ref, gd_ref, st_ref, w_ref, cb_ref, cg_ref, d_ref, cs_ref, buf_ref):
    n_tok = ua_ref.shape[2]
    hist = CONV_WIDTH - 1
    for e in range(ua_ref.shape[1]):
        buf_ref[0:hist, :] = st_ref[e]
        buf_ref[hist:hist + n_tok, :] = ua_ref[0, e] * jax.nn.sigmoid(ub_ref[0, e])
        y = jnp.zeros((n_tok, SEG), F32) + cb_ref[...]
        for k in range(CONV_WIDTH):
            y = y + w_ref[k:k + 1, :] * buf_ref[k:k + n_tok, :]
        mu = jnp.mean(y, axis=-1, keepdims=True)
        yc = y - mu
        var = jnp.mean(yc * yc, axis=-1, keepdims=True)
        yn = yc * lax.rsqrt(var + EPS) * cg_ref[...]
        d_ref[e] = (_silu(yn) * _silu(gd_ref[0, e])).astype(BF16)
        cs_ref[e] = buf_ref[n_tok:n_tok + hist, :]


def _conv_sample(z3, n_seq, n_tok, state, conv_w, conv_b, conv_gain):
    bt = math.gcd(n_seq, CONV_SAMPLE_BATCH)
    hist = CONV_WIDTH - 1
    z4 = z3.reshape(z3.shape[0], n_seq, n_tok, SEG)
    tok = lambda s: pl.BlockSpec((1, bt, n_tok, SEG), lambda b, s=s: (s, b, 0, 0))
    row = pl.BlockSpec((1, SEG), lambda b: (0, 0))
    return pl.pallas_call(
        _conv_sample_kernel,
        grid=(n_seq // bt,),
        in_specs=[tok(4), tok(5), tok(6),
                  pl.BlockSpec((bt, hist, SEG), lambda b: (b, 0, 0)),
                  pl.BlockSpec((CONV_WIDTH, SEG), lambda b: (0, 0)), row, row],
        out_specs=[pl.BlockSpec((bt, n_tok, SEG), lambda b: (b, 0, 0)),
                   pl.BlockSpec((bt, hist, SEG), lambda b: (b, 0, 0))],
        out_shape=[jax.ShapeDtypeStruct((n_seq, n_tok, SEG), BF16),
                   jax.ShapeDtypeStruct((n_seq, hist, SEG), F32)],
        scratch_shapes=[pltpu.VMEM((hist + n_tok + 6, SEG), F32)],
        compiler_params=_cparams(1),
        name="conv_sample",
    )(z4, z4, z4, state.astype(F32), conv_w.astype(F32), conv_b.astype(F32).reshape(1, SEG),
      conv_gain.astype(F32).reshape(1, SEG))


RET_SAMPLE_GROUPS = 8


def _even_layer_sample(x, cache_k, cache_v, ret_state, page_table, norm_g, w_in, q_gain, k_gain, t5_table,
                       ret_gain, w_out):
    b, l, d = x.shape
    x2d = x.reshape(b * l, d)
    w3, w2 = _even_weights(w_in, w_out)
    z3 = _proj_in(x2d, norm_g, w3)
    oa, ka = _moba_sample(z3, b, l, cache_k, cache_v, page_table, t5_table, q_gain, k_gain)
    past_len = page_table.shape[1] * PAGE_SIZE
    groups = math.gcd(b, RET_SAMPLE_GROUPS)
    orr, sp = _retention(z3, b, l, l * groups, groups, past_len, ret_gain, _pair_blockdiag(ret_state), F32,
                         "retention_sample")
    y = _proj_out(oa.reshape(b * l, SEG), orr, x2d, w2)
    return (y.reshape(b, l, d), ka.reshape(b, l, N_HEADS, HEAD_DIM),
            z3[2].reshape(b, l, N_HEADS, HEAD_DIM), _pair_unblockdiag(sp))


def _odd_layer_sample(x, cache_k, cache_v, cache_logf, conv_state, page_table, norm_g, w_in, b_forget, q_gain,
                      k_gain, conv_w, conv_b, conv_gain, w_out):
    b, l, d = x.shape
    x2d = x.reshape(b * l, d)
    w3, w2 = _odd_weights(w_in, w_out)
    z3 = _proj_in(x2d, norm_g, w3)
    oc, kc, logf = _fox_sample(z3, b, l, cache_k, cache_v, cache_logf, page_table, b_forget, q_gain, k_gain)
    dd, cs = _conv_sample(z3, b, l, conv_state, conv_w, conv_b, conv_gain)
    y = _proj_out(oc.reshape(b * l, SEG), dd.reshape(b * l, SEG), x2d, w2)
    return (y.reshape(b, l, d), kc.reshape(b, l, N_HEADS, HEAD_DIM),
            z3[2].reshape(b, l, N_HEADS, HEAD_DIM), logf, cs)


def kernel(x_prompt, x_sample, cache_moba_k, cache_moba_v, cache_fox_k, cache_fox_v, cache_fox_logf, state_ret, state_conv, page_table, norm_g_even, w_in_even, moba_q_gain, moba_k_gain, t5_table, ret_gain, w_out_even, norm_g_odd, w_in_odd, b_forget, fox_q_gain, fox_k_gain, conv_w, conv_b, conv_gain, w_out_odd):
    depth = norm_g_even.shape[0] + norm_g_odd.shape[0]
    xp, xs = x_prompt, x_sample
    outs = {name: [] for name in ("mk_p", "mv_p", "mk_s", "mv_s", "rs_p", "rs_s",
                                  "fk_p", "fv_p", "fl_p", "fk_s", "fv_s", "fl_s", "cs_p", "cs_s")}
    for layer in range(depth):
        i = layer // 2
        if layer % 2 == 0:
            w = (norm_g_even[i], w_in_even[i], moba_q_gain[i], moba_k_gain[i], t5_table, ret_gain[i], w_out_even[i])
            xp, k1, v1, s1 = _even_layer_prompt(xp, *w)
            xs, k2, v2, s2 = _even_layer_sample(xs, cache_moba_k[i], cache_moba_v[i], state_ret[i], page_table, *w)
            for name, val in zip(("mk_p", "mv_p", "rs_p", "mk_s", "mv_s", "rs_s"), (k1, v1, s1, k2, v2, s2)):
                outs[name].append(val)
        else:
            w = (norm_g_odd[i], w_in_odd[i], b_forget[i], fox_q_gain[i], fox_k_gain[i],
                 conv_w[i], conv_b[i], conv_gain[i], w_out_odd[i])
            xp, k1, v1, f1, c1 = _odd_layer_prompt(xp, *w)
            xs, k2, v2, f2, c2 = _odd_layer_sample(xs, cache_fox_k[i], cache_fox_v[i], cache_fox_logf[i],
                                                   state_conv[i], page_table, *w)
            for name, val in zip(("fk_p", "fv_p", "fl_p", "cs_p", "fk_s", "fv_s", "fl_s", "cs_s"),
                                 (k1, v1, f1, c1, k2, v2, f2, c2)):
                outs[name].append(val)
    st = lambda name: jnp.stack(outs[name])
    return (xp, xs, st("mk_p"), st("mv_p"), st("mk_s"), st("mv_s"), st("rs_p"), st("rs_s"),
            st("fk_p"), st("fv_p"), st("fl_p"), st("fk_s"), st("fv_s"), st("fl_s"), st("cs_p"), st("cs_s"))
```
